```python
import jax, jax.numpy as jnp
from jax import lax
import numpy as np

D_MODEL = 4096
BATCH = 4
SEQ = 2048
DEPTH = 2

MIX_WIDTH = D_MODEL
EPS = 1e-6
Q_BLOCK = 128
GLA_HEADS = 4
GLA_DK = D_MODEL // 16
GLA_DV = D_MODEL // 8
GLA_QK = GLA_HEADS * GLA_DK
GLA_V = GLA_HEADS * GLA_DV
GLA_GATE_RANK = 16
GLA_GATE_TAU = 16.0
GLA_CHUNK = 64
FOX_HEAD_DIM = 128
FOX_W = MIX_WIDTH // 2
FOX_HEADS = FOX_W // FOX_HEAD_DIM
RG_WIDTH = MIX_WIDTH // 2
RG_BLOCK = 256
RG_BLOCKS = RG_WIDTH // RG_BLOCK
RG_CONV = 4
RG_C = 8.0
SB_HEAD_DIM = 128
SB_W = MIX_WIDTH // 2
SB_HEADS = SB_W // SB_HEAD_DIM
D_FF = -(-8 * D_MODEL // (3 * 256)) * 256
N_EVEN = (DEPTH + 1) // 2
N_ODD = DEPTH // 2
AB_SIZES = (GLA_QK, GLA_QK, GLA_V, GLA_V, GLA_GATE_RANK, FOX_W, FOX_W, FOX_W, FOX_HEADS)
CD_SIZES = (RG_WIDTH, RG_WIDTH, SB_W, SB_W, SB_W)
AB_IN = sum(AB_SIZES)
CD_IN = sum(CD_SIZES)
AB_OUT = GLA_V + FOX_W
CD_OUT = RG_WIDTH + SB_W

kernel_name = 'hybrid_gla_fox_rglru_stickbreaking_block'


def rmsnorm(x, g):
    xf = x.astype(jnp.float32)
    y = xf * lax.rsqrt(jnp.mean(xf * xf, axis=-1, keepdims=True) + EPS)
    return (y * g.astype(jnp.float32)).astype(x.dtype)


def split_cols(t, sizes):
    idx = [int(v) for v in np.cumsum(sizes)[:-1]]
    return jnp.split(t, idx, axis=-1)


def gla_chunked(q, k, v, log_a):
    B, S, H, K = q.shape
    V = v.shape[-1]
    n = S // GLA_CHUNK

    def to_chunks(t):
        return t.reshape(B, n, GLA_CHUNK, H, t.shape[-1]).transpose(1, 0, 3, 2, 4).astype(jnp.float32)

    qc, kc, vc, ac = to_chunks(q), to_chunks(k), to_chunks(v), to_chunks(log_a)
    bc = jnp.cumsum(ac, axis=-2)
    causal = jnp.tril(jnp.ones((GLA_CHUNK, GLA_CHUNK), bool))

    def step(state, inp):
        qi, ki, vi, bi = inp
        inter = jnp.einsum('bhtk,bhkv->bhtv', qi * jnp.exp(bi), state)
        diff = bi[:, :, :, None, :] - bi[:, :, None, :, :]
        decay = jnp.exp(jnp.where(causal[:, :, None], diff, -jnp.inf))
        scores = jnp.einsum('bhtk,bhsk,bhtsk->bhts', qi, ki, decay)
        intra = jnp.einsum('bhts,bhsv->bhtv', scores, vi)
        b_last = bi[:, :, -1:, :]
        k_dec = ki * jnp.exp(b_last - bi)
        state = jnp.exp(b_last[:, :, 0, :])[..., None] * state + jnp.einsum('bhsk,bhsv->bhkv', k_dec, vi)
        return state, inter + intra

    state0 = jnp.zeros((B, H, K, V), jnp.float32)
    _, out = lax.scan(step, state0, (qc, kc, vc, bc))
    return out.transpose(1, 0, 3, 2, 4).reshape(B, S, H, V)


def forgetting_attention(q, k, v, log_f):
    B, S, H, Dh = q.shape
    n = S // Q_BLOCK
    c = jnp.cumsum(log_f.astype(jnp.float32), axis=1).transpose(0, 2, 1)
    qb = q.reshape(B, n, Q_BLOCK, H, Dh).transpose(1, 0, 2, 3, 4)
    cb = c.reshape(B, H, n, Q_BLOCK).transpose(2, 0, 1, 3)
    pos_k = jnp.arange(S)
    scale = Dh ** -0.5

    def block(args):
        qi, ci, i = args
        pos_q = i * Q_BLOCK + jnp.arange(Q_BLOCK)
        logits = (jnp.einsum('bqhd,bkhd->bhqk', qi, k).astype(jnp.float32) * scale
                  + ci[:, :, :, None] - c[:, :, None, :])
        logits = jnp.where(pos_k[None, :] <= pos_q[:, None], logits, -jnp.inf)
        p = jax.nn.softmax(logits, axis=-1)
        return jnp.einsum('bhqk,bkhd->bqhd', p.astype(v.dtype), v)

    out = lax.map(block, (qb, cb, jnp.arange(n)))
    return out.transpose(1, 0, 2, 3, 4).reshape(B, S, H, Dh)


def stick_breaking_attention(q, k, v):
    B, S, H, Dh = q.shape
    n = S // Q_BLOCK
    qb = q.reshape(B, n, Q_BLOCK, H, Dh).transpose(1, 0, 2, 3, 4)
    pos_k = jnp.arange(S)
    scale = Dh ** -0.5

    def block(args):
        qi, i = args
        pos_q = i * Q_BLOCK + jnp.arange(Q_BLOCK)
        mask = (pos_k[None, :] < pos_q[:, None])[None, None]
        z = jnp.einsum('bqhd,bkhd->bhqk', qi, k).astype(jnp.float32) * scale
        log_beta = jax.nn.log_sigmoid(z)
        log_1m = jnp.where(mask, jax.nn.log_sigmoid(-z), 0.0)
        suffix = lax.cumsum(log_1m, axis=3, reverse=True) - log_1m
        a = jnp.where(mask, jnp.exp(log_beta + suffix), 0.0)
        return jnp.einsum('bhqk,bkhd->bqhd', a.astype(v.dtype), v)

    out = lax.map(block, (qb, jnp.arange(n)))
    return out.transpose(1, 0, 2, 3, 4).reshape(B, S, H, Dh)


def rg_lru_branch(xb, conv_w, conv_b, w_x, b_x, w_a, b_a, lam):
    B, S, W = xb.shape
    xc = lax.conv_general_dilated(xb, conv_w[:, None, :].astype(xb.dtype), window_strides=(1,),
                                  padding=[(RG_CONV - 1, 0)], dimension_numbers=('NWC', 'WIO', 'NWC'),
                                  feature_group_count=W) + conv_b
    xg = xc.reshape(B, S, RG_BLOCKS, RG_BLOCK)
    gate_x = jax.nn.sigmoid(jnp.einsum('bsgi,gij->bsgj', xg, w_x).reshape(B, S, W) + b_x)
    gate_a = jax.nn.sigmoid(jnp.einsum('bsgi,gij->bsgj', xg, w_a).reshape(B, S, W) + b_a)
    log_a = -RG_C * gate_a.astype(jnp.float32) * jax.nn.softplus(-lam.astype(jnp.float32))
    a = jnp.exp(log_a)
    u = jnp.sqrt(-jnp.expm1(2.0 * log_a)) * (gate_x * xc).astype(jnp.float32)

    def combine(left, right):
        a_l, b_l = left
        a_r, b_r = right
        return a_l * a_r, a_r * b_l + b_r

    _, h = lax.associative_scan(combine, (a, u), axis=1)
    return h.astype(xb.dtype)


def even_mixer(u, w_in, gla_w_gate_up, gla_b_gate, gla_norm, fox_b_f, w_out):
    B, S, _ = u.shape
    proj = u @ w_in
    gq, gk, gv, gg, glr, fq, fk, fv, ff = split_cols(proj, AB_SIZES)
    gq = gq.reshape(B, S, GLA_HEADS, GLA_DK) * (GLA_DK ** -0.5)
    gk = gk.reshape(B, S, GLA_HEADS, GLA_DK)
    gv = gv.reshape(B, S, GLA_HEADS, GLA_DV)
    log_alpha = jax.nn.log_sigmoid((glr @ gla_w_gate_up + gla_b_gate).astype(jnp.float32)) / GLA_GATE_TAU
    log_alpha = log_alpha.reshape(B, S, GLA_HEADS, GLA_DK)
    o_gla = gla_chunked(gq, gk, gv, log_alpha).astype(u.dtype)
    o_gla = rmsnorm(o_gla, gla_norm) * jax.nn.silu(gg.reshape(B, S, GLA_HEADS, GLA_DV))
    o_gla = o_gla.reshape(B, S, GLA_V)
    fq = fq.reshape(B, S, FOX_HEADS, FOX_HEAD_DIM)
    fk = fk.reshape(B, S, FOX_HEADS, FOX_HEAD_DIM)
    fv = fv.reshape(B, S, FOX_HEADS, FOX_HEAD_DIM)
    log_f = jax.nn.log_sigmoid((ff + fox_b_f).astype(jnp.float32))
    o_fox = forgetting_attention(fq, fk, fv, log_f).reshape(B, S, FOX_W)
    return jnp.concatenate([o_gla, o_fox], axis=-1) @ w_out


def odd_mixer(u, w_in, conv_w, conv_b, w_x, b_x, w_a, b_a, lam, w_out):
    B, S, _ = u.shape
    proj = u @ w_in
    ry, rx, sq, sk, sv = split_cols(proj, CD_SIZES)
    o_rg = rg_lru_branch(rx, conv_w, conv_b, w_x, b_x, w_a, b_a, lam) * jax.nn.gelu(ry)
    sq = sq.reshape(B, S, SB_HEADS, SB_HEAD_DIM)
    sk = sk.reshape(B, S, SB_HEADS, SB_HEAD_DIM)
    sv = sv.reshape(B, S, SB_HEADS, SB_HEAD_DIM)
    o_sb = stick_breaking_attention(sq, sk, sv).reshape(B, S, SB_W)
    return jnp.concatenate([o_rg, o_sb], axis=-1) @ w_out


def swiglu(u, w_gate, w_up, w_down):
    return (jax.nn.silu(u @ w_gate) * (u @ w_up)) @ w_down


def setup_inputs(seed: int = 0) -> dict:
    key = jax.random.key(seed)
    ks = jax.random.split(key, 24)
    f32 = jnp.float32

    def nrm(k, shape, fan_in):
        return jax.random.normal(k, shape, f32) * (fan_in ** -0.5)

    def gain(k, shape):
        return 1.0 + 0.02 * jax.random.normal(k, shape, f32)

    def small(k, shape):
        return 0.01 * jax.random.normal(k, shape, f32)

    u = jax.random.uniform(ks[21], (N_ODD, RG_WIDTH), f32, minval=0.9, maxval=0.999)
    p = u ** (1.0 / RG_C)
    rg_lambda = jnp.log(p) - jnp.log1p(-p)
    return {
        'x': jax.random.normal(ks[0], (BATCH, SEQ, D_MODEL), f32),
        'norm_mix': gain(ks[1], (DEPTH, D_MODEL)),
        'norm_ffn': gain(ks[2], (DEPTH, D_MODEL)),
        'ffn_w_gate': nrm(ks[3], (DEPTH, D_MODEL, D_FF), D_MODEL),
        'ffn_w_up': nrm(ks[4], (DEPTH, D_MODEL, D_FF), D_MODEL),
        'ffn_w_down': nrm(ks[5], (DEPTH, D_FF, D_MODEL), D_FF),
        'ab_w_in': nrm(ks[6], (N_EVEN, D_MODEL, AB_IN), D_MODEL),
        'gla_w_gate_up': nrm(ks[7], (N_EVEN, GLA_GATE_RANK, GLA_QK), GLA_GATE_RANK),
        'gla_b_gate': small(ks[8], (N_EVEN, GLA_QK)),
        'gla_norm': gain(ks[9], (N_EVEN, GLA_DV)),
        'fox_b_f': 3.0 + 0.5 * jax.random.normal(ks[10], (N_EVEN, FOX_HEADS), f32),
        'ab_w_out': nrm(ks[11], (N_EVEN, AB_OUT, D_MODEL), AB_OUT),
        'cd_w_in': nrm(ks[12], (N_ODD, D_MODEL, CD_IN), D_MODEL),
        'rg_conv_w': nrm(ks[13], (N_ODD, RG_CONV, RG_WIDTH), RG_CONV),
        'rg_conv_b': small(ks[14], (N_ODD, RG_WIDTH)),
        'rg_w_x': nrm(ks[15], (N_ODD, RG_BLOCKS, RG_BLOCK, RG_BLOCK), RG_BLOCK),
        'rg_b_x': small(ks[16], (N_ODD, RG_WIDTH)),
        'rg_w_a': nrm(ks[17], (N_ODD, RG_BLOCKS, RG_BLOCK, RG_BLOCK), RG_BLOCK),
        'rg_b_a': small(ks[18], (N_ODD, RG_WIDTH)),
        'rg_lambda': rg_lambda,
        'cd_w_out': nrm(ks[19], (N_ODD, CD_OUT, D_MODEL), CD_OUT),
        'final_norm': gain(ks[20], (D_MODEL,)),
    }


def reference(x, norm_mix, norm_ffn, ffn_w_gate, ffn_w_up, ffn_w_down, ab_w_in, gla_w_gate_up,
              gla_b_gate, gla_norm, fox_b_f, ab_w_out, cd_w_in, rg_conv_w, rg_conv_b, rg_w_x,
              rg_b_x, rg_w_a, rg_b_a, rg_lambda, cd_w_out, final_norm):
    h = x
    for layer in range(DEPTH):
        i = layer // 2
        u = rmsnorm(h, norm_mix[layer])
        if layer % 2 == 0:
            h = h + even_mixer(u, ab_w_in[i], gla_w_gate_up[i], gla_b_gate[i], gla_norm[i],
                               fox_b_f[i], ab_w_out[i])
        else:
            h = h + odd_mixer(u, cd_w_in[i], rg_conv_w[i], rg_conv_b[i], rg_w_x[i], rg_b_x[i],
                              rg_w_a[i], rg_b_a[i], rg_lambda[i], cd_w_out[i])
        u = rmsnorm(h, norm_ffn[layer])
        h = h + swiglu(u, ffn_w_gate[layer], ffn_w_up[layer], ffn_w_down[layer])
    return rmsnorm(h, final_norm)
```

```python
import functools
import math

import jax
import jax.numpy as jnp
from jax import lax
from jax.experimental import pallas as pl
from jax.experimental.pallas import tpu as pltpu

F32 = jnp.float32
BF16 = jnp.bfloat16

EPS = 1e-6
LANES = 128
VMEM_LIMIT = 56 * 1024 * 1024

GLA_HEADS = 4
GLA_GATE_RANK = 16
GLA_GATE_TAU = 16.0
GLA_CHUNK = 64
GLA_SUB = 16
FOX_HEAD_DIM = 128
SB_HEAD_DIM = 128
RG_BLOCK = 256
RG_CONV = 4
RG_C = 8.0

NEG_BIG = -1e30
_NT = (((1,), (1,)), ((), ()))


def _params(*sem):
    return pltpu.CompilerParams(dimension_semantics=sem, vmem_limit_bytes=VMEM_LIMIT)


def _dot(a, b):
    return jnp.dot(a, b, preferred_element_type=F32)


def _dot_nt(a, b):
    return lax.dot_general(a, b, _NT, preferred_element_type=F32)


def _split_bf16(x, n):
    parts = []
    r = x
    for _ in range(n):
        p = r.astype(BF16)
        parts.append(p)
        r = r - p.astype(F32)
    return parts


def _dot_f32(a, b):
    ah, al = _split_bf16(a, 2)
    bh, bl = _split_bf16(b, 2)
    return _dot(ah, bh) + _dot(ah, bl) + _dot(al, bh)


def _dot_exact_lhs(m01, x, n=3):
    acc = None
    for p in _split_bf16(x, n):
        t = _dot(m01, p)
        acc = t if acc is None else acc + t
    return acc


def _softplus_neg_abs(z):
    return jnp.log1p(jnp.exp(-jnp.abs(z)))


def _log_sigmoid(z):
    return jnp.minimum(z, 0.0) - _softplus_neg_abs(z)


def _sigmoid(z):
    return 1.0 / (1.0 + jnp.exp(-z))


def _gelu_tanh(y):
    c = math.sqrt(2.0 / math.pi)
    return y * (0.5 * (1.0 + jnp.tanh(c * (y + 0.044715 * (y * y * y)))))


def _rmsnorm_kernel(x_ref, g_ref, o_ref):
    x = x_ref[...]
    ms = jnp.mean(x * x, axis=-1, keepdims=True)
    o_ref[...] = (x * lax.rsqrt(ms + EPS) * g_ref[...]).astype(o_ref.dtype)


def _rmsnorm(x, g, out_dtype, tm=512):
    t, d = x.shape
    return pl.pallas_call(
        _rmsnorm_kernel,
        grid=(t // tm,),
        in_specs=[pl.BlockSpec((tm, d), lambda i: (i, 0)),
                  pl.BlockSpec((1, d), lambda i: (0, 0))],
        out_specs=pl.BlockSpec((tm, d), lambda i: (i, 0)),
        out_shape=jax.ShapeDtypeStruct((t, d), out_dtype),
        compiler_params=_params("parallel"),
        name="rmsnorm",
    )(x, g.reshape(1, d))


def _gate_proj_kernel(x_ref, g_ref, w_ref, o_ref):
    x = x_ref[...]
    ms = jnp.mean(x * x, axis=-1, keepdims=True)
    u = x * lax.rsqrt(ms + EPS) * g_ref[...]
    o_ref[...] = _dot_f32(u, w_ref[...])


def _gate_proj(x, g, w_small, tm=256):
    t, d = x.shape
    n = w_small.shape[1]
    return pl.pallas_call(
        _gate_proj_kernel,
        grid=(t // tm,),
        in_specs=[pl.BlockSpec((tm, d), lambda i: (i, 0)),
                  pl.BlockSpec((1, d), lambda i: (0, 0)),
                  pl.BlockSpec((d, n), lambda i: (0, 0))],
        out_specs=pl.BlockSpec((tm, n), lambda i: (i, 0)),
        out_shape=jax.ShapeDtypeStruct((t, n), F32),
        compiler_params=_params("parallel"),
        name="gate_proj",
    )(x, g.reshape(1, d), w_small)


_CAST_ROWS = 512


def _cast_rows(w_ref, wb_ref):
    k = w_ref.shape[0]

    def body(r, c):
        rows = pl.ds(pl.multiple_of(r * _CAST_ROWS, _CAST_ROWS), _CAST_ROWS)
        wb_ref[rows, :] = w_ref[rows, :].astype(BF16)
        return c

    lax.fori_loop(0, k // _CAST_ROWS, body, 0)


def _mm_kernel(a_ref, w_ref, o_ref, wb_ref):
    @pl.when(pl.program_id(1) == 0)
    def _():
        _cast_rows(w_ref, wb_ref)

    o_ref[...] = _dot(a_ref[...], wb_ref[...]).astype(o_ref.dtype)


def _mm_res_kernel(a_ref, w_ref, r_ref, o_ref, wb_ref):
    @pl.when(pl.program_id(1) == 0)
    def _():
        _cast_rows(w_ref, wb_ref)

    o_ref[...] = r_ref[...] + _dot(a_ref[...], wb_ref[...])


def _mm_res_bf16w_kernel(a_ref, w_ref, r_ref, o_ref):
    o_ref[...] = r_ref[...] + _dot(a_ref[...], w_ref[...])


def _mm_swiglu_kernel(a_ref, wg_ref, wu_ref, o_ref, wgb_ref, wub_ref):
    @pl.when(pl.program_id(1) == 0)
    def _():
        _cast_rows(wg_ref, wgb_ref)
        _cast_rows(wu_ref, wub_ref)

    a = a_ref[...]
    g = _dot(a, wgb_ref[...])
    u = _dot(a, wub_ref[...])
    o_ref[...] = (g * _sigmoid(g) * u).astype(o_ref.dtype)


def _in_proj_shift_kernel(a_ref, w_ref, wx_ref, o_ref, wb_ref, *, n_plain, shift):
    j = pl.program_id(0)
    first = pl.program_id(1) == 0
    k, tn = w_ref.shape
    width = tn + wx_ref.shape[1]

    @pl.when(first & (j < n_plain))
    def _():
        _cast_rows(w_ref, wb_ref)

    @pl.when(first & (j >= n_plain))
    def _():
        def body(r, c):
            rows = pl.ds(pl.multiple_of(r * _CAST_ROWS, _CAST_ROWS), _CAST_ROWS)
            cat = jnp.concatenate([w_ref[rows, :], wx_ref[rows, :]], axis=1)
            wb_ref[rows, :] = pltpu.roll(cat, width - shift, axis=1)[:, :tn].astype(BF16)
            return c

        lax.fori_loop(0, k // _CAST_ROWS, body, 0)

    o_ref[...] = _dot(a_ref[...], wb_ref[...]).astype(o_ref.dtype)


def _matmul(a, w, *, tm, tn, out_dtype, n_out=None):
    m, k = a.shape
    n = w.shape[1] if n_out is None else n_out
    return pl.pallas_call(
        _mm_kernel,
        grid=(n // tn, m // tm),
        in_specs=[pl.BlockSpec((tm, k), lambda j, i: (i, 0)),
                  pl.BlockSpec((k, tn), lambda j, i: (0, j))],
        out_specs=pl.BlockSpec((tm, tn), lambda j, i: (i, j)),
        out_shape=jax.ShapeDtypeStruct((m, n), out_dtype),
        scratch_shapes=[pltpu.VMEM((k, tn), BF16)],
        compiler_params=_params("arbitrary", "arbitrary"),
        name="matmul",
    )(a, w)


def _matmul_res(a, w, res, *, tm, tn):
    m, k = a.shape
    n = w.shape[1]
    cast = w.dtype != BF16
    return pl.pallas_call(
        _mm_res_kernel if cast else _mm_res_bf16w_kernel,
        grid=(n // tn, m // tm),
        in_specs=[pl.BlockSpec((tm, k), lambda j, i: (i, 0)),
                  pl.BlockSpec((k, tn), lambda j, i: (0, j)),
                  pl.BlockSpec((tm, tn), lambda j, i: (i, j))],
        out_specs=pl.BlockSpec((tm, tn), lambda j, i: (i, j)),
        out_shape=jax.ShapeDtypeStruct((m, n), F32),
        scratch_shapes=[pltpu.VMEM((k, tn), BF16)] if cast else [],
        compiler_params=_params("arbitrary", "arbitrary"),
        name="matmul_res" if cast else "matmul_res_bf16w",
    )(a, w, res)


def _matmul_swiglu(a, wg, wu, *, tm, tn):
    m, k = a.shape
    n = wg.shape[1]
    return pl.pallas_call(
        _mm_swiglu_kernel,
        grid=(n // tn, m // tm),
        in_specs=[pl.BlockSpec((tm, k), lambda j, i: (i, 0)),
                  pl.BlockSpec((k, tn), lambda j, i: (0, j)),
                  pl.BlockSpec((k, tn), lambda j, i: (0, j))],
        out_specs=pl.BlockSpec((tm, tn), lambda j, i: (i, j)),
        out_shape=jax.ShapeDtypeStruct((m, n), BF16),
        scratch_shapes=[pltpu.VMEM((k, tn), BF16), pltpu.VMEM((k, tn), BF16)],
        compiler_params=_params("arbitrary", "arbitrary"),
        name="matmul_swiglu",
    )(a, wg, wu)


def _in_proj_shift(a, w, *, tm, tn, n_plain, n_total, shift):
    m, k = a.shape
    nx = tn // LANES
    kern = functools.partial(_in_proj_shift_kernel, n_plain=n_plain, shift=shift)
    return pl.pallas_call(
        kern,
        grid=(n_total, m // tm),
        in_specs=[pl.BlockSpec((tm, k), lambda j, i: (i, 0)),
                  pl.BlockSpec((k, tn), lambda j, i: (0, j)),
                  pl.BlockSpec((k, LANES), lambda j, i: (0, (j + 1) * nx))],
        out_specs=pl.BlockSpec((tm, tn), lambda j, i: (i, j)),
        out_shape=jax.ShapeDtypeStruct((m, n_total * tn), BF16),
        scratch_shapes=[pltpu.VMEM((k, tn), BF16)],
        compiler_params=_params("arbitrary", "arbitrary"),
        name="in_proj_shift",
    )(a, w, w)


def _fox_cumsum_kernel(g_ref, b_ref, c_ref, ct_ref, c_sc, *, blk, lane0, heads):
    s = g_ref.shape[0]
    row = lax.broadcasted_iota(jnp.int32, (blk, blk), 0)
    col = lax.broadcasted_iota(jnp.int32, (blk, blk), 1)
    tri = (col <= row).astype(BF16)
    carry = jnp.zeros((1, g_ref.shape[1]), F32)
    for n in range(s // blk):
        rows = slice(n * blk, (n + 1) * blk)
        lf = _log_sigmoid(g_ref[rows, :] + b_ref[...])
        c = _dot_exact_lhs(tri, lf) + carry
        c_sc[rows, :] = c
        carry = c[blk - 1:blk, :]
    c_all = c_sc[...]
    c_ref[...] = c_all
    ct_ref[...] = c_all.T[lane0:lane0 + heads, :]


def _fox_cumsum(gates, bias_row, *, batch, seq, lane0, heads, blk=256):
    t, n = gates.shape
    kern = functools.partial(_fox_cumsum_kernel, blk=blk, lane0=lane0, heads=heads)
    return pl.pallas_call(
        kern,
        grid=(batch,),
        in_specs=[pl.BlockSpec((seq, n), lambda b: (b, 0)),
                  pl.BlockSpec((1, n), lambda b: (0, 0))],
        out_specs=[pl.BlockSpec((seq, n), lambda b: (b, 0)),
                   pl.BlockSpec((None, heads, seq), lambda b: (b, 0, 0))],
        out_shape=[jax.ShapeDtypeStruct((t, n), F32),
                   jax.ShapeDtypeStruct((batch, heads, seq), F32)],
        scratch_shapes=[pltpu.VMEM((seq, n), F32)],
        compiler_params=_params("parallel"),
        name="fox_cumsum",
    )(gates, bias_row)


def _fox_kernel(q_ref, k_ref, v_ref, c_ref, ck_ref, o_ref, *, tq, lane0):
    h = pl.program_id(1)
    qi = pl.program_id(2)
    dh = q_ref.shape[1]
    scale = dh ** -0.5
    q = q_ref[...]
    lane = lax.broadcasted_iota(jnp.int32, c_ref.shape, 1)
    cq = jnp.sum(jnp.where(lane == lane0 + h, c_ref[...], 0.0), axis=1, keepdims=True)

    def logits(kb):
        rows = pl.ds(pl.multiple_of(kb * tq, tq), tq)
        s = _dot_nt(q, k_ref[rows, :]) * scale
        return s + cq - ck_ref[kb], v_ref[rows, :]

    s, v = logits(qi)
    tpos = lax.broadcasted_iota(jnp.int32, (tq, tq), 0)
    spos = lax.broadcasted_iota(jnp.int32, (tq, tq), 1)
    s = jnp.where(spos <= tpos, s, NEG_BIG)
    m0 = jnp.max(s, axis=1, keepdims=True)
    p = jnp.exp(s - m0)
    l0 = jnp.sum(p, axis=1, keepdims=True)
    acc0 = _dot(p.astype(BF16), v)

    def body(kb, carry):
        m, l, acc = carry
        s, v = logits(kb)
        m_new = jnp.maximum(m, jnp.max(s, axis=1, keepdims=True))
        alpha = jnp.exp(m - m_new)
        p = jnp.exp(s - m_new)
        l = alpha * l + jnp.sum(p, axis=1, keepdims=True)
        acc = alpha * acc + _dot(p.astype(BF16), v)
        return m_new, l, acc

    _, l, acc = lax.fori_loop(0, qi, body, (m0, l0, acc0))
    o_ref[...] = (acc / l).astype(o_ref.dtype)


def _fox_attention(proj, c, ck, *, batch, seq, heads, col0, lane0, tq=256):
    t = proj.shape[0]
    dh = FOX_HEAD_DIM
    nq = seq // tq
    qb = col0 // dh
    kern = functools.partial(_fox_kernel, tq=tq, lane0=lane0)
    return pl.pallas_call(
        kern,
        grid=(batch, heads, nq),
        in_specs=[pl.BlockSpec((tq, dh), lambda b, h, i: (b * nq + i, qb + h)),
                  pl.BlockSpec((seq, dh), lambda b, h, i: (b, qb + heads + h)),
                  pl.BlockSpec((seq, dh), lambda b, h, i: (b, qb + 2 * heads + h)),
                  pl.BlockSpec((tq, c.shape[1]), lambda b, h, i: (b * nq + i, 0)),
                  pl.BlockSpec((None, None, nq, 1, tq), lambda b, h, i: (b, h, 0, 0, 0))],
        out_specs=pl.BlockSpec((tq, dh), lambda b, h, i: (b * nq + i, h)),
        out_shape=jax.ShapeDtypeStruct((t, heads * dh), BF16),
        compiler_params=_params("parallel", "parallel", "arbitrary"),
        name="fox_attention",
    )(proj, proj, proj, c, ck)


def _sb_kernel(q_ref, k_ref, v_ref, o_ref, *, tq, tk):
    qi = pl.program_id(2)
    dh = q_ref.shape[1]
    scale = dh ** -0.5
    q = q_ref[...]
    q0 = qi * tq
    nkb = (q0 + tq) // tk
    jj = lax.broadcasted_iota(jnp.int32, (tk, tk), 0)
    ss = lax.broadcasted_iota(jnp.int32, (tk, tk), 1)
    upper = (jj > ss).astype(BF16)
    tpos = q0 + lax.broadcasted_iota(jnp.int32, (tq, tk), 0)
    soff = lax.broadcasted_iota(jnp.int32, (tq, tk), 1)

    def body(n, carry):
        run, acc = carry
        kb = nkb - 1 - n
        rows = pl.ds(pl.multiple_of(kb * tk, tk), tk)
        z = _dot_nt(q, k_ref[rows, :]) * scale
        mask = (kb * tk + soff) < tpos
        sp = _softplus_neg_abs(z)
        log_beta = jnp.minimum(z, 0.0) - sp
        log_1m = jnp.where(mask, jnp.minimum(-z, 0.0) - sp, 0.0)
        hi, lo = _split_bf16(log_1m, 2)
        suffix = _dot(hi, upper) + _dot(lo, upper)
        a = jnp.where(mask, jnp.exp(log_beta + suffix + run), 0.0)
        acc = acc + _dot(a.astype(BF16), v_ref[rows, :])
        run = run + jnp.sum(log_1m, axis=1, keepdims=True)
        return run, acc

    init = (jnp.zeros((tq, 1), F32), jnp.zeros((tq, dh), F32))
    _, acc = lax.fori_loop(0, nkb, body, init)
    o_ref[...] = acc.astype(o_ref.dtype)


def _sb_attention(proj, *, batch, seq, heads, col0, tq=256, tk=128):
    t = proj.shape[0]
    dh = SB_HEAD_DIM
    nq = seq // tq
    qb = col0 // dh
    kern = functools.partial(_sb_kernel, tq=tq, tk=tk)
    return pl.pallas_call(
        kern,
        grid=(batch, heads, nq),
        in_specs=[pl.BlockSpec((tq, dh), lambda b, h, i: (b * nq + i, qb + h)),
                  pl.BlockSpec((seq, dh), lambda b, h, i: (b, qb + heads + h)),
                  pl.BlockSpec((seq, dh), lambda b, h, i: (b, qb + 2 * heads + h))],
        out_specs=pl.BlockSpec((tq, dh), lambda b, h, i: (b * nq + i, h)),
        out_shape=jax.ShapeDtypeStruct((t, heads * dh), BF16),
        compiler_params=_params("parallel", "parallel", "arbitrary"),
        name="sb_attention",
    )(proj, proj, proj)


def _gla_kernel(q_ref, k_ref, v_ref, gg_ref, gate_ref, wup_ref, bg_ref, gn_ref, o_ref,
                state_ref, osc_ref, b_sc, *, chunk, sub):
    rows_blk, dk = q_ref.shape
    dv = v_ref.shape[1]
    n_chunks = rows_blk // chunk
    n_sub = chunk // sub
    q_scale = dk ** -0.5

    @pl.when(pl.program_id(2) == 0)
    def _():
        state_ref[...] = jnp.zeros_like(state_ref)

    z = _dot_f32(gate_ref[...], wup_ref[...]) + bg_ref[...]
    a = _log_sigmoid(z) / GLA_GATE_TAU
    row = lax.broadcasted_iota(jnp.int32, (rows_blk, rows_blk), 0)
    col = lax.broadcasted_iota(jnp.int32, (rows_blk, rows_blk), 1)
    shift = int(math.log2(chunk))
    same = lax.shift_right_logical(row, shift) == lax.shift_right_logical(col, shift)
    tri = ((col <= row) & same).astype(BF16)
    b = _dot_exact_lhs(tri, a)

    key_row = lax.broadcasted_iota(jnp.int32, (chunk, 1), 0)
    for c in range(n_chunks):
        r0 = c * chunk
        bc = b[r0:r0 + chunk]
        qc = q_ref[r0:r0 + chunk, :].astype(F32) * q_scale
        kc = k_ref[r0:r0 + chunk, :].astype(F32)
        vc = v_ref[r0:r0 + chunk, :]
        state = state_ref[...]
        inter = _dot((qc * jnp.exp(bc)).astype(BF16), state.astype(BF16))
        outs = [inter[0:sub]]
        for i in range(1, n_sub):
            lo, hi = i * sub, (i + 1) * sub
            ref = bc[lo - 1:lo]
            qi = (qc[lo:hi] * jnp.exp(bc[lo:hi] - ref)).astype(BF16)
            kdec = jnp.exp(jnp.minimum(ref - bc, 0.0))
            ki = jnp.where(key_row < lo, kc * kdec, 0.0).astype(BF16)
            s = _dot_nt(qi, ki)
            outs.append(inter[lo:hi] + _dot(s.astype(BF16), vc))
        osc_ref[r0:r0 + chunk, :] = jnp.concatenate(outs, axis=0)

        bl = bc[chunk - 1:chunk]
        kd = kc * jnp.exp(bl - bc)
        stacked = jnp.concatenate([kd, jnp.broadcast_to(jnp.exp(bl), (chunk, dk))], axis=0)
        st = stacked.T
        vpad = jnp.concatenate([vc, jnp.zeros_like(vc)], axis=0)
        state_ref[...] = state * st[:, chunk:chunk + 1] + _dot(st.astype(BF16), vpad)

    b_sc[...] = b
    trow = lax.broadcasted_iota(jnp.int32, (sub, 1), 0)

    def diag(sb, carry):
        rows = pl.ds(pl.multiple_of(sb * sub, sub), sub)
        qs = q_ref[rows, :].astype(F32) * q_scale
        ks = k_ref[rows, :].astype(F32)
        vs = v_ref[rows, :].astype(F32)
        bs = b_sc[rows, :]
        acc = jnp.zeros((sub, dv), F32)
        for s_ in range(sub):
            d = jnp.where(trow >= s_, bs - bs[s_:s_ + 1], NEG_BIG)
            p = qs * ks[s_:s_ + 1] * jnp.exp(d)
            acc = acc + jnp.sum(p, axis=1, keepdims=True) * vs[s_:s_ + 1]
        osc_ref[rows, :] = osc_ref[rows, :] + acc
        return carry

    lax.fori_loop(0, rows_blk // sub, diag, 0)

    o = osc_ref[...]
    ms = jnp.mean(o * o, axis=-1, keepdims=True)
    y = o * lax.rsqrt(ms + EPS) * gn_ref[...]
    g = gg_ref[...].astype(F32)
    o_ref[...] = (y * (g * _sigmoid(g))).astype(o_ref.dtype)


def _gla(proj, gates, wup_pad, b_gate, g_norm, *, batch, seq, heads, dk, dv, rows_blk=256):
    t = proj.shape[0]
    nr = seq // rows_blk
    kb0 = heads
    vb0 = 2 * heads * dk // dv
    gb0 = vb0 + heads
    kern = functools.partial(_gla_kernel, chunk=GLA_CHUNK, sub=GLA_SUB)
    return pl.pallas_call(
        kern,
        grid=(batch, heads, nr),
        in_specs=[pl.BlockSpec((rows_blk, dk), lambda b, h, r: (b * nr + r, h)),
                  pl.BlockSpec((rows_blk, dk), lambda b, h, r: (b * nr + r, kb0 + h)),
                  pl.BlockSpec((rows_blk, dv), lambda b, h, r: (b * nr + r, vb0 + h)),
                  pl.BlockSpec((rows_blk, dv), lambda b, h, r: (b * nr + r, gb0 + h)),
                  pl.BlockSpec((rows_blk, gates.shape[1]), lambda b, h, r: (b * nr + r, 0)),
                  pl.BlockSpec((wup_pad.shape[0], dk), lambda b, h, r: (0, h)),
                  pl.BlockSpec((1, dk), lambda b, h, r: (0, h)),
                  pl.BlockSpec((1, dv), lambda b, h, r: (0, 0))],
        out_specs=pl.BlockSpec((rows_blk, dv), lambda b, h, r: (b * nr + r, h)),
        out_shape=jax.ShapeDtypeStruct((t, heads * dv), BF16),
        scratch_shapes=[pltpu.VMEM((dk, dv), F32),
                        pltpu.VMEM((rows_blk, dv), F32),
                        pltpu.VMEM((rows_blk, dk), F32)],
        compiler_params=_params("parallel", "parallel", "arbitrary"),
        name="gla",
    )(proj, proj, proj, proj, gates, wup_pad, b_gate, g_norm)


def _rglru_kernel(rx_ref, ry_ref, cw_ref, cb_ref, wx_ref, bx_ref, wa_ref, ba_ref, lam_ref,
                  o_ref, *, rows_t):
    seq, w = rx_ref.shape
    halo = 8
    wxb = wx_ref[...].astype(BF16)
    wab = wa_ref[...].astype(BF16)
    cw = cw_ref[...]
    neg_lam = -lam_ref[...]
    softplus = jnp.maximum(neg_lam, 0.0) + _softplus_neg_abs(neg_lam)
    coef = -RG_C * softplus
    row = lax.broadcasted_iota(jnp.int32, (rows_t, 1), 0)

    def body(t, carry):
        h_prev, tail = carry
        rows = pl.ds(pl.multiple_of(t * rows_t, rows_t), rows_t)
        x = rx_ref[rows, :].astype(F32)
        xe = jnp.concatenate([tail, x], axis=0)
        xc = cb_ref[...] + cw[RG_CONV - 1:RG_CONV] * x
        for s_ in range(1, RG_CONV):
            xc = xc + cw[RG_CONV - 1 - s_:RG_CONV - s_] * pltpu.roll(xe, s_, axis=0)[halo:]
        xcb = xc.astype(BF16)
        gate_x = _sigmoid(_dot(xcb, wxb) + bx_ref[...])
        gate_a = _sigmoid(_dot(xcb, wab) + ba_ref[...])
        log_a = coef * gate_a
        a_dec = jnp.exp(log_a)
        one_m_a2 = -jnp.tanh(log_a) * (a_dec * a_dec + 1.0)
        u = jnp.sqrt(one_m_a2) * (gate_x * xc)
        d = 1
        while d < rows_t:
            if d < halo:
                a_sh = jnp.where(row >= d, pltpu.roll(a_dec, d, axis=0), 1.0)
                u_sh = jnp.where(row >= d, pltpu.roll(u, d, axis=0), 0.0)
            else:
                a_sh = jnp.concatenate([jnp.ones((d, w), F32), a_dec[:rows_t - d]], axis=0)
                u_sh = jnp.concatenate([jnp.zeros((d, w), F32), u[:rows_t - d]], axis=0)
            u = a_dec * u_sh + u
            a_dec = a_dec * a_sh
            d *= 2
        h = u + a_dec * h_prev
        y = ry_ref[rows, :].astype(F32)
        o_ref[rows, :] = (h * _gelu_tanh(y)).astype(o_ref.dtype)
        return h[rows_t - 1:rows_t], x[rows_t - halo:]

    init = (jnp.zeros((1, w), F32), jnp.zeros((halo, w), F32))
    lax.fori_loop(0, seq // rows_t, body, init)


def _rglru(proj, conv_w, conv_b, w_x, b_x, w_a, b_a, lam, *, batch, seq, width, rows_t=256):
    t = proj.shape[0]
    wb = RG_BLOCK
    nb = width // wb
    kern = functools.partial(_rglru_kernel, rows_t=rows_t)
    vec = lambda v: v.reshape(1, width)
    vspec = pl.BlockSpec((1, wb), lambda b, g: (0, g))
    mspec = pl.BlockSpec((None, wb, wb), lambda b, g: (g, 0, 0))
    return pl.pallas_call(
        kern,
        grid=(batch, nb),
        in_specs=[pl.BlockSpec((seq, wb), lambda b, g: (b, nb + g)),
                  pl.BlockSpec((seq, wb), lambda b, g: (b, g)),
                  pl.BlockSpec((RG_CONV, wb), lambda b, g: (0, g)),
                  vspec, mspec, vspec, mspec, vspec, vspec],
        out_specs=pl.BlockSpec((seq, wb), lambda b, g: (b, g)),
        out_shape=jax.ShapeDtypeStruct((t, width), BF16),
        compiler_params=_params("parallel", "parallel"),
        name="rglru",
    )(proj, proj, conv_w, vec(conv_b), w_x, vec(b_x), w_a, vec(b_a), vec(lam))


_TM = 1024
_TN = 512


def _ffn(h, g, w_gate, w_up, w_down):
    u = _rmsnorm(h, g, BF16)
    hid = _matmul_swiglu(u, w_gate, w_up, tm=_TM, tn=256)
    return _matmul_res(hid, w_down.astype(BF16), h, tm=512, tn=_TN)


def _even_mixer(h, g, w_in, w_gate_up, b_gate, g_norm, b_f, w_out, *, batch, seq):
    d = h.shape[1]
    dk, dv = d // 16, d // 8
    qk, vw, fw = GLA_HEADS * dk, GLA_HEADS * dv, d // 2
    fox_heads = fw // FOX_HEAD_DIM
    rank = GLA_GATE_RANK
    main0 = 2 * qk + 2 * vw
    main1 = 3 * fw
    w_small = jnp.concatenate([w_in[:, main0:main0 + rank],
                               w_in[:, main0 + rank + main1:]], axis=1)
    w_small = jnp.pad(w_small, ((0, 0), (0, LANES - w_small.shape[1])))
    gates = _gate_proj(h, g, w_small)
    u = _rmsnorm(h, g, BF16)
    proj = _in_proj_shift(u, w_in, tm=_TM, tn=_TN, n_plain=main0 // _TN,
                          n_total=(main0 + main1) // _TN, shift=rank)
    bias_row = jnp.zeros((1, LANES), F32).at[0, rank:rank + fox_heads].set(b_f)
    c, ct = _fox_cumsum(gates, bias_row, batch=batch, seq=seq, lane0=rank, heads=fox_heads)
    tq = 256
    ck = ct.reshape(batch, fox_heads, seq // tq, 1, tq)
    o_fox = _fox_attention(proj, c, ck, batch=batch, seq=seq, heads=fox_heads, col0=main0,
                           lane0=rank, tq=tq)
    wup_pad = jnp.pad(w_gate_up, ((0, LANES - rank), (0, 0)))
    o_gla = _gla(proj, gates, wup_pad, b_gate.reshape(1, qk), g_norm.reshape(1, dv),
                 batch=batch, seq=seq, heads=GLA_HEADS, dk=dk, dv=dv)
    o = jnp.concatenate([o_gla, o_fox], axis=1)
    return _matmul_res(o, w_out, h, tm=_TM, tn=_TN)


def _odd_mixer(h, g, w_in, conv_w, conv_b, w_x, b_x, w_a, b_a, lam, w_out, *, batch, seq):
    d = h.shape[1]
    width = d // 2
    sb_heads = width // SB_HEAD_DIM
    u = _rmsnorm(h, g, BF16)
    proj = _matmul(u, w_in, tm=_TM, tn=_TN, out_dtype=BF16)
    o_rg = _rglru(proj, conv_w, conv_b, w_x, b_x, w_a, b_a, lam, batch=batch, seq=seq,
                  width=width)
    o_sb = _sb_attention(proj, batch=batch, seq=seq, heads=sb_heads, col0=2 * width)
    o = jnp.concatenate([o_rg, o_sb], axis=1)
    return _matmul_res(o, w_out, h, tm=_TM, tn=_TN)


def kernel(x, norm_mix, norm_ffn, ffn_w_gate, ffn_w_up, ffn_w_down, ab_w_in, gla_w_gate_up,
           gla_b_gate, gla_norm, fox_b_f, ab_w_out, cd_w_in, rg_conv_w, rg_conv_b, rg_w_x,
           rg_b_x, rg_w_a, rg_b_a, rg_lambda, cd_w_out, final_norm):
    batch, seq, d = x.shape
    h = x.reshape(batch * seq, d)
    for layer in range(norm_mix.shape[0]):
        i = layer // 2
        if layer % 2 == 0:
            h = _even_mixer(h, norm_mix[layer], ab_w_in[i], gla_w_gate_up[i], gla_b_gate[i],
                            gla_norm[i], fox_b_f[i], ab_w_out[i], batch=batch, seq=seq)
        else:
            h = _odd_mixer(h, norm_mix[layer], cd_w_in[i], rg_conv_w[i], rg_conv_b[i],
                           rg_w_x[i], rg_b_x[i], rg_w_a[i], rg_b_a[i], rg_lambda[i],
                           cd_w_out[i], batch=batch, seq=seq)
        h = _ffn(h, norm_ffn[layer], ffn_w_gate[layer], ffn_w_up[layer], ffn_w_down[layer])
    return _rmsnorm(h, final_norm, F32).reshape(batch, seq, d)
```

```python
import functools
import math

import jax
import jax.numpy as jnp
from jax import lax
from jax.experimental import pallas as pl
from jax.experimental.pallas import tpu as pltpu

F32 = jnp.float32
BF16 = jnp.bfloat16

EPS = 1e-6
LANES = 128
VMEM_LIMIT = 56 * 1024 * 1024

GLA_HEADS = 4
GLA_GATE_RANK = 16
GLA_GATE_TAU = 16.0
GLA_CHUNK = 64
GLA_SUB = 16
FOX_HEAD_DIM = 128
SB_HEAD_DIM = 128
RG_BLOCK = 256
RG_CONV = 4
RG_C = 8.0

NEG_BIG = -1e30
_NT = (((1,), (1,)), ((), ()))


def _params(*sem):
    return pltpu.CompilerParams(dimension_semantics=sem, vmem_limit_bytes=VMEM_LIMIT)


def _dot(a, b):
    return jnp.dot(a, b, preferred_element_type=F32)


def _dot_nt(a, b):
    return lax.dot_general(a, b, _NT, preferred_element_type=F32)


def _split_bf16(x, n):
    parts = []
    r = x
    for _ in range(n):
        p = r.astype(BF16)
        parts.append(p)
        r = r - p.astype(F32)
    return parts


def _dot_f32(a, b):
    ah, al = _split_bf16(a, 2)
    bh, bl = _split_bf16(b, 2)
    return _dot(ah, bh) + _dot(ah, bl) + _dot(al, bh)


def _dot_exact_lhs(m01, x, n=3):
    acc = None
    for p in _split_bf16(x, n):
        t = _dot(m01, p)
        acc = t if acc is None else acc + t
    return acc


def _softplus_neg_abs(z):
    return jnp.log1p(jnp.exp(-jnp.abs(z)))


def _log_sigmoid(z):
    return jnp.minimum(z, 0.0) - _softplus_neg_abs(z)


def _sigmoid(z):
    return 1.0 / (1.0 + jnp.exp(-z))


def _gelu_tanh(y):
    c = math.sqrt(2.0 / math.pi)
    return y * (0.5 * (1.0 + jnp.tanh(c * (y + 0.044715 * (y * y * y)))))


def _rmsnorm_kernel(x_ref, g_ref, o_ref):
    x = x_ref[...]
    ms = jnp.mean(x * x, axis=-1, keepdims=True)
    o_ref[...] = (x * lax.rsqrt(ms + EPS) * g_ref[...]).astype(o_ref.dtype)


def _rmsnorm(x, g, out_dtype, tm=512):
    t, d = x.shape
    return pl.pallas_call(
        _rmsnorm_kernel,
        grid=(t // tm,),
        in_specs=[pl.BlockSpec((tm, d), lambda i: (i, 0)),
                  pl.BlockSpec((1, d), lambda i: (0, 0))],
        out_specs=pl.BlockSpec((tm, d), lambda i: (i, 0)),
        out_shape=jax.ShapeDtypeStruct((t, d), out_dtype),
        compiler_params=_params("parallel"),
        name="rmsnorm",
    )(x, g.reshape(1, d))


def _gate_proj_kernel(x_ref, g_ref, wa_ref, wb_ref, u_ref, o_ref, *, n_a, n_b):
    x = x_ref[...]
    ms = jnp.mean(x * x, axis=-1, keepdims=True)
    u = x * lax.rsqrt(ms + EPS) * g_ref[...]
    u_ref[...] = u.astype(u_ref.dtype)
    lane = lax.broadcasted_iota(jnp.int32, o_ref.shape, 1)
    oa = _dot_f32(u, wa_ref[...])
    ob = _dot_f32(u, wb_ref[...])
    o_ref[...] = jnp.where(lane < n_a, oa, jnp.where(lane < n_a + n_b, ob, 0.0))


def _gate_proj(x, g, w, layer, *, col_a, n_a, col_b, n_b, tm=256):
    t, d = x.shape
    assert col_a % LANES == 0 and col_b % LANES == n_a
    kern = functools.partial(_gate_proj_kernel, n_a=n_a, n_b=n_b)
    return pl.pallas_call(
        kern,
        grid=(t // tm,),
        in_specs=[pl.BlockSpec((tm, d), lambda i: (i, 0)),
                  pl.BlockSpec((1, d), lambda i: (0, 0)),
                  pl.BlockSpec((None, d, LANES), lambda i: (layer, 0, col_a // LANES)),
                  pl.BlockSpec((None, d, LANES), lambda i: (layer, 0, col_b // LANES))],
        out_specs=[pl.BlockSpec((tm, d), lambda i: (i, 0)),
                   pl.BlockSpec((tm, LANES), lambda i: (i, 0))],
        out_shape=[jax.ShapeDtypeStruct((t, d), BF16),
                   jax.ShapeDtypeStruct((t, LANES), F32)],
        compiler_params=_params("parallel"),
        name="gate_proj",
    )(x, g.reshape(1, d), w, w)


_CAST_ROWS = 512


def _cast_rows(w_ref, wb_ref):
    k = w_ref.shape[0]

    def body(r, c):
        rows = pl.ds(pl.multiple_of(r * _CAST_ROWS, _CAST_ROWS), _CAST_ROWS)
        wb_ref[rows, :] = w_ref[rows, :].astype(BF16)
        return c

    lax.fori_loop(0, k // _CAST_ROWS, body, 0)


def _mm_kernel(a_ref, w_ref, o_ref, wb_ref):
    @pl.when(pl.program_id(1) == 0)
    def _():
        _cast_rows(w_ref, wb_ref)

    o_ref[...] = _dot(a_ref[...], wb_ref[...]).astype(o_ref.dtype)


def _mm_res2_kernel(a1_ref, a2_ref, w_ref, r_ref, o_ref, wb_ref):
    @pl.when(pl.program_id(1) == 0)
    def _():
        _cast_rows(w_ref, wb_ref)

    k1 = a1_ref.shape[1]
    o_ref[...] = (r_ref[...] + _dot(a1_ref[...], wb_ref[:k1, :])
                  + _dot(a2_ref[...], wb_ref[k1:, :]))


def _mm_res_bf16w_kernel(a_ref, w_ref, r_ref, o_ref):
    o_ref[...] = r_ref[...] + _dot(a_ref[...], w_ref[...])


def _mm_swiglu_kernel(a_ref, wg_ref, wu_ref, o_ref, wgb_ref, wub_ref):
    @pl.when(pl.program_id(1) == 0)
    def _():
        _cast_rows(wg_ref, wgb_ref)
        _cast_rows(wu_ref, wub_ref)

    a = a_ref[...]
    g = _dot(a, wgb_ref[...])
    u = _dot(a, wub_ref[...])
    o_ref[...] = (g * _sigmoid(g) * u).astype(o_ref.dtype)


def _in_proj_shift_kernel(a_ref, w_ref, wx_ref, o_ref, wb_ref, *, n_plain, shift):
    j = pl.program_id(0)
    first = pl.program_id(1) == 0
    k, tn = w_ref.shape
    width = tn + wx_ref.shape[1]

    @pl.when(first & (j < n_plain))
    def _():
        _cast_rows(w_ref, wb_ref)

    @pl.when(first & (j >= n_plain))
    def _():
        def body(r, c):
            rows = pl.ds(pl.multiple_of(r * _CAST_ROWS, _CAST_ROWS), _CAST_ROWS)
            cat = jnp.concatenate([w_ref[rows, :], wx_ref[rows, :]], axis=1)
            wb_ref[rows, :] = pltpu.roll(cat, width - shift, axis=1)[:, :tn].astype(BF16)
            return c

        lax.fori_loop(0, k // _CAST_ROWS, body, 0)

    o_ref[...] = _dot(a_ref[...], wb_ref[...]).astype(o_ref.dtype)


def _w_spec(layer, k, tn):
    return pl.BlockSpec((None, k, tn), lambda j, i: (layer, 0, j))


def _matmul(a, w, layer, *, tm, tn, out_dtype):
    m, k = a.shape
    n = w.shape[2]
    return pl.pallas_call(
        _mm_kernel,
        grid=(n // tn, m // tm),
        in_specs=[pl.BlockSpec((tm, k), lambda j, i: (i, 0)), _w_spec(layer, k, tn)],
        out_specs=pl.BlockSpec((tm, tn), lambda j, i: (i, j)),
        out_shape=jax.ShapeDtypeStruct((m, n), out_dtype),
        scratch_shapes=[pltpu.VMEM((k, tn), BF16)],
        compiler_params=_params("arbitrary", "arbitrary"),
        name="matmul",
    )(a, w)


def _matmul_res2(a1, a2, w, layer, res, *, tm, tn):
    m, k1 = a1.shape
    k2 = a2.shape[1]
    k, n = w.shape[1:]
    assert k == k1 + k2
    return pl.pallas_call(
        _mm_res2_kernel,
        grid=(n // tn, m // tm),
        in_specs=[pl.BlockSpec((tm, k1), lambda j, i: (i, 0)),
                  pl.BlockSpec((tm, k2), lambda j, i: (i, 0)),
                  _w_spec(layer, k, tn),
                  pl.BlockSpec((tm, tn), lambda j, i: (i, j))],
        out_specs=pl.BlockSpec((tm, tn), lambda j, i: (i, j)),
        out_shape=jax.ShapeDtypeStruct((m, n), F32),
        scratch_shapes=[pltpu.VMEM((k, tn), BF16)],
        compiler_params=_params("arbitrary", "arbitrary"),
        name="matmul_res2",
    )(a1, a2, w, res)


def _matmul_res_bf16w(a, w, layer, res, *, tm, tn):
    m, k = a.shape
    n = w.shape[2]
    return pl.pallas_call(
        _mm_res_bf16w_kernel,
        grid=(n // tn, m // tm),
        in_specs=[pl.BlockSpec((tm, k), lambda j, i: (i, 0)),
                  _w_spec(layer, k, tn),
                  pl.BlockSpec((tm, tn), lambda j, i: (i, j))],
        out_specs=pl.BlockSpec((tm, tn), lambda j, i: (i, j)),
        out_shape=jax.ShapeDtypeStruct((m, n), F32),
        compiler_params=_params("arbitrary", "arbitrary"),
        name="matmul_res_bf16w",
    )(a, w, res)


def _matmul_swiglu(a, wg, wu, layer, *, tm, tn):
    m, k = a.shape
    n = wg.shape[2]
    return pl.pallas_call(
        _mm_swiglu_kernel,
        grid=(n // tn, m // tm),
        in_specs=[pl.BlockSpec((tm, k), lambda j, i: (i, 0)),
                  _w_spec(layer, k, tn), _w_spec(layer, k, tn)],
        out_specs=pl.BlockSpec((tm, tn), lambda j, i: (i, j)),
        out_shape=jax.ShapeDtypeStruct((m, n), BF16),
        scratch_shapes=[pltpu.VMEM((k, tn), BF16), pltpu.VMEM((k, tn), BF16)],
        compiler_params=_params("arbitrary", "arbitrary"),
        name="matmul_swiglu",
    )(a, wg, wu)


def _in_proj_shift(a, w, layer, *, tm, tn, n_plain, n_total, shift):
    m, k = a.shape
    nx = tn // LANES
    kern = functools.partial(_in_proj_shift_kernel, n_plain=n_plain, shift=shift)
    return pl.pallas_call(
        kern,
        grid=(n_total, m // tm),
        in_specs=[pl.BlockSpec((tm, k), lambda j, i: (i, 0)),
                  _w_spec(layer, k, tn),
                  pl.BlockSpec((None, k, LANES), lambda j, i: (layer, 0, (j + 1) * nx))],
        out_specs=pl.BlockSpec((tm, tn), lambda j, i: (i, j)),
        out_shape=jax.ShapeDtypeStruct((m, n_total * tn), BF16),
        scratch_shapes=[pltpu.VMEM((k, tn), BF16)],
        compiler_params=_params("arbitrary", "arbitrary"),
        name="in_proj_shift",
    )(a, w, w)


def _fox_cumsum_kernel(g_ref, b_ref, c_ref, ct_ref, c_sc, *, blk, lane0, heads):
    s = g_ref.shape[0]
    row = lax.broadcasted_iota(jnp.int32, (blk, blk), 0)
    col = lax.broadcasted_iota(jnp.int32, (blk, blk), 1)
    tri = (col <= row).astype(BF16)
    carry = jnp.zeros((1, g_ref.shape[1]), F32)
    for n in range(s // blk):
        rows = slice(n * blk, (n + 1) * blk)
        lf = _log_sigmoid(g_ref[rows, :] + b_ref[...])
        c = _dot_exact_lhs(tri, lf) + carry
        c_sc[rows, :] = c
        carry = c[blk - 1:blk, :]
    c_all = c_sc[...]
    c_ref[...] = c_all
    ct_ref[...] = c_all.T[lane0:lane0 + heads, :]


def _fox_cumsum(gates, bias_row, *, batch, seq, lane0, heads, blk=256):
    t, n = gates.shape
    kern = functools.partial(_fox_cumsum_kernel, blk=blk, lane0=lane0, heads=heads)
    return pl.pallas_call(
        kern,
        grid=(batch,),
        in_specs=[pl.BlockSpec((seq, n), lambda b: (b, 0)),
                  pl.BlockSpec((1, n), lambda b: (0, 0))],
        out_specs=[pl.BlockSpec((seq, n), lambda b: (b, 0)),
                   pl.BlockSpec((None, heads, seq), lambda b: (b, 0, 0))],
        out_shape=[jax.ShapeDtypeStruct((t, n), F32),
                   jax.ShapeDtypeStruct((batch, heads, seq), F32)],
        scratch_shapes=[pltpu.VMEM((seq, n), F32)],
        compiler_params=_params("parallel"),
        name="fox_cumsum",
    )(gates, bias_row)


def _fox_kernel(q_ref, k_ref, v_ref, c_ref, ck_ref, o_ref, *, tq, lane0):
    h = pl.program_id(1)
    qi = pl.program_id(2)
    dh = q_ref.shape[1]
    scale = dh ** -0.5
    q = q_ref[...]
    lane = lax.broadcasted_iota(jnp.int32, c_ref.shape, 1)
    cq = jnp.sum(jnp.where(lane == lane0 + h, c_ref[...], 0.0), axis=1, keepdims=True)

    def logits(kb):
        rows = pl.ds(pl.multiple_of(kb * tq, tq), tq)
        s = _dot_nt(q, k_ref[rows, :]) * scale
        return s + cq - ck_ref[kb], v_ref[rows, :]

    s, v = logits(qi)
    tpos = lax.broadcasted_iota(jnp.int32, (tq, tq), 0)
    spos = lax.broadcasted_iota(jnp.int32, (tq, tq), 1)
    s = jnp.where(spos <= tpos, s, NEG_BIG)
    m0 = jnp.max(s, axis=1, keepdims=True)
    p = jnp.exp(s - m0)
    l0 = jnp.sum(p, axis=1, keepdims=True)
    acc0 = _dot(p.astype(BF16), v)

    def body(kb, carry):
        m, l, acc = carry
        s, v = logits(kb)
        m_new = jnp.maximum(m, jnp.max(s, axis=1, keepdims=True))
        alpha = jnp.exp(m - m_new)
        p = jnp.exp(s - m_new)
        l = alpha * l + jnp.sum(p, axis=1, keepdims=True)
        acc = alpha * acc + _dot(p.astype(BF16), v)
        return m_new, l, acc

    _, l, acc = lax.fori_loop(0, qi, body, (m0, l0, acc0))
    o_ref[...] = (acc / l).astype(o_ref.dtype)


def _fox_attention(proj, c, ck, *, batch, seq, heads, col0, lane0, tq=256):
    t = proj.shape[0]
    dh = FOX_HEAD_DIM
    nq = seq // tq
    qb = col0 // dh
    kern = functools.partial(_fox_kernel, tq=tq, lane0=lane0)
    return pl.pallas_call(
        kern,
        grid=(batch, heads, nq),
        in_specs=[pl.BlockSpec((tq, dh), lambda b, h, i: (b * nq + i, qb + h)),
                  pl.BlockSpec((seq, dh), lambda b, h, i: (b, qb + heads + h)),
                  pl.BlockSpec((seq, dh), lambda b, h, i: (b, qb + 2 * heads + h)),
                  pl.BlockSpec((tq, c.shape[1]), lambda b, h, i: (b * nq + i, 0)),
                  pl.BlockSpec((None, None, nq, 1, tq), lambda b, h, i: (b, h, 0, 0, 0))],
        out_specs=pl.BlockSpec((tq, dh), lambda b, h, i: (b * nq + i, h)),
        out_shape=jax.ShapeDtypeStruct((t, heads * dh), BF16),
        compiler_params=_params("parallel", "parallel", "arbitrary"),
        name="fox_attention",
    )(proj, proj, proj, c, ck)


def _sb_kernel(q_ref, k_ref, v_ref, o_ref, *, tq, tk):
    qi = pl.program_id(2)
    dh = q_ref.shape[1]
    scale = dh ** -0.5
    q = q_ref[...]
    nd = tq // tk
    jj = lax.broadcasted_iota(jnp.int32, (tk, tk), 0)
    ss = lax.broadcasted_iota(jnp.int32, (tk, tk), 1)
    upper = (jj > ss).astype(BF16)
    row = lax.broadcasted_iota(jnp.int32, (tq, tk), 0)
    col = lax.broadcasted_iota(jnp.int32, (tq, tk), 1)

    def block(kb, run, mask):
        rows = pl.ds(pl.multiple_of(kb * tk, tk), tk)
        z = _dot_nt(q, k_ref[rows, :]) * scale
        sp = jnp.log(1.0 + jnp.exp(-jnp.abs(z)))
        log_beta = jnp.minimum(z, 0.0) - sp
        log_1m = jnp.minimum(-z, 0.0) - sp
        if mask is not None:
            log_1m = jnp.where(mask, log_1m, 0.0)
        hi, lo = _split_bf16(log_1m, 2)
        suffix = _dot(hi, upper) + _dot(lo, upper)
        a = jnp.exp(log_beta + suffix + run)
        if mask is not None:
            a = jnp.where(mask, a, 0.0)
        pv = _dot(a.astype(BF16), v_ref[rows, :])
        return pv, jnp.sum(log_1m, axis=1, keepdims=True)

    run = jnp.zeros((tq, 1), F32)
    acc = jnp.zeros((tq, dh), F32)
    for d in range(nd):
        off = (nd - 1 - d) * tk
        pv, rs = block((qi + 1) * nd - 1 - d, run, (off + col) < row)
        acc = acc + pv
        run = run + rs

    def body(p, carry):
        run, acc = carry
        for d in range(nd):
            pv, rs = block((qi - p) * nd - 1 - d, run, None)
            acc = acc + pv
            run = run + rs
        return run, acc

    _, acc = lax.fori_loop(0, qi, body, (run, acc))
    o_ref[...] = acc.astype(o_ref.dtype)


def _sb_attention(proj, *, batch, seq, heads, col0, tq=512, tk=256):
    t = proj.shape[0]
    dh = SB_HEAD_DIM
    nq = seq // tq
    qb = col0 // dh
    kern = functools.partial(_sb_kernel, tq=tq, tk=tk)
    return pl.pallas_call(
        kern,
        grid=(batch, heads, nq),
        in_specs=[pl.BlockSpec((tq, dh), lambda b, h, i: (b * nq + i, qb + h)),
                  pl.BlockSpec((seq, dh), lambda b, h, i: (b, qb + heads + h)),
                  pl.BlockSpec((seq, dh), lambda b, h, i: (b, qb + 2 * heads + h))],
        out_specs=pl.BlockSpec((tq, dh), lambda b, h, i: (b * nq + i, h)),
        out_shape=jax.ShapeDtypeStruct((t, heads * dh), BF16),
        compiler_params=_params("parallel", "parallel", "arbitrary"),
        name="sb_attention",
    )(proj, proj, proj)


def _gla_kernel(q_ref, k_ref, v_ref, gg_ref, gate_ref, wup_ref, bg_ref, gn_ref, o_ref,
                state_ref, osc_ref, b_sc, *, chunk, sub):
    rows_blk, dk = q_ref.shape
    dv = v_ref.shape[1]
    n_chunks = rows_blk // chunk
    n_sub = chunk // sub
    q_scale = dk ** -0.5

    @pl.when(pl.program_id(2) == 0)
    def _():
        state_ref[...] = jnp.zeros_like(state_ref)

    z = _dot_f32(gate_ref[...], wup_ref[...]) + bg_ref[...]
    a = _log_sigmoid(z) / GLA_GATE_TAU
    row = lax.broadcasted_iota(jnp.int32, (rows_blk, rows_blk), 0)
    col = lax.broadcasted_iota(jnp.int32, (rows_blk, rows_blk), 1)
    shift = int(math.log2(chunk))
    same = lax.shift_right_logical(row, shift) == lax.shift_right_logical(col, shift)
    tri = ((col <= row) & same).astype(BF16)
    b = _dot_exact_lhs(tri, a)

    key_row = lax.broadcasted_iota(jnp.int32, (chunk, 1), 0)
    for c in range(n_chunks):
        r0 = c * chunk
        bc = b[r0:r0 + chunk]
        qc = q_ref[r0:r0 + chunk, :].astype(F32) * q_scale
        kc = k_ref[r0:r0 + chunk, :].astype(F32)
        vc = v_ref[r0:r0 + chunk, :]
        state = state_ref[...]
        inter = _dot((qc * jnp.exp(bc)).astype(BF16), state.astype(BF16))
        outs = [inter[0:sub]]
        for i in range(1, n_sub):
            lo, hi = i * sub, (i + 1) * sub
            ref = bc[lo - 1:lo]
            qi = (qc[lo:hi] * jnp.exp(bc[lo:hi] - ref)).astype(BF16)
            kdec = jnp.exp(jnp.minimum(ref - bc, 0.0))
            ki = jnp.where(key_row < lo, kc * kdec, 0.0).astype(BF16)
            s = _dot_nt(qi, ki)
            outs.append(inter[lo:hi] + _dot(s.astype(BF16), vc))
        osc_ref[r0:r0 + chunk, :] = jnp.concatenate(outs, axis=0)

        bl = bc[chunk - 1:chunk]
        kd = kc * jnp.exp(bl - bc)
        stacked = jnp.concatenate([kd, jnp.broadcast_to(jnp.exp(bl), (chunk, dk))], axis=0)
        st = stacked.T
        vpad = jnp.concatenate([vc, jnp.zeros_like(vc)], axis=0)
        state_ref[...] = state * st[:, chunk:chunk + 1] + _dot(st.astype(BF16), vpad)

    b_sc[...] = b
    trow = lax.broadcasted_iota(jnp.int32, (sub, 1), 0)

    def diag(sb, carry):
        rows = pl.ds(pl.multiple_of(sb * sub, sub), sub)
        qs = q_ref[rows, :].astype(F32) * q_scale
        ks = k_ref[rows, :].astype(F32)
        vs = v_ref[rows, :].astype(F32)
        bs = b_sc[rows, :]
        acc = jnp.zeros((sub, dv), F32)
        for s_ in range(sub):
            d = jnp.where(trow >= s_, bs - bs[s_:s_ + 1], NEG_BIG)
            p = qs * ks[s_:s_ + 1] * jnp.exp(d)
            acc = acc + jnp.sum(p, axis=1, keepdims=True) * vs[s_:s_ + 1]
        osc_ref[rows, :] = osc_ref[rows, :] + acc
        return carry

    lax.fori_loop(0, rows_blk // sub, diag, 0)

    o = osc_ref[...]
    ms = jnp.mean(o * o, axis=-1, keepdims=True)
    y = o * lax.rsqrt(ms + EPS) * gn_ref[...]
    g = gg_ref[...].astype(F32)
    o_ref[...] = (y * (g * _sigmoid(g))).astype(o_ref.dtype)


def _gla(proj, gates, wup_pad, b_gate, g_norm, *, batch, seq, heads, dk, dv, rows_blk=256):
    t = proj.shape[0]
    nr = seq // rows_blk
    kb0 = heads
    vb0 = 2 * heads * dk // dv
    gb0 = vb0 + heads
    kern = functools.partial(_gla_kernel, chunk=GLA_CHUNK, sub=GLA_SUB)
    return pl.pallas_call(
        kern,
        grid=(batch, heads, nr),
        in_specs=[pl.BlockSpec((rows_blk, dk), lambda b, h, r: (b * nr + r, h)),
                  pl.BlockSpec((rows_blk, dk), lambda b, h, r: (b * nr + r, kb0 + h)),
                  pl.BlockSpec((rows_blk, dv), lambda b, h, r: (b * nr + r, vb0 + h)),
                  pl.BlockSpec((rows_blk, dv), lambda b, h, r: (b * nr + r, gb0 + h)),
                  pl.BlockSpec((rows_blk, gates.shape[1]), lambda b, h, r: (b * nr + r, 0)),
                  pl.BlockSpec((wup_pad.shape[0], dk), lambda b, h, r: (0, h)),
                  pl.BlockSpec((1, dk), lambda b, h, r: (0, h)),
                  pl.BlockSpec((1, dv), lambda b, h, r: (0, 0))],
        out_specs=pl.BlockSpec((rows_blk, dv), lambda b, h, r: (b * nr + r, h)),
        out_shape=jax.ShapeDtypeStruct((t, heads * dv), BF16),
        scratch_shapes=[pltpu.VMEM((dk, dv), F32),
                        pltpu.VMEM((rows_blk, dv), F32),
                        pltpu.VMEM((rows_blk, dk), F32)],
        compiler_params=_params("parallel", "parallel", "arbitrary"),
        name="gla",
    )(proj, proj, proj, proj, gates, wup_pad, b_gate, g_norm)


def _rglru_kernel(rx_ref, ry_ref, cw_ref, cb_ref, wx_ref, bx_ref, wa_ref, ba_ref, lam_ref,
                  o_ref, *, rows_t):
    seq, w = rx_ref.shape
    halo = 8
    wxb = wx_ref[...].astype(BF16)
    wab = wa_ref[...].astype(BF16)
    cw = cw_ref[...]
    neg_lam = -lam_ref[...]
    softplus = jnp.maximum(neg_lam, 0.0) + _softplus_neg_abs(neg_lam)
    coef = -RG_C * softplus
    row = lax.broadcasted_iota(jnp.int32, (rows_t, 1), 0)

    def body(t, carry):
        h_prev, tail = carry
        rows = pl.ds(pl.multiple_of(t * rows_t, rows_t), rows_t)
        x = rx_ref[rows, :].astype(F32)
        xe = jnp.concatenate([tail, x], axis=0)
        xc = cb_ref[...] + cw[RG_CONV - 1:RG_CONV] * x
        for s_ in range(1, RG_CONV):
            xc = xc + cw[RG_CONV - 1 - s_:RG_CONV - s_] * pltpu.roll(xe, s_, axis=0)[halo:]
        xcb = xc.astype(BF16)
        gate_x = _sigmoid(_dot(xcb, wxb) + bx_ref[...])
        gate_a = _sigmoid(_dot(xcb, wab) + ba_ref[...])
        log_a = coef * gate_a
        a_dec = jnp.exp(log_a)
        one_m_a2 = -jnp.tanh(log_a) * (a_dec * a_dec + 1.0)
        u = jnp.sqrt(one_m_a2) * (gate_x * xc)
        d = 1
        while d < rows_t:
            if d < halo:
                a_sh = jnp.where(row >= d, pltpu.roll(a_dec, d, axis=0), 1.0)
                u_sh = jnp.where(row >= d, pltpu.roll(u, d, axis=0), 0.0)
            else:
                a_sh = jnp.concatenate([jnp.ones((d, w), F32), a_dec[:rows_t - d]], axis=0)
                u_sh = jnp.concatenate([jnp.zeros((d, w), F32), u[:rows_t - d]], axis=0)
            u = a_dec * u_sh + u
            a_dec = a_dec * a_sh
            d *= 2
        h = u + a_dec * h_prev
        y = ry_ref[rows, :].astype(F32)
        o_ref[rows, :] = (h * _gelu_tanh(y)).astype(o_ref.dtype)
        return h[rows_t - 1:rows_t], x[rows_t - halo:]

    init = (jnp.zeros((1, w), F32), jnp.zeros((halo, w), F32))
    lax.fori_loop(0, seq // rows_t, body, init)


def _rglru(proj, conv_w, conv_b, w_x, b_x, w_a, b_a, lam, *, batch, seq, width, rows_t=256):
    t = proj.shape[0]
    wb = RG_BLOCK
    nb = width // wb
    kern = functools.partial(_rglru_kernel, rows_t=rows_t)
    vec = lambda v: v.reshape(1, width)
    vspec = pl.BlockSpec((1, wb), lambda b, g: (0, g))
    mspec = pl.BlockSpec((None, wb, wb), lambda b, g: (g, 0, 0))
    return pl.pallas_call(
        kern,
        grid=(batch, nb),
        in_specs=[pl.BlockSpec((seq, wb), lambda b, g: (b, nb + g)),
                  pl.BlockSpec((seq, wb), lambda b, g: (b, g)),
                  pl.BlockSpec((RG_CONV, wb), lambda b, g: (0, g)),
                  vspec, mspec, vspec, mspec, vspec, vspec],
        out_specs=pl.BlockSpec((seq, wb), lambda b, g: (b, g)),
        out_shape=jax.ShapeDtypeStruct((t, width), BF16),
        compiler_params=_params("parallel", "parallel"),
        name="rglru",
    )(proj, proj, conv_w, vec(conv_b), w_x, vec(b_x), w_a, vec(b_a), vec(lam))


_TM = 1024
_TN = 512


_FOX_TQ = 512


def _ffn(h, g, w_gate, w_up, w_down_bf16, layer):
    u = _rmsnorm(h, g, BF16)
    hid = _matmul_swiglu(u, w_gate, w_up, layer, tm=_TM, tn=256)
    return _matmul_res_bf16w(hid, w_down_bf16, layer, h, tm=512, tn=_TN)


def _even_mixer(h, g, w_in, i, w_gate_up, b_gate, g_norm, b_f, w_out, *, batch, seq):
    d = h.shape[1]
    dk, dv = d // 16, d // 8
    qk, vw, fw = GLA_HEADS * dk, GLA_HEADS * dv, d // 2
    fox_heads = fw // FOX_HEAD_DIM
    rank = GLA_GATE_RANK
    main0 = 2 * qk + 2 * vw
    main1 = 3 * fw
    u, gates = _gate_proj(h, g, w_in, i, col_a=main0, n_a=rank,
                          col_b=main0 + rank + main1, n_b=fox_heads)
    proj = _in_proj_shift(u, w_in, i, tm=_TM, tn=_TN, n_plain=main0 // _TN,
                          n_total=(main0 + main1) // _TN, shift=rank)
    bias_row = jnp.zeros((1, LANES), F32).at[0, rank:rank + fox_heads].set(b_f)
    c, ct = _fox_cumsum(gates, bias_row, batch=batch, seq=seq, lane0=rank, heads=fox_heads)
    ck = ct.reshape(batch, fox_heads, seq // _FOX_TQ, 1, _FOX_TQ)
    o_fox = _fox_attention(proj, c, ck, batch=batch, seq=seq, heads=fox_heads, col0=main0,
                           lane0=rank, tq=_FOX_TQ)
    wup_pad = jnp.pad(w_gate_up, ((0, LANES - rank), (0, 0)))
    o_gla = _gla(proj, gates, wup_pad, b_gate.reshape(1, qk), g_norm.reshape(1, dv),
                 batch=batch, seq=seq, heads=GLA_HEADS, dk=dk, dv=dv)
    return _matmul_res2(o_gla, o_fox, w_out, i, h, tm=_TM, tn=_TN)


def _odd_mixer(h, g, w_in, i, conv_w, conv_b, w_x, b_x, w_a, b_a, lam, w_out, *, batch, seq):
    d = h.shape[1]
    width = d // 2
    sb_heads = width // SB_HEAD_DIM
    u = _rmsnorm(h, g, BF16)
    proj = _matmul(u, w_in, i, tm=_TM, tn=_TN, out_dtype=BF16)
    o_rg = _rglru(proj, conv_w, conv_b, w_x, b_x, w_a, b_a, lam, batch=batch, seq=seq,
                  width=width)
    o_sb = _sb_attention(proj, batch=batch, seq=seq, heads=sb_heads, col0=2 * width)
    return _matmul_res2(o_rg, o_sb, w_out, i, h, tm=_TM, tn=_TN)


def kernel(x, norm_mix, norm_ffn, ffn_w_gate, ffn_w_up, ffn_w_down, ab_w_in, gla_w_gate_up,
           gla_b_gate, gla_norm, fox_b_f, ab_w_out, cd_w_in, rg_conv_w, rg_conv_b, rg_w_x,
           rg_b_x, rg_w_a, rg_b_a, rg_lambda, cd_w_out, final_norm):
    batch, seq, d = x.shape
    h = x.reshape(batch * seq, d)
    w_down_bf16 = ffn_w_down.astype(BF16)
    for layer in range(norm_mix.shape[0]):
        i = layer // 2
        if layer % 2 == 0:
            h = _even_mixer(h, norm_mix[layer], ab_w_in, i, gla_w_gate_up[i], gla_b_gate[i],
                            gla_norm[i], fox_b_f[i], ab_w_out, batch=batch, seq=seq)
        else:
            h = _odd_mixer(h, norm_mix[layer], cd_w_in, i, rg_conv_w[i], rg_conv_b[i],
                           rg_w_x[i], rg_b_x[i], rg_w_a[i], rg_b_a[i], rg_lambda[i],
                           cd_w_out, batch=batch, seq=seq)
        h = _ffn(h, norm_ffn[layer], ffn_w_gate, ffn_w_up, w_down_bf16, layer)
    return _rmsnorm(h, final_norm, F32).reshape(batch, seq, d)
```

```python
import functools
import math

import jax
import jax.numpy as jnp
from jax import lax
from jax.experimental import pallas as pl
from jax.experimental.pallas import tpu as pltpu

F32 = jnp.float32
BF16 = jnp.bfloat16

EPS = 1e-6
LANES = 128
VMEM_LIMIT = 56 * 1024 * 1024

GLA_HEADS = 4
GLA_GATE_RANK = 16
GLA_GATE_TAU = 16.0
GLA_CHUNK = 64
GLA_SUB = 16
FOX_HEAD_DIM = 128
SB_HEAD_DIM = 128
RG_BLOCK = 256
RG_CONV = 4
RG_C = 8.0

NEG_BIG = -1e30
_NT = (((1,), (1,)), ((), ()))


def _params(*sem):
    return pltpu.CompilerParams(dimension_semantics=sem, vmem_limit_bytes=VMEM_LIMIT)


def _dot(a, b):
    return jnp.dot(a, b, preferred_element_type=F32)


def _dot_nt(a, b):
    return lax.dot_general(a, b, _NT, preferred_element_type=F32)


def _split_bf16(x, n):
    parts = []
    r = x
    for _ in range(n):
        p = r.astype(BF16)
        parts.append(p)
        r = r - p.astype(F32)
    return parts


def _dot_f32(a, b):
    ah, al = _split_bf16(a, 2)
    bh, bl = _split_bf16(b, 2)
    return _dot(ah, bh) + _dot(ah, bl) + _dot(al, bh)


def _dot_exact_lhs(m01, x, n=3):
    acc = None
    for p in _split_bf16(x, n):
        t = _dot(m01, p)
        acc = t if acc is None else acc + t
    return acc


def _softplus_neg_abs(z):
    return jnp.log1p(jnp.exp(-jnp.abs(z)))


def _log_sigmoid(z):
    return jnp.minimum(z, 0.0) - _softplus_neg_abs(z)


def _sigmoid(z):
    return 1.0 / (1.0 + jnp.exp(-z))


def _gelu_tanh(y):
    c = math.sqrt(2.0 / math.pi)
    return y * (0.5 * (1.0 + jnp.tanh(c * (y + 0.044715 * (y * y * y)))))


def _rmsnorm_kernel(x_ref, g_ref, o_ref):
    x = x_ref[...]
    ms = jnp.mean(x * x, axis=-1, keepdims=True)
    o_ref[...] = (x * lax.rsqrt(ms + EPS) * g_ref[...]).astype(o_ref.dtype)


def _rmsnorm(x, g, out_dtype, tm=512):
    t, d = x.shape
    return pl.pallas_call(
        _rmsnorm_kernel,
        grid=(t // tm,),
        in_specs=[pl.BlockSpec((tm, d), lambda i: (i, 0)),
                  pl.BlockSpec((1, d), lambda i: (0, 0))],
        out_specs=pl.BlockSpec((tm, d), lambda i: (i, 0)),
        out_shape=jax.ShapeDtypeStruct((t, d), out_dtype),
        compiler_params=_params("parallel"),
        name="rmsnorm",
    )(x, g.reshape(1, d))


def _gate_proj_kernel(x_ref, g_ref, wa_ref, wb_ref, u_ref, o_ref, *, n_a, n_b):
    x = x_ref[...]
    ms = jnp.mean(x * x, axis=-1, keepdims=True)
    u = x * lax.rsqrt(ms + EPS) * g_ref[...]
    u_ref[...] = u.astype(u_ref.dtype)
    lane = lax.broadcasted_iota(jnp.int32, o_ref.shape, 1)
    oa = _dot_f32(u, wa_ref[...])
    ob = _dot_f32(u, wb_ref[...])
    o_ref[...] = jnp.where(lane < n_a, oa, jnp.where(lane < n_a + n_b, ob, 0.0))


def _gate_proj(x, g, w, layer, *, col_a, n_a, col_b, n_b, tm=256):
    t, d = x.shape
    assert col_a % LANES == 0 and col_b % LANES == n_a
    kern = functools.partial(_gate_proj_kernel, n_a=n_a, n_b=n_b)
    return pl.pallas_call(
        kern,
        grid=(t // tm,),
        in_specs=[pl.BlockSpec((tm, d), lambda i: (i, 0)),
                  pl.BlockSpec((1, d), lambda i: (0, 0)),
                  pl.BlockSpec((None, d, LANES), lambda i: (layer, 0, col_a // LANES)),
                  pl.BlockSpec((None, d, LANES), lambda i: (layer, 0, col_b // LANES))],
        out_specs=[pl.BlockSpec((tm, d), lambda i: (i, 0)),
                   pl.BlockSpec((tm, LANES), lambda i: (i, 0))],
        out_shape=[jax.ShapeDtypeStruct((t, d), BF16),
                   jax.ShapeDtypeStruct((t, LANES), F32)],
        compiler_params=_params("parallel"),
        name="gate_proj",
    )(x, g.reshape(1, d), w, w)


_CAST_ROWS = 512


def _cast_rows(w_ref, wb_ref):
    k = w_ref.shape[0]

    def body(r, c):
        rows = pl.ds(pl.multiple_of(r * _CAST_ROWS, _CAST_ROWS), _CAST_ROWS)
        wb_ref[rows, :] = w_ref[rows, :].astype(BF16)
        return c

    lax.fori_loop(0, k // _CAST_ROWS, body, 0)


def _mm_kernel(a_ref, w_ref, o_ref, wb_ref):
    @pl.when(pl.program_id(1) == 0)
    def _():
        _cast_rows(w_ref, wb_ref)

    o_ref[...] = _dot(a_ref[...], wb_ref[...]).astype(o_ref.dtype)


def _mm_res2_kernel(a1_ref, a2_ref, w_ref, r_ref, o_ref, wb_ref):
    @pl.when(pl.program_id(1) == 0)
    def _():
        _cast_rows(w_ref, wb_ref)

    k1 = a1_ref.shape[1]
    o_ref[...] = (r_ref[...] + _dot(a1_ref[...], wb_ref[:k1, :])
                  + _dot(a2_ref[...], wb_ref[k1:, :]))


def _mm_res_bf16w_kernel(a_ref, w_ref, r_ref, o_ref):
    o_ref[...] = r_ref[...] + _dot(a_ref[...], w_ref[...])


def _mm_swiglu_kernel(a_ref, wg_ref, wu_ref, o_ref, wgb_ref, wub_ref):
    @pl.when(pl.program_id(1) == 0)
    def _():
        _cast_rows(wg_ref, wgb_ref)
        _cast_rows(wu_ref, wub_ref)

    a = a_ref[...]
    g = _dot(a, wgb_ref[...])
    u = _dot(a, wub_ref[...])
    o_ref[...] = (g * _sigmoid(g) * u).astype(o_ref.dtype)


def _in_proj_shift_kernel(a_ref, w_ref, wx_ref, o_ref, wb_ref, *, n_plain, shift):
    j = pl.program_id(0)
    first = pl.program_id(1) == 0
    k, tn = w_ref.shape
    width = tn + wx_ref.shape[1]

    @pl.when(first & (j < n_plain))
    def _():
        _cast_rows(w_ref, wb_ref)

    @pl.when(first & (j >= n_plain))
    def _():
        def body(r, c):
            rows = pl.ds(pl.multiple_of(r * _CAST_ROWS, _CAST_ROWS), _CAST_ROWS)
            cat = jnp.concatenate([w_ref[rows, :], wx_ref[rows, :]], axis=1)
            wb_ref[rows, :] = pltpu.roll(cat, width - shift, axis=1)[:, :tn].astype(BF16)
            return c

        lax.fori_loop(0, k // _CAST_ROWS, body, 0)

    o_ref[...] = _dot(a_ref[...], wb_ref[...]).astype(o_ref.dtype)


def _mm_as_kernel(a_ref, w_ref, o_ref):
    o_ref[...] = _dot(a_ref[...], w_ref[...].astype(BF16)).astype(o_ref.dtype)


def _mm_as_swiglu_kernel(a_ref, wg_ref, wu_ref, o_ref):
    a = a_ref[...]
    g = _dot(a, wg_ref[...].astype(BF16))
    u = _dot(a, wu_ref[...].astype(BF16))
    o_ref[...] = (g * _sigmoid(g) * u).astype(o_ref.dtype)


def _in_proj_as_kernel(a_ref, w_ref, wx_ref, o_ref, *, n_plain, shift):
    j = pl.program_id(1)
    tn = w_ref.shape[1]
    width = tn + wx_ref.shape[1]

    @pl.when(j < n_plain)
    def _():
        o_ref[...] = _dot(a_ref[...], w_ref[...].astype(BF16)).astype(o_ref.dtype)

    @pl.when(j >= n_plain)
    def _():
        cat = jnp.concatenate([w_ref[...], wx_ref[...]], axis=1)
        w = pltpu.roll(cat, width - shift, axis=1)[:, :tn].astype(BF16)
        o_ref[...] = _dot(a_ref[...], w).astype(o_ref.dtype)


def _a_spec(tm, k):
    return pl.BlockSpec((tm, k), lambda i, j: (i, 0), pipeline_mode=pl.Buffered(1))


def _wt_spec(layer, k, tn):
    return pl.BlockSpec((None, k, tn), lambda i, j: (layer, 0, j))


def _matmul_as(a, w, layer, *, tm, tn, out_dtype):
    m, k = a.shape
    n = w.shape[2]
    return pl.pallas_call(
        _mm_as_kernel,
        grid=(m // tm, n // tn),
        in_specs=[_a_spec(tm, k), _wt_spec(layer, k, tn)],
        out_specs=pl.BlockSpec((tm, tn), lambda i, j: (i, j)),
        out_shape=jax.ShapeDtypeStruct((m, n), out_dtype),
        compiler_params=_params("parallel", "arbitrary"),
        name="matmul_as",
    )(a, w)


def _matmul_as_swiglu(a, wg, wu, layer, *, tm, tn):
    m, k = a.shape
    n = wg.shape[2]
    return pl.pallas_call(
        _mm_as_swiglu_kernel,
        grid=(m // tm, n // tn),
        in_specs=[_a_spec(tm, k), _wt_spec(layer, k, tn), _wt_spec(layer, k, tn)],
        out_specs=pl.BlockSpec((tm, tn), lambda i, j: (i, j)),
        out_shape=jax.ShapeDtypeStruct((m, n), BF16),
        compiler_params=_params("parallel", "arbitrary"),
        name="matmul_as_swiglu",
    )(a, wg, wu)


def _mm_as_res_kernel(a_ref, w_ref, r_ref, o_ref):
    o_ref[...] = r_ref[...] + _dot(a_ref[...], w_ref[...].astype(BF16))


def _matmul_as_res(a, w, layer, res, *, tm, tn):
    m, k = a.shape
    n = w.shape[2]
    return pl.pallas_call(
        _mm_as_res_kernel,
        grid=(m // tm, n // tn),
        in_specs=[_a_spec(tm, k), _wt_spec(layer, k, tn),
                  pl.BlockSpec((tm, tn), lambda i, j: (i, j))],
        out_specs=pl.BlockSpec((tm, tn), lambda i, j: (i, j)),
        out_shape=jax.ShapeDtypeStruct((m, n), F32),
        compiler_params=_params("parallel", "arbitrary"),
        name="matmul_as_res",
    )(a, w, res)


def _in_proj_as(a, w, layer, *, tm, tn, n_plain, n_total, shift):
    m, k = a.shape
    nx = tn // LANES
    kern = functools.partial(_in_proj_as_kernel, n_plain=n_plain, shift=shift)
    return pl.pallas_call(
        kern,
        grid=(m // tm, n_total),
        in_specs=[_a_spec(tm, k), _wt_spec(layer, k, tn),
                  pl.BlockSpec((None, k, LANES), lambda i, j: (layer, 0, (j + 1) * nx))],
        out_specs=pl.BlockSpec((tm, tn), lambda i, j: (i, j)),
        out_shape=jax.ShapeDtypeStruct((m, n_total * tn), BF16),
        compiler_params=_params("parallel", "arbitrary"),
        name="in_proj_as",
    )(a, w, w)


def _w_spec(layer, k, tn):
    return pl.BlockSpec((None, k, tn), lambda j, i: (layer, 0, j))


def _matmul(a, w, layer, *, tm, tn, out_dtype):
    m, k = a.shape
    n = w.shape[2]
    return pl.pallas_call(
        _mm_kernel,
        grid=(n // tn, m // tm),
        in_specs=[pl.BlockSpec((tm, k), lambda j, i: (i, 0)), _w_spec(layer, k, tn)],
        out_specs=pl.BlockSpec((tm, tn), lambda j, i: (i, j)),
        out_shape=jax.ShapeDtypeStruct((m, n), out_dtype),
        scratch_shapes=[pltpu.VMEM((k, tn), BF16)],
        compiler_params=_params("arbitrary", "arbitrary"),
        name="matmul",
    )(a, w)


def _matmul_res2(a1, a2, w, layer, res, *, tm, tn):
    m, k1 = a1.shape
    k2 = a2.shape[1]
    k, n = w.shape[1:]
    assert k == k1 + k2
    return pl.pallas_call(
        _mm_res2_kernel,
        grid=(n // tn, m // tm),
        in_specs=[pl.BlockSpec((tm, k1), lambda j, i: (i, 0)),
                  pl.BlockSpec((tm, k2), lambda j, i: (i, 0)),
                  _w_spec(layer, k, tn),
                  pl.BlockSpec((tm, tn), lambda j, i: (i, j))],
        out_specs=pl.BlockSpec((tm, tn), lambda j, i: (i, j)),
        out_shape=jax.ShapeDtypeStruct((m, n), F32),
        scratch_shapes=[pltpu.VMEM((k, tn), BF16)],
        compiler_params=_params("arbitrary", "arbitrary"),
        name="matmul_res2",
    )(a1, a2, w, res)


def _matmul_res_bf16w(a, w, layer, res, *, tm, tn):
    m, k = a.shape
    n = w.shape[2]
    return pl.pallas_call(
        _mm_res_bf16w_kernel,
        grid=(n // tn, m // tm),
        in_specs=[pl.BlockSpec((tm, k), lambda j, i: (i, 0)),
                  _w_spec(layer, k, tn),
                  pl.BlockSpec((tm, tn), lambda j, i: (i, j))],
        out_specs=pl.BlockSpec((tm, tn), lambda j, i: (i, j)),
        out_shape=jax.ShapeDtypeStruct((m, n), F32),
        compiler_params=_params("arbitrary", "arbitrary"),
        name="matmul_res_bf16w",
    )(a, w, res)


def _matmul_swiglu(a, wg, wu, layer, *, tm, tn):
    m, k = a.shape
    n = wg.shape[2]
    return pl.pallas_call(
        _mm_swiglu_kernel,
        grid=(n // tn, m // tm),
        in_specs=[pl.BlockSpec((tm, k), lambda j, i: (i, 0)),
                  _w_spec(layer, k, tn), _w_spec(layer, k, tn)],
        out_specs=pl.BlockSpec((tm, tn), lambda j, i: (i, j)),
        out_shape=jax.ShapeDtypeStruct((m, n), BF16),
        scratch_shapes=[pltpu.VMEM((k, tn), BF16), pltpu.VMEM((k, tn), BF16)],
        compiler_params=_params("arbitrary", "arbitrary"),
        name="matmul_swiglu",
    )(a, wg, wu)


def _in_proj_shift(a, w, layer, *, tm, tn, n_plain, n_total, shift):
    m, k = a.shape
    nx = tn // LANES
    kern = functools.partial(_in_proj_shift_kernel, n_plain=n_plain, shift=shift)
    return pl.pallas_call(
        kern,
        grid=(n_total, m // tm),
        in_specs=[pl.BlockSpec((tm, k), lambda j, i: (i, 0)),
                  _w_spec(layer, k, tn),
                  pl.BlockSpec((None, k, LANES), lambda j, i: (layer, 0, (j + 1) * nx))],
        out_specs=pl.BlockSpec((tm, tn), lambda j, i: (i, j)),
        out_shape=jax.ShapeDtypeStruct((m, n_total * tn), BF16),
        scratch_shapes=[pltpu.VMEM((k, tn), BF16)],
        compiler_params=_params("arbitrary", "arbitrary"),
        name="in_proj_shift",
    )(a, w, w)


def _fox_cumsum_kernel(g_ref, b_ref, c_ref, ct_ref, c_sc, *, blk, lane0, heads):
    s = g_ref.shape[0]
    row = lax.broadcasted_iota(jnp.int32, (blk, blk), 0)
    col = lax.broadcasted_iota(jnp.int32, (blk, blk), 1)
    tri = (col <= row).astype(BF16)
    carry = jnp.zeros((1, g_ref.shape[1]), F32)
    for n in range(s // blk):
        rows = slice(n * blk, (n + 1) * blk)
        lf = _log_sigmoid(g_ref[rows, :] + b_ref[...])
        c = _dot_exact_lhs(tri, lf) + carry
        c_sc[rows, :] = c
        carry = c[blk - 1:blk, :]
    c_all = c_sc[...]
    c_ref[...] = c_all
    ct_ref[...] = c_all.T[lane0:lane0 + heads, :]


def _fox_cumsum(gates, bias_row, *, batch, seq, lane0, heads, blk=256):
    t, n = gates.shape
    kern = functools.partial(_fox_cumsum_kernel, blk=blk, lane0=lane0, heads=heads)
    return pl.pallas_call(
        kern,
        grid=(batch,),
        in_specs=[pl.BlockSpec((seq, n), lambda b: (b, 0)),
                  pl.BlockSpec((1, n), lambda b: (0, 0))],
        out_specs=[pl.BlockSpec((seq, n), lambda b: (b, 0)),
                   pl.BlockSpec((None, heads, seq), lambda b: (b, 0, 0))],
        out_shape=[jax.ShapeDtypeStruct((t, n), F32),
                   jax.ShapeDtypeStruct((batch, heads, seq), F32)],
        scratch_shapes=[pltpu.VMEM((seq, n), F32)],
        compiler_params=_params("parallel"),
        name="fox_cumsum",
    )(gates, bias_row)


def _fox_kernel(q_ref, k_ref, v_ref, c_ref, ck_ref, o_ref, *, tq, lane0):
    h = pl.program_id(1)
    qi = pl.program_id(2)
    dh = q_ref.shape[1]
    scale = dh ** -0.5
    q = q_ref[...]
    lane = lax.broadcasted_iota(jnp.int32, c_ref.shape, 1)
    cq = jnp.sum(jnp.where(lane == lane0 + h, c_ref[...], 0.0), axis=1, keepdims=True)

    def logits(kb):
        rows = pl.ds(pl.multiple_of(kb * tq, tq), tq)
        s = _dot_nt(q, k_ref[rows, :]) * scale
        return s + cq - ck_ref[kb], v_ref[rows, :]

    s, v = logits(qi)
    tpos = lax.broadcasted_iota(jnp.int32, (tq, tq), 0)
    spos = lax.broadcasted_iota(jnp.int32, (tq, tq), 1)
    s = jnp.where(spos <= tpos, s, NEG_BIG)
    m0 = jnp.max(s, axis=1, keepdims=True)
    p = jnp.exp(s - m0)
    l0 = jnp.sum(p, axis=1, keepdims=True)
    acc0 = _dot(p.astype(BF16), v)

    def body(kb, carry):
        m, l, acc = carry
        s, v = logits(kb)
        m_new = jnp.maximum(m, jnp.max(s, axis=1, keepdims=True))
        alpha = jnp.exp(m - m_new)
        p = jnp.exp(s - m_new)
        l = alpha * l + jnp.sum(p, axis=1, keepdims=True)
        acc = alpha * acc + _dot(p.astype(BF16), v)
        return m_new, l, acc

    _, l, acc = lax.fori_loop(0, qi, body, (m0, l0, acc0))
    o_ref[...] = (acc / l).astype(o_ref.dtype)


def _fox_attention(proj, c, ck, *, batch, seq, heads, col0, lane0, tq=256):
    t = proj.shape[0]
    dh = FOX_HEAD_DIM
    nq = seq // tq
    qb = col0 // dh
    kern = functools.partial(_fox_kernel, tq=tq, lane0=lane0)
    return pl.pallas_call(
        kern,
        grid=(batch, heads, nq),
        in_specs=[pl.BlockSpec((tq, dh), lambda b, h, i: (b * nq + i, qb + h)),
                  pl.BlockSpec((seq, dh), lambda b, h, i: (b, qb + heads + h)),
                  pl.BlockSpec((seq, dh), lambda b, h, i: (b, qb + 2 * heads + h)),
                  pl.BlockSpec((tq, c.shape[1]), lambda b, h, i: (b * nq + i, 0)),
                  pl.BlockSpec((None, None, nq, 1, tq), lambda b, h, i: (b, h, 0, 0, 0))],
        out_specs=pl.BlockSpec((tq, dh), lambda b, h, i: (b * nq + i, h)),
        out_shape=jax.ShapeDtypeStruct((t, heads * dh), BF16),
        compiler_params=_params("parallel", "parallel", "arbitrary"),
        name="fox_attention",
    )(proj, proj, proj, c, ck)


def _sb_kernel(q_ref, k_ref, v_ref, o_ref, *, tq, tk):
    qi = pl.program_id(2)
    dh = q_ref.shape[1]
    scale = dh ** -0.5
    q = q_ref[...]
    nd = tq // tk
    jj = lax.broadcasted_iota(jnp.int32, (tk, tk), 0)
    ss = lax.broadcasted_iota(jnp.int32, (tk, tk), 1)
    upper = (jj > ss).astype(BF16)
    row = lax.broadcasted_iota(jnp.int32, (tq, tk), 0)
    col = lax.broadcasted_iota(jnp.int32, (tq, tk), 1)

    def block(kb, run, mask):
        rows = pl.ds(pl.multiple_of(kb * tk, tk), tk)
        z = _dot_nt(q, k_ref[rows, :]) * scale
        sp = jnp.log(1.0 + jnp.exp(-jnp.abs(z)))
        log_beta = jnp.minimum(z, 0.0) - sp
        log_1m = log_beta - z
        if mask is not None:
            log_1m = jnp.where(mask, log_1m, 0.0)
        hi, lo = _split_bf16(log_1m, 2)
        suffix = _dot(hi, upper) + _dot(lo, upper)
        a = jnp.exp(log_beta + suffix + run)
        if mask is not None:
            a = jnp.where(mask, a, 0.0)
        pv = _dot(a.astype(BF16), v_ref[rows, :])
        return pv, jnp.sum(log_1m, axis=1, keepdims=True)

    run = jnp.zeros((tq, 1), F32)
    acc = jnp.zeros((tq, dh), F32)
    for d in range(nd):
        off = (nd - 1 - d) * tk
        pv, rs = block((qi + 1) * nd - 1 - d, run, (off + col) < row)
        acc = acc + pv
        run = run + rs

    def body(p, carry):
        run, acc = carry
        for d in range(nd):
            pv, rs = block((qi - p) * nd - 1 - d, run, None)
            acc = acc + pv
            run = run + rs
        return run, acc

    _, acc = lax.fori_loop(0, qi, body, (run, acc))
    o_ref[...] = acc.astype(o_ref.dtype)


def _sb_attention(proj, *, batch, seq, heads, col0, tq=512, tk=256):
    t = proj.shape[0]
    dh = SB_HEAD_DIM
    nq = seq // tq
    qb = col0 // dh
    kern = functools.partial(_sb_kernel, tq=tq, tk=tk)
    return pl.pallas_call(
        kern,
        grid=(batch, heads, nq),
        in_specs=[pl.BlockSpec((tq, dh), lambda b, h, i: (b * nq + i, qb + h)),
                  pl.BlockSpec((seq, dh), lambda b, h, i: (b, qb + heads + h)),
                  pl.BlockSpec((seq, dh), lambda b, h, i: (b, qb + 2 * heads + h))],
        out_specs=pl.BlockSpec((tq, dh), lambda b, h, i: (b * nq + i, h)),
        out_shape=jax.ShapeDtypeStruct((t, heads * dh), BF16),
        compiler_params=_params("parallel", "parallel", "arbitrary"),
        name="sb_attention",
    )(proj, proj, proj)


def _gla_kernel(q_ref, k_ref, v_ref, gg_ref, gate_ref, wup_ref, bg_ref, gn_ref, o_ref,
                state_ref, osc_ref, b_sc, *, chunk, sub):
    rows_blk, dk = q_ref.shape
    dv = v_ref.shape[1]
    n_chunks = rows_blk // chunk
    n_sub = chunk // sub
    q_scale = dk ** -0.5

    @pl.when(pl.program_id(2) == 0)
    def _():
        state_ref[...] = jnp.zeros_like(state_ref)

    z = _dot_f32(gate_ref[...], wup_ref[...]) + bg_ref[...]
    a = _log_sigmoid(z) / GLA_GATE_TAU
    row = lax.broadcasted_iota(jnp.int32, (rows_blk, rows_blk), 0)
    col = lax.broadcasted_iota(jnp.int32, (rows_blk, rows_blk), 1)
    shift = int(math.log2(chunk))
    same = lax.shift_right_logical(row, shift) == lax.shift_right_logical(col, shift)
    tri = ((col <= row) & same).astype(BF16)
    b = _dot_exact_lhs(tri, a)

    key_row = lax.broadcasted_iota(jnp.int32, (chunk, 1), 0)
    for c in range(n_chunks):
        r0 = c * chunk
        bc = b[r0:r0 + chunk]
        qc = q_ref[r0:r0 + chunk, :].astype(F32) * q_scale
        kc = k_ref[r0:r0 + chunk, :].astype(F32)
        vc = v_ref[r0:r0 + chunk, :]
        state = state_ref[...]
        inter = _dot((qc * jnp.exp(bc)).astype(BF16), state.astype(BF16))
        outs = [inter[0:sub]]
        for i in range(1, n_sub):
            lo, hi = i * sub, (i + 1) * sub
            ref = bc[lo - 1:lo]
            qi = (qc[lo:hi] * jnp.exp(bc[lo:hi] - ref)).astype(BF16)
            kdec = jnp.exp(jnp.minimum(ref - bc, 0.0))
            ki = jnp.where(key_row < lo, kc * kdec, 0.0).astype(BF16)
            s = _dot_nt(qi, ki)
            outs.append(inter[lo:hi] + _dot(s.astype(BF16), vc))
        osc_ref[r0:r0 + chunk, :] = jnp.concatenate(outs, axis=0)

        bl = bc[chunk - 1:chunk]
        kd = kc * jnp.exp(bl - bc)
        stacked = jnp.concatenate([kd, jnp.broadcast_to(jnp.exp(bl), (chunk, dk))], axis=0)
        st = stacked.T
        vpad = jnp.concatenate([vc, jnp.zeros_like(vc)], axis=0)
        state_ref[...] = state * st[:, chunk:chunk + 1] + _dot(st.astype(BF16), vpad)

    b_sc[...] = b
    trow = lax.broadcasted_iota(jnp.int32, (sub, 1), 0)

    def diag(sb, carry):
        rows = pl.ds(pl.multiple_of(sb * sub, sub), sub)
        qs = q_ref[rows, :].astype(F32) * q_scale
        ks = k_ref[rows, :].astype(F32)
        vs = v_ref[rows, :].astype(F32)
        bs = b_sc[rows, :]
        acc = jnp.zeros((sub, dv), F32)
        for s_ in range(sub):
            d = jnp.where(trow >= s_, bs - bs[s_:s_ + 1], NEG_BIG)
            p = qs * ks[s_:s_ + 1] * jnp.exp(d)
            acc = acc + jnp.sum(p, axis=1, keepdims=True) * vs[s_:s_ + 1]
        osc_ref[rows, :] = osc_ref[rows, :] + acc
        return carry

    lax.fori_loop(0, rows_blk // sub, diag, 0)

    o = osc_ref[...]
    ms = jnp.mean(o * o, axis=-1, keepdims=True)
    y = o * lax.rsqrt(ms + EPS) * gn_ref[...]
    g = gg_ref[...].astype(F32)
    o_ref[...] = (y * (g * _sigmoid(g))).astype(o_ref.dtype)


def _gla(proj, gates, wup_pad, b_gate, g_norm, *, batch, seq, heads, dk, dv, rows_blk=256):
    t = proj.shape[0]
    nr = seq // rows_blk
    kb0 = heads
    vb0 = 2 * heads * dk // dv
    gb0 = vb0 + heads
    kern = functools.partial(_gla_kernel, chunk=GLA_CHUNK, sub=GLA_SUB)
    return pl.pallas_call(
        kern,
        grid=(batch, heads, nr),
        in_specs=[pl.BlockSpec((rows_blk, dk), lambda b, h, r: (b * nr + r, h)),
                  pl.BlockSpec((rows_blk, dk), lambda b, h, r: (b * nr + r, kb0 + h)),
                  pl.BlockSpec((rows_blk, dv), lambda b, h, r: (b * nr + r, vb0 + h)),
                  pl.BlockSpec((rows_blk, dv), lambda b, h, r: (b * nr + r, gb0 + h)),
                  pl.BlockSpec((rows_blk, gates.shape[1]), lambda b, h, r: (b * nr + r, 0)),
                  pl.BlockSpec((wup_pad.shape[0], dk), lambda b, h, r: (0, h)),
                  pl.BlockSpec((1, dk), lambda b, h, r: (0, h)),
                  pl.BlockSpec((1, dv), lambda b, h, r: (0, 0))],
        out_specs=pl.BlockSpec((rows_blk, dv), lambda b, h, r: (b * nr + r, h)),
        out_shape=jax.ShapeDtypeStruct((t, heads * dv), BF16),
        scratch_shapes=[pltpu.VMEM((dk, dv), F32),
                        pltpu.VMEM((rows_blk, dv), F32),
                        pltpu.VMEM((rows_blk, dk), F32)],
        compiler_params=_params("parallel", "parallel", "arbitrary"),
        name="gla",
    )(proj, proj, proj, proj, gates, wup_pad, b_gate, g_norm)


def _rglru_kernel(rx_ref, ry_ref, cw_ref, cb_ref, wx_ref, bx_ref, wa_ref, ba_ref, lam_ref,
                  o_ref, *, rows_t):
    seq, w = rx_ref.shape
    halo = 8
    wxb = wx_ref[...].astype(BF16)
    wab = wa_ref[...].astype(BF16)
    cw = cw_ref[...]
    neg_lam = -lam_ref[...]
    softplus = jnp.maximum(neg_lam, 0.0) + _softplus_neg_abs(neg_lam)
    coef = -RG_C * softplus
    row = lax.broadcasted_iota(jnp.int32, (rows_t, 1), 0)

    def body(t, carry):
        h_prev, tail = carry
        rows = pl.ds(pl.multiple_of(t * rows_t, rows_t), rows_t)
        x = rx_ref[rows, :].astype(F32)
        xe = jnp.concatenate([tail, x], axis=0)
        xc = cb_ref[...] + cw[RG_CONV - 1:RG_CONV] * x
        for s_ in range(1, RG_CONV):
            xc = xc + cw[RG_CONV - 1 - s_:RG_CONV - s_] * pltpu.roll(xe, s_, axis=0)[halo:]
        xcb = xc.astype(BF16)
        gate_x = _sigmoid(_dot(xcb, wxb) + bx_ref[...])
        gate_a = _sigmoid(_dot(xcb, wab) + ba_ref[...])
        log_a = coef * gate_a
        a_dec = jnp.exp(log_a)
        one_m_a2 = -jnp.tanh(log_a) * (a_dec * a_dec + 1.0)
        u = jnp.sqrt(one_m_a2) * (gate_x * xc)
        d = 1
        while d < rows_t:
            if d < halo:
                a_sh = jnp.where(row >= d, pltpu.roll(a_dec, d, axis=0), 1.0)
                u_sh = jnp.where(row >= d, pltpu.roll(u, d, axis=0), 0.0)
            else:
                a_sh = jnp.concatenate([jnp.ones((d, w), F32), a_dec[:rows_t - d]], axis=0)
                u_sh = jnp.concatenate([jnp.zeros((d, w), F32), u[:rows_t - d]], axis=0)
            u = a_dec * u_sh + u
            a_dec = a_dec * a_sh
            d *= 2
        h = u + a_dec * h_prev
        y = ry_ref[rows, :].astype(F32)
        o_ref[rows, :] = (h * _gelu_tanh(y)).astype(o_ref.dtype)
        return h[rows_t - 1:rows_t], x[rows_t - halo:]

    init = (jnp.zeros((1, w), F32), jnp.zeros((halo, w), F32))
    lax.fori_loop(0, seq // rows_t, body, init)


def _rglru(proj, conv_w, conv_b, w_x, b_x, w_a, b_a, lam, *, batch, seq, width, rows_t=256):
    t = proj.shape[0]
    wb = RG_BLOCK
    nb = width // wb
    kern = functools.partial(_rglru_kernel, rows_t=rows_t)
    vec = lambda v: v.reshape(1, width)
    vspec = pl.BlockSpec((1, wb), lambda b, g: (0, g))
    mspec = pl.BlockSpec((None, wb, wb), lambda b, g: (g, 0, 0))
    return pl.pallas_call(
        kern,
        grid=(batch, nb),
        in_specs=[pl.BlockSpec((seq, wb), lambda b, g: (b, nb + g)),
                  pl.BlockSpec((seq, wb), lambda b, g: (b, g)),
                  pl.BlockSpec((RG_CONV, wb), lambda b, g: (0, g)),
                  vspec, mspec, vspec, mspec, vspec, vspec],
        out_specs=pl.BlockSpec((seq, wb), lambda b, g: (b, g)),
        out_shape=jax.ShapeDtypeStruct((t, width), BF16),
        compiler_params=_params("parallel", "parallel"),
        name="rglru",
    )(proj, proj, conv_w, vec(conv_b), w_x, vec(b_x), w_a, vec(b_a), vec(lam))


_TM = 1024
_TN = 512


_TM_AS = 2048
_FOX_TQ = 512


def _ffn(h, g, w_gate, w_up, w_down, layer):
    u = _rmsnorm(h, g, BF16)
    hid = _matmul_as_swiglu(u, w_gate, w_up, layer, tm=_TM_AS, tn=256)
    return _matmul_as_res(hid, w_down, layer, h, tm=1024, tn=256)


def _even_mixer(h, g, w_in, i, w_gate_up, b_gate, g_norm, b_f, w_out, *, batch, seq):
    d = h.shape[1]
    dk, dv = d // 16, d // 8
    qk, vw, fw = GLA_HEADS * dk, GLA_HEADS * dv, d // 2
    fox_heads = fw // FOX_HEAD_DIM
    rank = GLA_GATE_RANK
    main0 = 2 * qk + 2 * vw
    main1 = 3 * fw
    u, gates = _gate_proj(h, g, w_in, i, col_a=main0, n_a=rank,
                          col_b=main0 + rank + main1, n_b=fox_heads)
    proj = _in_proj_as(u, w_in, i, tm=_TM_AS, tn=_TN, n_plain=main0 // _TN,
                       n_total=(main0 + main1) // _TN, shift=rank)
    bias_row = jnp.zeros((1, LANES), F32).at[0, rank:rank + fox_heads].set(b_f)
    c, ct = _fox_cumsum(gates, bias_row, batch=batch, seq=seq, lane0=rank, heads=fox_heads)
    ck = ct.reshape(batch, fox_heads, seq // _FOX_TQ, 1, _FOX_TQ)
    o_fox = _fox_attention(proj, c, ck, batch=batch, seq=seq, heads=fox_heads, col0=main0,
                           lane0=rank, tq=_FOX_TQ)
    wup_pad = jnp.pad(w_gate_up, ((0, LANES - rank), (0, 0)))
    o_gla = _gla(proj, gates, wup_pad, b_gate.reshape(1, qk), g_norm.reshape(1, dv),
                 batch=batch, seq=seq, heads=GLA_HEADS, dk=dk, dv=dv)
    return _matmul_res2(o_gla, o_fox, w_out, i, h, tm=_TM, tn=_TN)


def _odd_mixer(h, g, w_in, i, conv_w, conv_b, w_x, b_x, w_a, b_a, lam, w_out, *, batch, seq):
    d = h.shape[1]
    width = d // 2
    sb_heads = width // SB_HEAD_DIM
    u = _rmsnorm(h, g, BF16)
    proj = _matmul_as(u, w_in, i, tm=_TM_AS, tn=_TN, out_dtype=BF16)
    o_rg = _rglru(proj, conv_w, conv_b, w_x, b_x, w_a, b_a, lam, batch=batch, seq=seq,
                  width=width)
    o_sb = _sb_attention(proj, batch=batch, seq=seq, heads=sb_heads, col0=2 * width)
    return _matmul_res2(o_rg, o_sb, w_out, i, h, tm=_TM, tn=_TN)


def kernel(x, norm_mix, norm_ffn, ffn_w_gate, ffn_w_up, ffn_w_down, ab_w_in, gla_w_gate_up,
           gla_b_gate, gla_norm, fox_b_f, ab_w_out, cd_w_in, rg_conv_w, rg_conv_b, rg_w_x,
           rg_b_x, rg_w_a, rg_b_a, rg_lambda, cd_w_out, final_norm):
    batch, seq, d = x.shape
    h = x.reshape(batch * seq, d)
    for layer in range(norm_mix.shape[0]):
        i = layer // 2
        if layer % 2 == 0:
            h = _even_mixer(h, norm_mix[layer], ab_w_in, i, gla_w_gate_up[i], gla_b_gate[i],
                            gla_norm[i], fox_b_f[i], ab_w_out, batch=batch, seq=seq)
        else:
            h = _odd_mixer(h, norm_mix[layer], cd_w_in, i, rg_conv_w[i], rg_conv_b[i],
                           rg_w_x[i], rg_b_x[i], rg_w_a[i], rg_b_a[i], rg_lambda[i],
                           cd_w_out, batch=batch, seq=seq)
        h = _ffn(h, norm_ffn[layer], ffn_w_gate, ffn_w_up, ffn_w_down, layer)
    return _rmsnorm(h, final_norm, F32).reshape(batch, seq, d)
```

```python
import functools
import math

import jax
import jax.numpy as jnp
from jax import lax
from jax.experimental import pallas as pl
from jax.experimental.pallas import tpu as pltpu

F32 = jnp.float32
BF16 = jnp.bfloat16

EPS = 1e-6
LANES = 128
VMEM_LIMIT = 56 * 1024 * 1024

GLA_HEADS = 4
GLA_GATE_RANK = 16
GLA_GATE_TAU = 16.0
GLA_CHUNK = 64
GLA_SUB = 16
FOX_HEAD_DIM = 128
SB_HEAD_DIM = 128
RG_BLOCK = 256
RG_CONV = 4
RG_C = 8.0

NEG_BIG = -1e30
_NT = (((1,), (1,)), ((), ()))


def _params(*sem):
    return pltpu.CompilerParams(dimension_semantics=sem, vmem_limit_bytes=VMEM_LIMIT)


def _dot(a, b):
    return jnp.dot(a, b, preferred_element_type=F32)


def _dot_nt(a, b):
    return lax.dot_general(a, b, _NT, preferred_element_type=F32)


def _split_bf16(x, n):
    parts = []
    r = x
    for _ in range(n):
        p = r.astype(BF16)
        parts.append(p)
        r = r - p.astype(F32)
    return parts


def _dot_f32(a, b):
    ah, al = _split_bf16(a, 2)
    bh, bl = _split_bf16(b, 2)
    return _dot(ah, bh) + _dot(ah, bl) + _dot(al, bh)


def _dot_exact_lhs(m01, x, n=3):
    acc = None
    for p in _split_bf16(x, n):
        t = _dot(m01, p)
        acc = t if acc is None else acc + t
    return acc


def _softplus_neg_abs(z):
    return jnp.log1p(jnp.exp(-jnp.abs(z)))


def _log_sigmoid(z):
    return jnp.minimum(z, 0.0) - _softplus_neg_abs(z)


def _sigmoid(z):
    return 1.0 / (1.0 + jnp.exp(-z))


def _gelu_tanh(y):
    c = math.sqrt(2.0 / math.pi)
    return y * (0.5 * (1.0 + jnp.tanh(c * (y + 0.044715 * (y * y * y)))))


def _rmsnorm_kernel(x_ref, g_ref, o_ref):
    x = x_ref[...]
    ms = jnp.mean(x * x, axis=-1, keepdims=True)
    o_ref[...] = (x * lax.rsqrt(ms + EPS) * g_ref[...]).astype(o_ref.dtype)


def _rmsnorm(x, g, out_dtype, tm=512):
    t, d = x.shape
    return pl.pallas_call(
        _rmsnorm_kernel,
        grid=(t // tm,),
        in_specs=[pl.BlockSpec((tm, d), lambda i: (i, 0)),
                  pl.BlockSpec((1, d), lambda i: (0, 0))],
        out_specs=pl.BlockSpec((tm, d), lambda i: (i, 0)),
        out_shape=jax.ShapeDtypeStruct((t, d), out_dtype),
        compiler_params=_params("parallel"),
        name="rmsnorm",
    )(x, g.reshape(1, d))


def _gate_proj_kernel(x_ref, g_ref, wa_ref, wb_ref, u_ref, o_ref, *, n_a, n_b):
    x = x_ref[...]
    ms = jnp.mean(x * x, axis=-1, keepdims=True)
    u = x * lax.rsqrt(ms + EPS) * g_ref[...]
    u_ref[...] = u.astype(u_ref.dtype)
    lane = lax.broadcasted_iota(jnp.int32, o_ref.shape, 1)
    oa = _dot_f32(u, wa_ref[...])
    ob = _dot_f32(u, wb_ref[...])
    o_ref[...] = jnp.where(lane < n_a, oa, jnp.where(lane < n_a + n_b, ob, 0.0))


def _gate_proj(x, g, w, layer, *, col_a, n_a, col_b, n_b, tm=256):
    t, d = x.shape
    assert col_a % LANES == 0 and col_b % LANES == n_a
    kern = functools.partial(_gate_proj_kernel, n_a=n_a, n_b=n_b)
    return pl.pallas_call(
        kern,
        grid=(t // tm,),
        in_specs=[pl.BlockSpec((tm, d), lambda i: (i, 0)),
                  pl.BlockSpec((1, d), lambda i: (0, 0)),
                  pl.BlockSpec((None, d, LANES), lambda i: (layer, 0, col_a // LANES)),
                  pl.BlockSpec((None, d, LANES), lambda i: (layer, 0, col_b // LANES))],
        out_specs=[pl.BlockSpec((tm, d), lambda i: (i, 0)),
                   pl.BlockSpec((tm, LANES), lambda i: (i, 0))],
        out_shape=[jax.ShapeDtypeStruct((t, d), BF16),
                   jax.ShapeDtypeStruct((t, LANES), F32)],
        compiler_params=_params("parallel"),
        name="gate_proj",
    )(x, g.reshape(1, d), w, w)


_CAST_ROWS = 512


def _cast_rows(w_ref, wb_ref):
    k = w_ref.shape[0]

    def body(r, c):
        rows = pl.ds(pl.multiple_of(r * _CAST_ROWS, _CAST_ROWS), _CAST_ROWS)
        wb_ref[rows, :] = w_ref[rows, :].astype(BF16)
        return c

    lax.fori_loop(0, k // _CAST_ROWS, body, 0)


def _mm_kernel(a_ref, w_ref, o_ref, wb_ref):
    @pl.when(pl.program_id(1) == 0)
    def _():
        _cast_rows(w_ref, wb_ref)

    o_ref[...] = _dot(a_ref[...], wb_ref[...]).astype(o_ref.dtype)


def _mm_res2_kernel(a1_ref, a2_ref, w_ref, r_ref, o_ref, wb_ref):
    @pl.when(pl.program_id(1) == 0)
    def _():
        _cast_rows(w_ref, wb_ref)

    k1 = a1_ref.shape[1]
    o_ref[...] = (r_ref[...] + _dot(a1_ref[...], wb_ref[:k1, :])
                  + _dot(a2_ref[...], wb_ref[k1:, :]))


def _mm_res_bf16w_kernel(a_ref, w_ref, r_ref, o_ref):
    o_ref[...] = r_ref[...] + _dot(a_ref[...], w_ref[...])


def _mm_swiglu_kernel(a_ref, wg_ref, wu_ref, o_ref, wgb_ref, wub_ref):
    @pl.when(pl.program_id(1) == 0)
    def _():
        _cast_rows(wg_ref, wgb_ref)
        _cast_rows(wu_ref, wub_ref)

    a = a_ref[...]
    g = _dot(a, wgb_ref[...])
    u = _dot(a, wub_ref[...])
    o_ref[...] = (g * _sigmoid(g) * u).astype(o_ref.dtype)


def _in_proj_shift_kernel(a_ref, w_ref, wx_ref, o_ref, wb_ref, *, n_plain, shift):
    j = pl.program_id(0)
    first = pl.program_id(1) == 0
    k, tn = w_ref.shape
    width = tn + wx_ref.shape[1]

    @pl.when(first & (j < n_plain))
    def _():
        _cast_rows(w_ref, wb_ref)

    @pl.when(first & (j >= n_plain))
    def _():
        def body(r, c):
            rows = pl.ds(pl.multiple_of(r * _CAST_ROWS, _CAST_ROWS), _CAST_ROWS)
            cat = jnp.concatenate([w_ref[rows, :], wx_ref[rows, :]], axis=1)
            wb_ref[rows, :] = pltpu.roll(cat, width - shift, axis=1)[:, :tn].astype(BF16)
            return c

        lax.fori_loop(0, k // _CAST_ROWS, body, 0)

    o_ref[...] = _dot(a_ref[...], wb_ref[...]).astype(o_ref.dtype)


def _mm_as_kernel(a_ref, w_ref, o_ref):
    o_ref[...] = _dot(a_ref[...], w_ref[...].astype(BF16)).astype(o_ref.dtype)


def _mm_as_swiglu_kernel(a_ref, wg_ref, wu_ref, o_ref):
    a = a_ref[...]
    g = _dot(a, wg_ref[...].astype(BF16))
    u = _dot(a, wu_ref[...].astype(BF16))
    o_ref[...] = (g * _sigmoid(g) * u).astype(o_ref.dtype)


def _in_proj_as_kernel(a_ref, w_ref, wx_ref, o_ref, *, n_plain, shift):
    j = pl.program_id(1)
    tn = w_ref.shape[1]
    width = tn + wx_ref.shape[1]

    @pl.when(j < n_plain)
    def _():
        o_ref[...] = _dot(a_ref[...], w_ref[...].astype(BF16)).astype(o_ref.dtype)

    @pl.when(j >= n_plain)
    def _():
        cat = jnp.concatenate([w_ref[...], wx_ref[...]], axis=1)
        w = pltpu.roll(cat, width - shift, axis=1)[:, :tn].astype(BF16)
        o_ref[...] = _dot(a_ref[...], w).astype(o_ref.dtype)


def _a_spec(tm, k):
    return pl.BlockSpec((tm, k), lambda i, j: (i, 0), pipeline_mode=pl.Buffered(1))


def _wt_spec(layer, k, tn):
    return pl.BlockSpec((None, k, tn), lambda i, j: (layer, 0, j))


def _matmul_as(a, w, layer, *, tm, tn, out_dtype):
    m, k = a.shape
    n = w.shape[2]
    return pl.pallas_call(
        _mm_as_kernel,
        grid=(m // tm, n // tn),
        in_specs=[_a_spec(tm, k), _wt_spec(layer, k, tn)],
        out_specs=pl.BlockSpec((tm, tn), lambda i, j: (i, j)),
        out_shape=jax.ShapeDtypeStruct((m, n), out_dtype),
        compiler_params=_params("parallel", "arbitrary"),
        name="matmul_as",
    )(a, w)


def _matmul_as_swiglu(a, wg, wu, layer, *, tm, tn):
    m, k = a.shape
    n = wg.shape[2]
    return pl.pallas_call(
        _mm_as_swiglu_kernel,
        grid=(m // tm, n // tn),
        in_specs=[_a_spec(tm, k), _wt_spec(layer, k, tn), _wt_spec(layer, k, tn)],
        out_specs=pl.BlockSpec((tm, tn), lambda i, j: (i, j)),
        out_shape=jax.ShapeDtypeStruct((m, n), BF16),
        compiler_params=_params("parallel", "arbitrary"),
        name="matmul_as_swiglu",
    )(a, wg, wu)


def _mm_as_res_kernel(a_ref, w_ref, r_ref, o_ref):
    o_ref[...] = r_ref[...] + _dot(a_ref[...], w_ref[...].astype(BF16))


def _matmul_as_res(a, w, layer, res, *, tm, tn):
    m, k = a.shape
    n = w.shape[2]
    return pl.pallas_call(
        _mm_as_res_kernel,
        grid=(m // tm, n // tn),
        in_specs=[_a_spec(tm, k), _wt_spec(layer, k, tn),
                  pl.BlockSpec((tm, tn), lambda i, j: (i, j))],
        out_specs=pl.BlockSpec((tm, tn), lambda i, j: (i, j)),
        out_shape=jax.ShapeDtypeStruct((m, n), F32),
        compiler_params=_params("parallel", "arbitrary"),
        name="matmul_as_res",
    )(a, w, res)


def _in_proj_as(a, w, layer, *, tm, tn, n_plain, n_total, shift):
    m, k = a.shape
    nx = tn // LANES
    kern = functools.partial(_in_proj_as_kernel, n_plain=n_plain, shift=shift)
    return pl.pallas_call(
        kern,
        grid=(m // tm, n_total),
        in_specs=[_a_spec(tm, k), _wt_spec(layer, k, tn),
                  pl.BlockSpec((None, k, LANES), lambda i, j: (layer, 0, (j + 1) * nx))],
        out_specs=pl.BlockSpec((tm, tn), lambda i, j: (i, j)),
        out_shape=jax.ShapeDtypeStruct((m, n_total * tn), BF16),
        compiler_params=_params("parallel", "arbitrary"),
        name="in_proj_as",
    )(a, w, w)


def _w_spec(layer, k, tn):
    return pl.BlockSpec((None, k, tn), lambda j, i: (layer, 0, j))


def _matmul(a, w, layer, *, tm, tn, out_dtype):
    m, k = a.shape
    n = w.shape[2]
    return pl.pallas_call(
        _mm_kernel,
        grid=(n // tn, m // tm),
        in_specs=[pl.BlockSpec((tm, k), lambda j, i: (i, 0)), _w_spec(layer, k, tn)],
        out_specs=pl.BlockSpec((tm, tn), lambda j, i: (i, j)),
        out_shape=jax.ShapeDtypeStruct((m, n), out_dtype),
        scratch_shapes=[pltpu.VMEM((k, tn), BF16)],
        compiler_params=_params("arbitrary", "arbitrary"),
        name="matmul",
    )(a, w)


def _matmul_res2(a1, a2, w, layer, res, *, tm, tn):
    m, k1 = a1.shape
    k2 = a2.shape[1]
    k, n = w.shape[1:]
    assert k == k1 + k2
    return pl.pallas_call(
        _mm_res2_kernel,
        grid=(n // tn, m // tm),
        in_specs=[pl.BlockSpec((tm, k1), lambda j, i: (i, 0)),
                  pl.BlockSpec((tm, k2), lambda j, i: (i, 0)),
                  _w_spec(layer, k, tn),
                  pl.BlockSpec((tm, tn), lambda j, i: (i, j))],
        out_specs=pl.BlockSpec((tm, tn), lambda j, i: (i, j)),
        out_shape=jax.ShapeDtypeStruct((m, n), F32),
        scratch_shapes=[pltpu.VMEM((k, tn), BF16)],
        compiler_params=_params("arbitrary", "arbitrary"),
        name="matmul_res2",
    )(a1, a2, w, res)


def _matmul_res_bf16w(a, w, layer, res, *, tm, tn):
    m, k = a.shape
    n = w.shape[2]
    return pl.pallas_call(
        _mm_res_bf16w_kernel,
        grid=(n // tn, m // tm),
        in_specs=[pl.BlockSpec((tm, k), lambda j, i: (i, 0)),
                  _w_spec(layer, k, tn),
                  pl.BlockSpec((tm, tn), lambda j, i: (i, j))],
        out_specs=pl.BlockSpec((tm, tn), lambda j, i: (i, j)),
        out_shape=jax.ShapeDtypeStruct((m, n), F32),
        compiler_params=_params("arbitrary", "arbitrary"),
        name="matmul_res_bf16w",
    )(a, w, res)


def _matmul_swiglu(a, wg, wu, layer, *, tm, tn):
    m, k = a.shape
    n = wg.shape[2]
    return pl.pallas_call(
        _mm_swiglu_kernel,
        grid=(n // tn, m // tm),
        in_specs=[pl.BlockSpec((tm, k), lambda j, i: (i, 0)),
                  _w_spec(layer, k, tn), _w_spec(layer, k, tn)],
        out_specs=pl.BlockSpec((tm, tn), lambda j, i: (i, j)),
        out_shape=jax.ShapeDtypeStruct((m, n), BF16),
        scratch_shapes=[pltpu.VMEM((k, tn), BF16), pltpu.VMEM((k, tn), BF16)],
        compiler_params=_params("arbitrary", "arbitrary"),
        name="matmul_swiglu",
    )(a, wg, wu)


def _in_proj_shift(a, w, layer, *, tm, tn, n_plain, n_total, shift):
    m, k = a.shape
    nx = tn // LANES
    kern = functools.partial(_in_proj_shift_kernel, n_plain=n_plain, shift=shift)
    return pl.pallas_call(
        kern,
        grid=(n_total, m // tm),
        in_specs=[pl.BlockSpec((tm, k), lambda j, i: (i, 0)),
                  _w_spec(layer, k, tn),
                  pl.BlockSpec((None, k, LANES), lambda j, i: (layer, 0, (j + 1) * nx))],
        out_specs=pl.BlockSpec((tm, tn), lambda j, i: (i, j)),
        out_shape=jax.ShapeDtypeStruct((m, n_total * tn), BF16),
        scratch_shapes=[pltpu.VMEM((k, tn), BF16)],
        compiler_params=_params("arbitrary", "arbitrary"),
        name="in_proj_shift",
    )(a, w, w)


def _fox_cumsum_kernel(g_ref, b_ref, c_ref, ct_ref, c_sc, *, blk, lane0, heads):
    s = g_ref.shape[0]
    row = lax.broadcasted_iota(jnp.int32, (blk, blk), 0)
    col = lax.broadcasted_iota(jnp.int32, (blk, blk), 1)
    tri = (col <= row).astype(BF16)
    carry = jnp.zeros((1, g_ref.shape[1]), F32)
    for n in range(s // blk):
        rows = slice(n * blk, (n + 1) * blk)
        lf = _log_sigmoid(g_ref[rows, :] + b_ref[...])
        c = _dot_exact_lhs(tri, lf) + carry
        c_sc[rows, :] = c
        carry = c[blk - 1:blk, :]
    c_all = c_sc[...]
    c_ref[...] = c_all
    ct_ref[...] = c_all.T[lane0:lane0 + heads, :]


def _fox_cumsum(gates, bias_row, *, batch, seq, lane0, heads, blk=256):
    t, n = gates.shape
    kern = functools.partial(_fox_cumsum_kernel, blk=blk, lane0=lane0, heads=heads)
    return pl.pallas_call(
        kern,
        grid=(batch,),
        in_specs=[pl.BlockSpec((seq, n), lambda b: (b, 0)),
                  pl.BlockSpec((1, n), lambda b: (0, 0))],
        out_specs=[pl.BlockSpec((seq, n), lambda b: (b, 0)),
                   pl.BlockSpec((None, heads, seq), lambda b: (b, 0, 0))],
        out_shape=[jax.ShapeDtypeStruct((t, n), F32),
                   jax.ShapeDtypeStruct((batch, heads, seq), F32)],
        scratch_shapes=[pltpu.VMEM((seq, n), F32)],
        compiler_params=_params("parallel"),
        name="fox_cumsum",
    )(gates, bias_row)


def _fox_kernel(q_ref, k_ref, v_ref, c_ref, ck_ref, o_ref, *, tq, lane0):
    h = pl.program_id(1)
    qi = pl.program_id(2)
    dh = q_ref.shape[1]
    scale = dh ** -0.5
    q = q_ref[...]
    lane = lax.broadcasted_iota(jnp.int32, c_ref.shape, 1)
    cq = jnp.sum(jnp.where(lane == lane0 + h, c_ref[...], 0.0), axis=1, keepdims=True)

    def logits(kb):
        rows = pl.ds(pl.multiple_of(kb * tq, tq), tq)
        s = _dot_nt(q, k_ref[rows, :]) * scale
        return s + cq - ck_ref[kb], v_ref[rows, :]

    s, v = logits(qi)
    tpos = lax.broadcasted_iota(jnp.int32, (tq, tq), 0)
    spos = lax.broadcasted_iota(jnp.int32, (tq, tq), 1)
    s = jnp.where(spos <= tpos, s, NEG_BIG)
    m0 = jnp.max(s, axis=1, keepdims=True)
    p = jnp.exp(s - m0)
    l0 = jnp.sum(p, axis=1, keepdims=True)
    acc0 = _dot(p.astype(BF16), v)

    def body(kb, carry):
        m, l, acc = carry
        s, v = logits(kb)
        m_new = jnp.maximum(m, jnp.max(s, axis=1, keepdims=True))
        alpha = jnp.exp(m - m_new)
        p = jnp.exp(s - m_new)
        l = alpha * l + jnp.sum(p, axis=1, keepdims=True)
        acc = alpha * acc + _dot(p.astype(BF16), v)
        return m_new, l, acc

    _, l, acc = lax.fori_loop(0, qi, body, (m0, l0, acc0))
    o_ref[...] = (acc / l).astype(o_ref.dtype)


def _fox_attention(proj, c, ck, *, batch, seq, heads, col0, lane0, tq=256):
    t = proj.shape[0]
    dh = FOX_HEAD_DIM
    nq = seq // tq
    qb = col0 // dh
    kern = functools.partial(_fox_kernel, tq=tq, lane0=lane0)
    return pl.pallas_call(
        kern,
        grid=(batch, heads, nq),
        in_specs=[pl.BlockSpec((tq, dh), lambda b, h, i: (b * nq + i, qb + h)),
                  pl.BlockSpec((seq, dh), lambda b, h, i: (b, qb + heads + h)),
                  pl.BlockSpec((seq, dh), lambda b, h, i: (b, qb + 2 * heads + h)),
                  pl.BlockSpec((tq, c.shape[1]), lambda b, h, i: (b * nq + i, 0)),
                  pl.BlockSpec((None, None, nq, 1, tq), lambda b, h, i: (b, h, 0, 0, 0))],
        out_specs=pl.BlockSpec((tq, dh), lambda b, h, i: (b * nq + i, h)),
        out_shape=jax.ShapeDtypeStruct((t, heads * dh), BF16),
        compiler_params=_params("parallel", "parallel", "arbitrary"),
        name="fox_attention",
    )(proj, proj, proj, c, ck)


def _sb_kernel(q_ref, k_ref, v_ref, o_ref, *, tq, tk):
    qi = pl.program_id(2)
    dh = q_ref.shape[1]
    scale = dh ** -0.5
    q = q_ref[...]
    nd = tq // tk
    jj = lax.broadcasted_iota(jnp.int32, (tk, tk), 0)
    ss = lax.broadcasted_iota(jnp.int32, (tk, tk), 1)
    upper = (jj > ss).astype(BF16)
    row = lax.broadcasted_iota(jnp.int32, (tq, tk), 0)
    col = lax.broadcasted_iota(jnp.int32, (tq, tk), 1)

    def block(kb, run, mask):
        rows = pl.ds(pl.multiple_of(kb * tk, tk), tk)
        z = _dot_nt(q, k_ref[rows, :]) * scale
        sp = jnp.log(1.0 + jnp.exp(-jnp.abs(z)))
        log_beta = jnp.minimum(z, 0.0) - sp
        log_1m = log_beta - z
        if mask is not None:
            log_1m = jnp.where(mask, log_1m, 0.0)
        hi, lo = _split_bf16(log_1m, 2)
        suffix = _dot(hi, upper) + _dot(lo, upper)
        a = jnp.exp(log_beta + suffix + run)
        if mask is not None:
            a = jnp.where(mask, a, 0.0)
        pv = _dot(a.astype(BF16), v_ref[rows, :])
        return pv, jnp.sum(log_1m, axis=1, keepdims=True)

    run = jnp.zeros((tq, 1), F32)
    acc = jnp.zeros((tq, dh), F32)
    for d in range(nd):
        off = (nd - 1 - d) * tk
        pv, rs = block((qi + 1) * nd - 1 - d, run, (off + col) < row)
        acc = acc + pv
        run = run + rs

    def body(p, carry):
        run, acc = carry
        for d in range(nd):
            pv, rs = block((qi - p) * nd - 1 - d, run, None)
            acc = acc + pv
            run = run + rs
        return run, acc

    _, acc = lax.fori_loop(0, qi, body, (run, acc))
    o_ref[...] = acc.astype(o_ref.dtype)


def _sb_attention(proj, *, batch, seq, heads, col0, tq=512, tk=256):
    t = proj.shape[0]
    dh = SB_HEAD_DIM
    nq = seq // tq
    qb = col0 // dh
    kern = functools.partial(_sb_kernel, tq=tq, tk=tk)
    return pl.pallas_call(
        kern,
        grid=(batch, heads, nq),
        in_specs=[pl.BlockSpec((tq, dh), lambda b, h, i: (b * nq + i, qb + h)),
                  pl.BlockSpec((seq, dh), lambda b, h, i: (b, qb + heads + h)),
                  pl.BlockSpec((seq, dh), lambda b, h, i: (b, qb + 2 * heads + h))],
        out_specs=pl.BlockSpec((tq, dh), lambda b, h, i: (b * nq + i, h)),
        out_shape=jax.ShapeDtypeStruct((t, heads * dh), BF16),
        compiler_params=_params("parallel", "parallel", "arbitrary"),
        name="sb_attention",
    )(proj, proj, proj)


def _gla_kernel(q_ref, k_ref, v_ref, gg_ref, gate_ref, wup_ref, bg_ref, gn_ref, o_ref,
                state_ref, osc_ref, *, chunk, sub):
    rows_blk, dk = q_ref.shape
    dv = v_ref.shape[1]
    n_chunks = rows_blk // chunk
    n_sub = chunk // sub
    q_scale = dk ** -0.5

    @pl.when(pl.program_id(2) == 0)
    def _():
        state_ref[...] = jnp.zeros_like(state_ref)

    z = _dot_f32(gate_ref[...], wup_ref[...]) + bg_ref[...]
    a = _log_sigmoid(z) / GLA_GATE_TAU
    row = lax.broadcasted_iota(jnp.int32, (rows_blk, rows_blk), 0)
    col = lax.broadcasted_iota(jnp.int32, (rows_blk, rows_blk), 1)
    shift = int(math.log2(chunk))
    same = lax.shift_right_logical(row, shift) == lax.shift_right_logical(col, shift)
    tri = ((col <= row) & same).astype(BF16)
    b = _dot_exact_lhs(tri, a)

    key_row = lax.broadcasted_iota(jnp.int32, (chunk, 1), 0)
    key_lane = lax.broadcasted_iota(jnp.int32, (sub, chunk), 1)
    band_row = lax.broadcasted_iota(jnp.int32, (sub, chunk), 0)
    for c in range(n_chunks):
        r0 = c * chunk
        bc = b[r0:r0 + chunk]
        qc = q_ref[r0:r0 + chunk, :].astype(F32) * q_scale
        kc = k_ref[r0:r0 + chunk, :].astype(F32)
        vc = v_ref[r0:r0 + chunk, :]
        state = state_ref[...]
        inter = _dot((qc * jnp.exp(bc)).astype(BF16), state.astype(BF16))
        bands = []
        for i in range(n_sub):
            lo, hi = i * sub, (i + 1) * sub
            qs, ks, bs = qc[lo:hi], kc[lo:hi], bc[lo:hi]
            if i == 0:
                s = jnp.zeros((sub, chunk), F32)
            else:
                ref = bc[lo - 1:lo]
                qi = (qs * jnp.exp(bs - ref)).astype(BF16)
                kdec = jnp.exp(jnp.minimum(ref - bc, 0.0))
                ki = jnp.where(key_row < lo, kc * kdec, 0.0).astype(BF16)
                s = _dot_nt(qi, ki)
            for s_ in range(sub):
                dec = jnp.exp(jnp.minimum(bs - bs[s_:s_ + 1], 0.0))
                col = jnp.sum(qs * ks[s_:s_ + 1] * dec, axis=1, keepdims=True)
                s = jnp.where((key_lane == lo + s_) & (band_row >= s_), col, s)
            bands.append(s)
        scores = jnp.concatenate(bands, axis=0)
        osc_ref[r0:r0 + chunk, :] = inter + _dot(scores.astype(BF16), vc)

        bl = bc[chunk - 1:chunk]
        kd = kc * jnp.exp(bl - bc)
        stacked = jnp.concatenate([kd, jnp.broadcast_to(jnp.exp(bl), (chunk, dk))], axis=0)
        st = stacked.T
        vpad = jnp.concatenate([vc, jnp.zeros_like(vc)], axis=0)
        state_ref[...] = state * st[:, chunk:chunk + 1] + _dot(st.astype(BF16), vpad)

    o = osc_ref[...]
    ms = jnp.mean(o * o, axis=-1, keepdims=True)
    y = o * lax.rsqrt(ms + EPS) * gn_ref[...]
    g = gg_ref[...].astype(F32)
    o_ref[...] = (y * (g * _sigmoid(g))).astype(o_ref.dtype)


def _gla(proj, gates, wup_pad, b_gate, g_norm, *, batch, seq, heads, dk, dv, rows_blk=256):
    t = proj.shape[0]
    nr = seq // rows_blk
    kb0 = heads
    vb0 = 2 * heads * dk // dv
    gb0 = vb0 + heads
    kern = functools.partial(_gla_kernel, chunk=GLA_CHUNK, sub=GLA_SUB)
    return pl.pallas_call(
        kern,
        grid=(batch, heads, nr),
        in_specs=[pl.BlockSpec((rows_blk, dk), lambda b, h, r: (b * nr + r, h)),
                  pl.BlockSpec((rows_blk, dk), lambda b, h, r: (b * nr + r, kb0 + h)),
                  pl.BlockSpec((rows_blk, dv), lambda b, h, r: (b * nr + r, vb0 + h)),
                  pl.BlockSpec((rows_blk, dv), lambda b, h, r: (b * nr + r, gb0 + h)),
                  pl.BlockSpec((rows_blk, gates.shape[1]), lambda b, h, r: (b * nr + r, 0)),
                  pl.BlockSpec((wup_pad.shape[0], dk), lambda b, h, r: (0, h)),
                  pl.BlockSpec((1, dk), lambda b, h, r: (0, h)),
                  pl.BlockSpec((1, dv), lambda b, h, r: (0, 0))],
        out_specs=pl.BlockSpec((rows_blk, dv), lambda b, h, r: (b * nr + r, h)),
        out_shape=jax.ShapeDtypeStruct((t, heads * dv), BF16),
        scratch_shapes=[pltpu.VMEM((dk, dv), F32),
                        pltpu.VMEM((rows_blk, dv), F32)],
        compiler_params=_params("parallel", "parallel", "arbitrary"),
        name="gla",
    )(proj, proj, proj, proj, gates, wup_pad, b_gate, g_norm)


def _rglru_kernel(rx_ref, ry_ref, cw_ref, cb_ref, wx_ref, bx_ref, wa_ref, ba_ref, lam_ref,
                  o_ref, *, rows_t):
    seq, w = rx_ref.shape
    halo = 8
    wxb = wx_ref[...].astype(BF16)
    wab = wa_ref[...].astype(BF16)
    cw = cw_ref[...]
    neg_lam = -lam_ref[...]
    softplus = jnp.maximum(neg_lam, 0.0) + _softplus_neg_abs(neg_lam)
    coef = -RG_C * softplus
    row = lax.broadcasted_iota(jnp.int32, (rows_t, 1), 0)

    def body(t, carry):
        h_prev, tail = carry
        rows = pl.ds(pl.multiple_of(t * rows_t, rows_t), rows_t)
        x = rx_ref[rows, :].astype(F32)
        xe = jnp.concatenate([tail, x], axis=0)
        xc = cb_ref[...] + cw[RG_CONV - 1:RG_CONV] * x
        for s_ in range(1, RG_CONV):
            xc = xc + cw[RG_CONV - 1 - s_:RG_CONV - s_] * pltpu.roll(xe, s_, axis=0)[halo:]
        xcb = xc.astype(BF16)
        gate_x = _sigmoid(_dot(xcb, wxb) + bx_ref[...])
        gate_a = _sigmoid(_dot(xcb, wab) + ba_ref[...])
        log_a = coef * gate_a
        a_dec = jnp.exp(log_a)
        one_m_a2 = -jnp.tanh(log_a) * (a_dec * a_dec + 1.0)
        u = jnp.sqrt(one_m_a2) * (gate_x * xc)
        d = 1
        while d < rows_t:
            if d < halo:
                a_sh = jnp.where(row >= d, pltpu.roll(a_dec, d, axis=0), 1.0)
                u_sh = jnp.where(row >= d, pltpu.roll(u, d, axis=0), 0.0)
            else:
                a_sh = jnp.concatenate([jnp.ones((d, w), F32), a_dec[:rows_t - d]], axis=0)
                u_sh = jnp.concatenate([jnp.zeros((d, w), F32), u[:rows_t - d]], axis=0)
            u = a_dec * u_sh + u
            a_dec = a_dec * a_sh
            d *= 2
        h = u + a_dec * h_prev
        y = ry_ref[rows, :].astype(F32)
        o_ref[rows, :] = (h * _gelu_tanh(y)).astype(o_ref.dtype)
        return h[rows_t - 1:rows_t], x[rows_t - halo:]

    init = (jnp.zeros((1, w), F32), jnp.zeros((halo, w), F32))
    lax.fori_loop(0, seq // rows_t, body, init)


def _rglru(proj, conv_w, conv_b, w_x, b_x, w_a, b_a, lam, *, batch, seq, width, rows_t=256):
    t = proj.shape[0]
    wb = RG_BLOCK
    nb = width // wb
    kern = functools.partial(_rglru_kernel, rows_t=rows_t)
    vec = lambda v: v.reshape(1, width)
    vspec = pl.BlockSpec((1, wb), lambda b, g: (0, g))
    mspec = pl.BlockSpec((None, wb, wb), lambda b, g: (g, 0, 0))
    return pl.pallas_call(
        kern,
        grid=(batch, nb),
        in_specs=[pl.BlockSpec((seq, wb), lambda b, g: (b, nb + g)),
                  pl.BlockSpec((seq, wb), lambda b, g: (b, g)),
                  pl.BlockSpec((RG_CONV, wb), lambda b, g: (0, g)),
                  vspec, mspec, vspec, mspec, vspec, vspec],
        out_specs=pl.BlockSpec((seq, wb), lambda b, g: (b, g)),
        out_shape=jax.ShapeDtypeStruct((t, width), BF16),
        compiler_params=_params("parallel", "parallel"),
        name="rglru",
    )(proj, proj, conv_w, vec(conv_b), w_x, vec(b_x), w_a, vec(b_a), vec(lam))


_TM = 1024
_TN = 512


_TM_AS = 2048
_FOX_TQ = 512


def _ffn(h, g, w_gate, w_up, w_down, layer):
    u = _rmsnorm(h, g, BF16)
    hid = _matmul_as_swiglu(u, w_gate, w_up, layer, tm=_TM_AS, tn=256)
    return _matmul_as_res(hid, w_down, layer, h, tm=1024, tn=256)


def _even_mixer(h, g, w_in, i, w_gate_up, b_gate, g_norm, b_f, w_out, *, batch, seq):
    d = h.shape[1]
    dk, dv = d // 16, d // 8
    qk, vw, fw = GLA_HEADS * dk, GLA_HEADS * dv, d // 2
    fox_heads = fw // FOX_HEAD_DIM
    rank = GLA_GATE_RANK
    main0 = 2 * qk + 2 * vw
    main1 = 3 * fw
    u, gates = _gate_proj(h, g, w_in, i, col_a=main0, n_a=rank,
                          col_b=main0 + rank + main1, n_b=fox_heads)
    proj = _in_proj_as(u, w_in, i, tm=_TM_AS, tn=_TN, n_plain=main0 // _TN,
                       n_total=(main0 + main1) // _TN, shift=rank)
    bias_row = jnp.zeros((1, LANES), F32).at[0, rank:rank + fox_heads].set(b_f)
    c, ct = _fox_cumsum(gates, bias_row, batch=batch, seq=seq, lane0=rank, heads=fox_heads)
    ck = ct.reshape(batch, fox_heads, seq // _FOX_TQ, 1, _FOX_TQ)
    o_fox = _fox_attention(proj, c, ck, batch=batch, seq=seq, heads=fox_heads, col0=main0,
                           lane0=rank, tq=_FOX_TQ)
    wup_pad = jnp.pad(w_gate_up, ((0, LANES - rank), (0, 0)))
    o_gla = _gla(proj, gates, wup_pad, b_gate.reshape(1, qk), g_norm.reshape(1, dv),
                 batch=batch, seq=seq, heads=GLA_HEADS, dk=dk, dv=dv)
    return _matmul_res2(o_gla, o_fox, w_out, i, h, tm=_TM, tn=_TN)


def _odd_mixer(h, g, w_in, i, conv_w, conv_b, w_x, b_x, w_a, b_a, lam, w_out, *, batch, seq):
    d = h.shape[1]
    width = d // 2
    sb_heads = width // SB_HEAD_DIM
    u = _rmsnorm(h, g, BF16)
    proj = _matmul_as(u, w_in, i, tm=_TM_AS, tn=_TN, out_dtype=BF16)
    o_rg = _rglru(proj, conv_w, conv_b, w_x, b_x, w_a, b_a, lam, batch=batch, seq=seq,
                  width=width)
    o_sb = _sb_attention(proj, batch=batch, seq=seq, heads=sb_heads, col0=2 * width)
    return _matmul_res2(o_rg, o_sb, w_out, i, h, tm=_TM, tn=_TN)


def kernel(x, norm_mix, norm_ffn, ffn_w_gate, ffn_w_up, ffn_w_down, ab_w_in, gla_w_gate_up,
           gla_b_gate, gla_norm, fox_b_f, ab_w_out, cd_w_in, rg_conv_w, rg_conv_b, rg_w_x,
           rg_b_x, rg_w_a, rg_b_a, rg_lambda, cd_w_out, final_norm):
    batch, seq, d = x.shape
    h = x.reshape(batch * seq, d)
    for layer in range(norm_mix.shape[0]):
        i = layer // 2
        if layer % 2 == 0:
            h = _even_mixer(h, norm_mix[layer], ab_w_in, i, gla_w_gate_up[i], gla_b_gate[i],
                            gla_norm[i], fox_b_f[i], ab_w_out, batch=batch, seq=seq)
        else:
            h = _odd_mixer(h, norm_mix[layer], cd_w_in, i, rg_conv_w[i], rg_conv_b[i],
                           rg_w_x[i], rg_b_x[i], rg_w_a[i], rg_b_a[i], rg_lambda[i],
                           cd_w_out, batch=batch, seq=seq)
        h = _ffn(h, norm_ffn[layer], ffn_w_gate, ffn_w_up, ffn_w_down, layer)
    return _rmsnorm(h, final_norm, F32).reshape(batch, seq, d)
```

```python
import functools
import math

import jax
import jax.numpy as jnp
from jax import lax
from jax.experimental import pallas as pl
from jax.experimental.pallas import tpu as pltpu

F32 = jnp.float32
BF16 = jnp.bfloat16

EPS = 1e-6
LANES = 128
VMEM_LIMIT = 56 * 1024 * 1024

GLA_HEADS = 4
GLA_GATE_RANK = 16
GLA_GATE_TAU = 16.0
GLA_CHUNK = 64
GLA_SUB = 16
FOX_HEAD_DIM = 128
SB_HEAD_DIM = 128
RG_BLOCK = 256
RG_CONV = 4
RG_C = 8.0

NEG_BIG = -1e30
_NT = (((1,), (1,)), ((), ()))


def _params(*sem):
    return pltpu.CompilerParams(dimension_semantics=sem, vmem_limit_bytes=VMEM_LIMIT)


def _dot(a, b):
    return jnp.dot(a, b, preferred_element_type=F32)


def _dot_nt(a, b):
    return lax.dot_general(a, b, _NT, preferred_element_type=F32)


def _split_bf16(x, n):
    parts = []
    r = x
    for _ in range(n):
        p = r.astype(BF16)
        parts.append(p)
        r = r - p.astype(F32)
    return parts


def _dot_f32(a, b):
    ah, al = _split_bf16(a, 2)
    bh, bl = _split_bf16(b, 2)
    return _dot(ah, bh) + _dot(ah, bl) + _dot(al, bh)


def _dot_exact_lhs(m01, x, n=3):
    acc = None
    for p in _split_bf16(x, n):
        t = _dot(m01, p)
        acc = t if acc is None else acc + t
    return acc


def _softplus_neg_abs(z):
    return jnp.log1p(jnp.exp(-jnp.abs(z)))


def _log_sigmoid(z):
    return jnp.minimum(z, 0.0) - _softplus_neg_abs(z)


def _sigmoid(z):
    return 1.0 / (1.0 + jnp.exp(-z))


def _gelu_tanh(y):
    c = math.sqrt(2.0 / math.pi)
    return y * (0.5 * (1.0 + jnp.tanh(c * (y + 0.044715 * (y * y * y)))))


def _rmsnorm_kernel(x_ref, g_ref, o_ref):
    x = x_ref[...]
    ms = jnp.mean(x * x, axis=-1, keepdims=True)
    o_ref[...] = (x * lax.rsqrt(ms + EPS) * g_ref[...]).astype(o_ref.dtype)


def _rmsnorm(x, g, out_dtype, tm=512):
    t, d = x.shape
    return pl.pallas_call(
        _rmsnorm_kernel,
        grid=(t // tm,),
        in_specs=[pl.BlockSpec((tm, d), lambda i: (i, 0)),
                  pl.BlockSpec((1, d), lambda i: (0, 0))],
        out_specs=pl.BlockSpec((tm, d), lambda i: (i, 0)),
        out_shape=jax.ShapeDtypeStruct((t, d), out_dtype),
        compiler_params=_params("parallel"),
        name="rmsnorm",
    )(x, g.reshape(1, d))


def _gate_proj_kernel(x_ref, g_ref, wa_ref, wb_ref, u_ref, o_ref, *, n_a, n_b):
    x = x_ref[...]
    ms = jnp.mean(x * x, axis=-1, keepdims=True)
    u = x * lax.rsqrt(ms + EPS) * g_ref[...]
    u_ref[...] = u.astype(u_ref.dtype)
    lane = lax.broadcasted_iota(jnp.int32, o_ref.shape, 1)
    o = _dot_f32(u, jnp.concatenate([wa_ref[...], wb_ref[...]], axis=1))
    oa, ob = o[:, :LANES], o[:, LANES:]
    o_ref[...] = jnp.where(lane < n_a, oa, jnp.where(lane < n_a + n_b, ob, 0.0))


def _gate_proj_t_kernel(x_ref, g_ref, wa_ref, wb_ref, u_ref, o_ref):
    x = x_ref[...]
    ms = jnp.mean(x * x, axis=-1, keepdims=True)
    u = x * lax.rsqrt(ms + EPS) * g_ref[...]
    u_ref[...] = u.astype(u_ref.dtype)
    used = wa_ref.shape[0] + wb_ref.shape[0]
    wt = jnp.concatenate([wa_ref[...], wb_ref[...],
                          jnp.zeros((o_ref.shape[1] - used, x.shape[1]), F32)], axis=0)
    uh, ul = _split_bf16(u, 2)
    wh, wl = _split_bf16(wt, 2)
    o_ref[...] = _dot_nt(uh, wh) + _dot_nt(uh, wl) + _dot_nt(ul, wh)


def _gate_proj_t(x, g, wt, layer, *, col_a, n_a, col_b, n_b, tm=256):
    t, d = x.shape
    assert col_a % n_a == 0 and col_b % n_b == 0
    return pl.pallas_call(
        _gate_proj_t_kernel,
        grid=(t // tm,),
        in_specs=[pl.BlockSpec((tm, d), lambda i: (i, 0)),
                  pl.BlockSpec((1, d), lambda i: (0, 0)),
                  pl.BlockSpec((None, n_a, d), lambda i: (layer, col_a // n_a, 0)),
                  pl.BlockSpec((None, n_b, d), lambda i: (layer, col_b // n_b, 0))],
        out_specs=[pl.BlockSpec((tm, d), lambda i: (i, 0)),
                   pl.BlockSpec((tm, LANES), lambda i: (i, 0))],
        out_shape=[jax.ShapeDtypeStruct((t, d), BF16),
                   jax.ShapeDtypeStruct((t, LANES), F32)],
        compiler_params=_params("parallel"),
        name="gate_proj_t",
    )(x, g.reshape(1, d), wt, wt)


def _gate_proj(x, g, w, layer, *, col_a, n_a, col_b, n_b, tm=256):
    t, d = x.shape
    assert col_a % LANES == 0 and col_b % LANES == n_a
    kern = functools.partial(_gate_proj_kernel, n_a=n_a, n_b=n_b)
    return pl.pallas_call(
        kern,
        grid=(t // tm,),
        in_specs=[pl.BlockSpec((tm, d), lambda i: (i, 0)),
                  pl.BlockSpec((1, d), lambda i: (0, 0)),
                  pl.BlockSpec((None, d, LANES), lambda i: (layer, 0, col_a // LANES)),
                  pl.BlockSpec((None, d, LANES), lambda i: (layer, 0, col_b // LANES))],
        out_specs=[pl.BlockSpec((tm, d), lambda i: (i, 0)),
                   pl.BlockSpec((tm, LANES), lambda i: (i, 0))],
        out_shape=[jax.ShapeDtypeStruct((t, d), BF16),
                   jax.ShapeDtypeStruct((t, LANES), F32)],
        compiler_params=_params("parallel"),
        name="gate_proj",
    )(x, g.reshape(1, d), w, w)


_CAST_ROWS = 512


def _cast_rows(w_ref, wb_ref):
    k = w_ref.shape[0]

    def body(r, c):
        rows = pl.ds(pl.multiple_of(r * _CAST_ROWS, _CAST_ROWS), _CAST_ROWS)
        wb_ref[rows, :] = w_ref[rows, :].astype(BF16)
        return c

    lax.fori_loop(0, k // _CAST_ROWS, body, 0)


def _mm_kernel(a_ref, w_ref, o_ref, wb_ref):
    @pl.when(pl.program_id(1) == 0)
    def _():
        _cast_rows(w_ref, wb_ref)

    o_ref[...] = _dot(a_ref[...], wb_ref[...]).astype(o_ref.dtype)


def _mm_res2_kernel(a1_ref, a2_ref, w_ref, r_ref, o_ref, wb_ref):
    @pl.when(pl.program_id(1) == 0)
    def _():
        _cast_rows(w_ref, wb_ref)

    k1 = a1_ref.shape[1]
    o_ref[...] = (r_ref[...] + _dot(a1_ref[...], wb_ref[:k1, :])
                  + _dot(a2_ref[...], wb_ref[k1:, :]))


def _mm_res_bf16w_kernel(a_ref, w_ref, r_ref, o_ref):
    o_ref[...] = r_ref[...] + _dot(a_ref[...], w_ref[...])


def _mm_swiglu_kernel(a_ref, wg_ref, wu_ref, o_ref, wgb_ref, wub_ref):
    @pl.when(pl.program_id(1) == 0)
    def _():
        _cast_rows(wg_ref, wgb_ref)
        _cast_rows(wu_ref, wub_ref)

    a = a_ref[...]
    g = _dot(a, wgb_ref[...])
    u = _dot(a, wub_ref[...])
    o_ref[...] = (g * _sigmoid(g) * u).astype(o_ref.dtype)


def _in_proj_shift_kernel(a_ref, w_ref, wx_ref, o_ref, wb_ref, *, n_plain, shift):
    j = pl.program_id(0)
    first = pl.program_id(1) == 0
    k, tn = w_ref.shape
    width = tn + wx_ref.shape[1]

    @pl.when(first & (j < n_plain))
    def _():
        _cast_rows(w_ref, wb_ref)

    @pl.when(first & (j >= n_plain))
    def _():
        def body(r, c):
            rows = pl.ds(pl.multiple_of(r * _CAST_ROWS, _CAST_ROWS), _CAST_ROWS)
            cat = jnp.concatenate([w_ref[rows, :], wx_ref[rows, :]], axis=1)
            wb_ref[rows, :] = pltpu.roll(cat, width - shift, axis=1)[:, :tn].astype(BF16)
            return c

        lax.fori_loop(0, k // _CAST_ROWS, body, 0)

    o_ref[...] = _dot(a_ref[...], wb_ref[...]).astype(o_ref.dtype)


def _mm_as_kernel(a_ref, w_ref, o_ref):
    o_ref[...] = _dot(a_ref[...], w_ref[...].astype(BF16)).astype(o_ref.dtype)


def _mm_as_swiglu_kernel(a_ref, wg_ref, wu_ref, o_ref):
    a = a_ref[...]
    g = _dot(a, wg_ref[...].astype(BF16))
    u = _dot(a, wu_ref[...].astype(BF16))
    o_ref[...] = (g * _sigmoid(g) * u).astype(o_ref.dtype)


def _in_proj_as_kernel(a_ref, w_ref, wx_ref, o_ref, *, n_plain, shift):
    j = pl.program_id(1)
    tn = w_ref.shape[1]
    width = tn + wx_ref.shape[1]

    @pl.when(j < n_plain)
    def _():
        o_ref[...] = _dot(a_ref[...], w_ref[...].astype(BF16)).astype(o_ref.dtype)

    @pl.when(j >= n_plain)
    def _():
        cat = jnp.concatenate([w_ref[...], wx_ref[...]], axis=1)
        w = pltpu.roll(cat, width - shift, axis=1)[:, :tn].astype(BF16)
        o_ref[...] = _dot(a_ref[...], w).astype(o_ref.dtype)


def _a_spec(tm, k):
    return pl.BlockSpec((tm, k), lambda i, j: (i, 0), pipeline_mode=pl.Buffered(1))


def _wt_spec(layer, k, tn):
    return pl.BlockSpec((None, k, tn), lambda i, j: (layer, 0, j))


def _matmul_as(a, w, layer, *, tm, tn, out_dtype):
    m, k = a.shape
    n = w.shape[2]
    return pl.pallas_call(
        _mm_as_kernel,
        grid=(m // tm, n // tn),
        in_specs=[_a_spec(tm, k), _wt_spec(layer, k, tn)],
        out_specs=pl.BlockSpec((tm, tn), lambda i, j: (i, j)),
        out_shape=jax.ShapeDtypeStruct((m, n), out_dtype),
        compiler_params=_params("parallel", "arbitrary"),
        name="matmul_as",
    )(a, w)


def _matmul_as_swiglu(a, wg, wu, layer, *, tm, tn):
    m, k = a.shape
    n = wg.shape[2]
    return pl.pallas_call(
        _mm_as_swiglu_kernel,
        grid=(m // tm, n // tn),
        in_specs=[_a_spec(tm, k), _wt_spec(layer, k, tn), _wt_spec(layer, k, tn)],
        out_specs=pl.BlockSpec((tm, tn), lambda i, j: (i, j)),
        out_shape=jax.ShapeDtypeStruct((m, n), BF16),
        compiler_params=_params("parallel", "arbitrary"),
        name="matmul_as_swiglu",
    )(a, wg, wu)


def _mm_as_res_kernel(a_ref, w_ref, r_ref, o_ref):
    o_ref[...] = r_ref[...] + _dot(a_ref[...], w_ref[...].astype(BF16))


def _matmul_as_res(a, w, layer, res, *, tm, tn):
    m, k = a.shape
    n = w.shape[2]
    return pl.pallas_call(
        _mm_as_res_kernel,
        grid=(m // tm, n // tn),
        in_specs=[_a_spec(tm, k), _wt_spec(layer, k, tn),
                  pl.BlockSpec((tm, tn), lambda i, j: (i, j))],
        out_specs=pl.BlockSpec((tm, tn), lambda i, j: (i, j)),
        out_shape=jax.ShapeDtypeStruct((m, n), F32),
        compiler_params=_params("parallel", "arbitrary"),
        name="matmul_as_res",
    )(a, w, res)


def _in_proj_as_t_kernel(a_ref, wt_ref, wtx_ref, o_ref, *, n_plain):
    j = pl.program_id(1)
    shift = wtx_ref.shape[0]

    @pl.when(j < n_plain)
    def _():
        o_ref[...] = _dot_nt(a_ref[...], wt_ref[...].astype(BF16)).astype(o_ref.dtype)

    @pl.when(j >= n_plain)
    def _():
        wt = jnp.concatenate([wt_ref[shift:, :], wtx_ref[...]], axis=0).astype(BF16)
        o_ref[...] = _dot_nt(a_ref[...], wt).astype(o_ref.dtype)


def _in_proj_as_t(a, wt, layer, *, tm, tn, n_plain, n_total, shift):
    m, k = a.shape
    assert tn % shift == 0
    kern = functools.partial(_in_proj_as_t_kernel, n_plain=n_plain)
    return pl.pallas_call(
        kern,
        grid=(m // tm, n_total),
        in_specs=[_a_spec(tm, k),
                  pl.BlockSpec((None, tn, k), lambda i, j: (layer, j, 0)),
                  pl.BlockSpec((None, shift, k), lambda i, j: (layer, (j + 1) * (tn // shift), 0))],
        out_specs=pl.BlockSpec((tm, tn), lambda i, j: (i, j)),
        out_shape=jax.ShapeDtypeStruct((m, n_total * tn), BF16),
        compiler_params=_params("parallel", "arbitrary"),
        name="in_proj_as_t",
    )(a, wt, wt)


def _in_proj_as(a, w, layer, *, tm, tn, n_plain, n_total, shift):
    m, k = a.shape
    nx = tn // LANES
    kern = functools.partial(_in_proj_as_kernel, n_plain=n_plain, shift=shift)
    return pl.pallas_call(
        kern,
        grid=(m // tm, n_total),
        in_specs=[_a_spec(tm, k), _wt_spec(layer, k, tn),
                  pl.BlockSpec((None, k, LANES), lambda i, j: (layer, 0, (j + 1) * nx))],
        out_specs=pl.BlockSpec((tm, tn), lambda i, j: (i, j)),
        out_shape=jax.ShapeDtypeStruct((m, n_total * tn), BF16),
        compiler_params=_params("parallel", "arbitrary"),
        name="in_proj_as",
    )(a, w, w)


def _w_spec(layer, k, tn):
    return pl.BlockSpec((None, k, tn), lambda j, i: (layer, 0, j))


def _matmul(a, w, layer, *, tm, tn, out_dtype):
    m, k = a.shape
    n = w.shape[2]
    return pl.pallas_call(
        _mm_kernel,
        grid=(n // tn, m // tm),
        in_specs=[pl.BlockSpec((tm, k), lambda j, i: (i, 0)), _w_spec(layer, k, tn)],
        out_specs=pl.BlockSpec((tm, tn), lambda j, i: (i, j)),
        out_shape=jax.ShapeDtypeStruct((m, n), out_dtype),
        scratch_shapes=[pltpu.VMEM((k, tn), BF16)],
        compiler_params=_params("arbitrary", "arbitrary"),
        name="matmul",
    )(a, w)


def _matmul_res2(a1, a2, w, layer, res, *, tm, tn):
    m, k1 = a1.shape
    k2 = a2.shape[1]
    k, n = w.shape[1:]
    assert k == k1 + k2
    return pl.pallas_call(
        _mm_res2_kernel,
        grid=(n // tn, m // tm),
        in_specs=[pl.BlockSpec((tm, k1), lambda j, i: (i, 0)),
                  pl.BlockSpec((tm, k2), lambda j, i: (i, 0)),
                  _w_spec(layer, k, tn),
                  pl.BlockSpec((tm, tn), lambda j, i: (i, j))],
        out_specs=pl.BlockSpec((tm, tn), lambda j, i: (i, j)),
        out_shape=jax.ShapeDtypeStruct((m, n), F32),
        scratch_shapes=[pltpu.VMEM((k, tn), BF16)],
        compiler_params=_params("arbitrary", "arbitrary"),
        name="matmul_res2",
    )(a1, a2, w, res)


def _matmul_res_bf16w(a, w, layer, res, *, tm, tn):
    m, k = a.shape
    n = w.shape[2]
    return pl.pallas_call(
        _mm_res_bf16w_kernel,
        grid=(n // tn, m // tm),
        in_specs=[pl.BlockSpec((tm, k), lambda j, i: (i, 0)),
                  _w_spec(layer, k, tn),
                  pl.BlockSpec((tm, tn), lambda j, i: (i, j))],
        out_specs=pl.BlockSpec((tm, tn), lambda j, i: (i, j)),
        out_shape=jax.ShapeDtypeStruct((m, n), F32),
        compiler_params=_params("arbitrary", "arbitrary"),
        name="matmul_res_bf16w",
    )(a, w, res)


def _matmul_swiglu(a, wg, wu, layer, *, tm, tn):
    m, k = a.shape
    n = wg.shape[2]
    return pl.pallas_call(
        _mm_swiglu_kernel,
        grid=(n // tn, m // tm),
        in_specs=[pl.BlockSpec((tm, k), lambda j, i: (i, 0)),
                  _w_spec(layer, k, tn), _w_spec(layer, k, tn)],
        out_specs=pl.BlockSpec((tm, tn), lambda j, i: (i, j)),
        out_shape=jax.ShapeDtypeStruct((m, n), BF16),
        scratch_shapes=[pltpu.VMEM((k, tn), BF16), pltpu.VMEM((k, tn), BF16)],
        compiler_params=_params("arbitrary", "arbitrary"),
        name="matmul_swiglu",
    )(a, wg, wu)


def _in_proj_shift(a, w, layer, *, tm, tn, n_plain, n_total, shift):
    m, k = a.shape
    nx = tn // LANES
    kern = functools.partial(_in_proj_shift_kernel, n_plain=n_plain, shift=shift)
    return pl.pallas_call(
        kern,
        grid=(n_total, m // tm),
        in_specs=[pl.BlockSpec((tm, k), lambda j, i: (i, 0)),
                  _w_spec(layer, k, tn),
                  pl.BlockSpec((None, k, LANES), lambda j, i: (layer, 0, (j + 1) * nx))],
        out_specs=pl.BlockSpec((tm, tn), lambda j, i: (i, j)),
        out_shape=jax.ShapeDtypeStruct((m, n_total * tn), BF16),
        scratch_shapes=[pltpu.VMEM((k, tn), BF16)],
        compiler_params=_params("arbitrary", "arbitrary"),
        name="in_proj_shift",
    )(a, w, w)


def _fox_cumsum_kernel(g_ref, b_ref, c_ref, ct_ref, c_sc, *, blk, lane0, heads):
    s = g_ref.shape[0]
    row = lax.broadcasted_iota(jnp.int32, (blk, blk), 0)
    col = lax.broadcasted_iota(jnp.int32, (blk, blk), 1)
    tri = (col <= row).astype(BF16)
    carry = jnp.zeros((1, g_ref.shape[1]), F32)
    for n in range(s // blk):
        rows = slice(n * blk, (n + 1) * blk)
        lf = _log_sigmoid(g_ref[rows, :] + b_ref[...])
        c = _dot_exact_lhs(tri, lf) + carry
        c_sc[rows, :] = c
        carry = c[blk - 1:blk, :]
    c_all = c_sc[...]
    c_ref[...] = c_all
    ct_ref[...] = c_all.T[lane0:lane0 + heads, :]


def _fox_cumsum(gates, bias_row, *, batch, seq, lane0, heads, blk=256):
    t, n = gates.shape
    kern = functools.partial(_fox_cumsum_kernel, blk=blk, lane0=lane0, heads=heads)
    return pl.pallas_call(
        kern,
        grid=(batch,),
        in_specs=[pl.BlockSpec((seq, n), lambda b: (b, 0)),
                  pl.BlockSpec((1, n), lambda b: (0, 0))],
        out_specs=[pl.BlockSpec((seq, n), lambda b: (b, 0)),
                   pl.BlockSpec((None, heads, seq), lambda b: (b, 0, 0))],
        out_shape=[jax.ShapeDtypeStruct((t, n), F32),
                   jax.ShapeDtypeStruct((batch, heads, seq), F32)],
        scratch_shapes=[pltpu.VMEM((seq, n), F32)],
        compiler_params=_params("parallel"),
        name="fox_cumsum",
    )(gates, bias_row)


def _fox_kernel(q_ref, k_ref, v_ref, c_ref, ck_ref, o_ref, *, tq, lane0):
    h = pl.program_id(1)
    qi = pl.program_id(2)
    dh = q_ref.shape[1]
    scale = dh ** -0.5
    q = q_ref[...]
    lane = lax.broadcasted_iota(jnp.int32, c_ref.shape, 1)
    cq = jnp.sum(jnp.where(lane == lane0 + h, c_ref[...], 0.0), axis=1, keepdims=True)

    def logits(kb):
        rows = pl.ds(pl.multiple_of(kb * tq, tq), tq)
        s = _dot_nt(q, k_ref[rows, :]) * scale
        return s + cq - ck_ref[kb], v_ref[rows, :]

    s, v = logits(qi)
    tpos = lax.broadcasted_iota(jnp.int32, (tq, tq), 0)
    spos = lax.broadcasted_iota(jnp.int32, (tq, tq), 1)
    s = jnp.where(spos <= tpos, s, NEG_BIG)
    m0 = jnp.max(s, axis=1, keepdims=True)
    p = jnp.exp(s - m0)
    l0 = jnp.sum(p, axis=1, keepdims=True)
    acc0 = _dot(p.astype(BF16), v)

    def body(kb, carry):
        m, l, acc = carry
        s, v = logits(kb)
        m_new = jnp.maximum(m, jnp.max(s, axis=1, keepdims=True))
        alpha = jnp.exp(m - m_new)
        p = jnp.exp(s - m_new)
        l = alpha * l + jnp.sum(p, axis=1, keepdims=True)
        acc = alpha * acc + _dot(p.astype(BF16), v)
        return m_new, l, acc

    _, l, acc = lax.fori_loop(0, qi, body, (m0, l0, acc0))
    o_ref[...] = (acc / l).astype(o_ref.dtype)


def _fox_attention(proj, c, ck, *, batch, seq, heads, col0, lane0, tq=256):
    t = proj.shape[0]
    dh = FOX_HEAD_DIM
    nq = seq // tq
    qb = col0 // dh
    kern = functools.partial(_fox_kernel, tq=tq, lane0=lane0)
    return pl.pallas_call(
        kern,
        grid=(batch, heads, nq),
        in_specs=[pl.BlockSpec((tq, dh), lambda b, h, i: (b * nq + i, qb + h)),
                  pl.BlockSpec((seq, dh), lambda b, h, i: (b, qb + heads + h)),
                  pl.BlockSpec((seq, dh), lambda b, h, i: (b, qb + 2 * heads + h)),
                  pl.BlockSpec((tq, c.shape[1]), lambda b, h, i: (b * nq + i, 0)),
                  pl.BlockSpec((None, None, nq, 1, tq), lambda b, h, i: (b, h, 0, 0, 0))],
        out_specs=pl.BlockSpec((tq, dh), lambda b, h, i: (b * nq + i, h)),
        out_shape=jax.ShapeDtypeStruct((t, heads * dh), BF16),
        compiler_params=_params("parallel", "parallel", "arbitrary"),
        name="fox_attention",
    )(proj, proj, proj, c, ck)


def _sb_kernel(q_ref, k_ref, v_ref, o_ref, *, tq, tk):
    qi = pl.program_id(2)
    dh = q_ref.shape[1]
    scale = dh ** -0.5
    q = q_ref[...]
    nd = tq // tk
    jj = lax.broadcasted_iota(jnp.int32, (tk, tk), 0)
    ss = lax.broadcasted_iota(jnp.int32, (tk, tk), 1)
    upper = (jj > ss).astype(BF16)
    row = lax.broadcasted_iota(jnp.int32, (tq, tk), 0)
    col = lax.broadcasted_iota(jnp.int32, (tq, tk), 1)

    def block(kb, run, mask):
        rows = pl.ds(pl.multiple_of(kb * tk, tk), tk)
        z = _dot_nt(q, k_ref[rows, :]) * scale
        sp = jnp.log(1.0 + jnp.exp(-jnp.abs(z)))
        log_beta = jnp.minimum(z, 0.0) - sp
        log_1m = log_beta - z
        if mask is not None:
            log_1m = jnp.where(mask, log_1m, 0.0)
        hi, lo = _split_bf16(log_1m, 2)
        suffix = _dot(hi, upper) + _dot(lo, upper)
        a = jnp.exp(log_beta + suffix + run)
        if mask is not None:
            a = jnp.where(mask, a, 0.0)
        pv = _dot(a.astype(BF16), v_ref[rows, :])
        return pv, jnp.sum(log_1m, axis=1, keepdims=True)

    run = jnp.zeros((tq, 1), F32)
    acc = jnp.zeros((tq, dh), F32)
    for d in range(nd):
        off = (nd - 1 - d) * tk
        pv, rs = block((qi + 1) * nd - 1 - d, run, (off + col) < row)
        acc = acc + pv
        run = run + rs

    def body(p, carry):
        run, acc = carry
        for d in range(nd):
            pv, rs = block((qi - p) * nd - 1 - d, run, None)
            acc = acc + pv
            run = run + rs
        return run, acc

    _, acc = lax.fori_loop(0, qi, body, (run, acc))
    o_ref[...] = acc.astype(o_ref.dtype)


def _sb_attention(proj, *, batch, seq, heads, col0, tq=512, tk=256):
    t = proj.shape[0]
    dh = SB_HEAD_DIM
    nq = seq // tq
    qb = col0 // dh
    kern = functools.partial(_sb_kernel, tq=tq, tk=tk)
    return pl.pallas_call(
        kern,
        grid=(batch, heads, nq),
        in_specs=[pl.BlockSpec((tq, dh), lambda b, h, i: (b * nq + i, qb + h)),
                  pl.BlockSpec((seq, dh), lambda b, h, i: (b, qb + heads + h)),
                  pl.BlockSpec((seq, dh), lambda b, h, i: (b, qb + 2 * heads + h))],
        out_specs=pl.BlockSpec((tq, dh), lambda b, h, i: (b * nq + i, h)),
        out_shape=jax.ShapeDtypeStruct((t, heads * dh), BF16),
        compiler_params=_params("parallel", "parallel", "arbitrary"),
        name="sb_attention",
    )(proj, proj, proj)


def _gla_kernel(q_ref, k_ref, v_ref, gg_ref, gate_ref, wup_ref, bg_ref, gn_ref, o_ref,
                state_ref, osc_ref, *, chunk, sub):
    rows_blk, dk = q_ref.shape
    dv = v_ref.shape[1]
    n_chunks = rows_blk // chunk
    n_sub = chunk // sub
    q_scale = dk ** -0.5

    @pl.when(pl.program_id(2) == 0)
    def _():
        state_ref[...] = jnp.zeros_like(state_ref)

    z = _dot_f32(gate_ref[...], wup_ref[...]) + bg_ref[...]
    a = _log_sigmoid(z) / GLA_GATE_TAU
    row = lax.broadcasted_iota(jnp.int32, (rows_blk, rows_blk), 0)
    col = lax.broadcasted_iota(jnp.int32, (rows_blk, rows_blk), 1)
    shift = int(math.log2(chunk))
    same = lax.shift_right_logical(row, shift) == lax.shift_right_logical(col, shift)
    tri = ((col <= row) & same).astype(BF16)
    b = _dot_exact_lhs(tri, a)

    key_row = lax.broadcasted_iota(jnp.int32, (chunk, 1), 0)
    key_lane = lax.broadcasted_iota(jnp.int32, (sub, chunk), 1)
    band_row = lax.broadcasted_iota(jnp.int32, (sub, chunk), 0)
    for c in range(n_chunks):
        r0 = c * chunk
        bc = b[r0:r0 + chunk]
        qc = q_ref[r0:r0 + chunk, :].astype(F32) * q_scale
        kc = k_ref[r0:r0 + chunk, :].astype(F32)
        vc = v_ref[r0:r0 + chunk, :]
        state = state_ref[...]
        inter = _dot((qc * jnp.exp(bc)).astype(BF16), state.astype(BF16))
        bands = []
        for i in range(n_sub):
            lo, hi = i * sub, (i + 1) * sub
            qs, ks, bs = qc[lo:hi], kc[lo:hi], bc[lo:hi]
            if i == 0:
                s = jnp.zeros((sub, chunk), F32)
            else:
                ref = bc[lo - 1:lo]
                qi = (qs * jnp.exp(bs - ref)).astype(BF16)
                kdec = jnp.exp(jnp.minimum(ref - bc, 0.0))
                ki = jnp.where(key_row < lo, kc * kdec, 0.0).astype(BF16)
                s = _dot_nt(qi, ki)
            for s_ in range(sub):
                dec = jnp.exp(jnp.minimum(bs - bs[s_:s_ + 1], 0.0))
                col = jnp.sum(qs * ks[s_:s_ + 1] * dec, axis=1, keepdims=True)
                s = jnp.where((key_lane == lo + s_) & (band_row >= s_), col, s)
            bands.append(s)
        scores = jnp.concatenate(bands, axis=0)
        osc_ref[r0:r0 + chunk, :] = inter + _dot(scores.astype(BF16), vc)

        bl = bc[chunk - 1:chunk]
        kd = kc * jnp.exp(bl - bc)
        stacked = jnp.concatenate([kd, jnp.broadcast_to(jnp.exp(bl), (chunk, dk))], axis=0)
        st = stacked.T
        vpad = jnp.concatenate([vc, jnp.zeros_like(vc)], axis=0)
        state_ref[...] = state * st[:, chunk:chunk + 1] + _dot(st.astype(BF16), vpad)

    o = osc_ref[...]
    ms = jnp.mean(o * o, axis=-1, keepdims=True)
    y = o * lax.rsqrt(ms + EPS) * gn_ref[...]
    g = gg_ref[...].astype(F32)
    o_ref[...] = (y * (g * _sigmoid(g))).astype(o_ref.dtype)


def _gla(proj, gates, wup_pad, b_gate, g_norm, *, batch, seq, heads, dk, dv, rows_blk=256):
    t = proj.shape[0]
    nr = seq // rows_blk
    kb0 = heads
    vb0 = 2 * heads * dk // dv
    gb0 = vb0 + heads
    kern = functools.partial(_gla_kernel, chunk=GLA_CHUNK, sub=GLA_SUB)
    return pl.pallas_call(
        kern,
        grid=(batch, heads, nr),
        in_specs=[pl.BlockSpec((rows_blk, dk), lambda b, h, r: (b * nr + r, h)),
                  pl.BlockSpec((rows_blk, dk), lambda b, h, r: (b * nr + r, kb0 + h)),
                  pl.BlockSpec((rows_blk, dv), lambda b, h, r: (b * nr + r, vb0 + h)),
                  pl.BlockSpec((rows_blk, dv), lambda b, h, r: (b * nr + r, gb0 + h)),
                  pl.BlockSpec((rows_blk, gates.shape[1]), lambda b, h, r: (b * nr + r, 0)),
                  pl.BlockSpec((wup_pad.shape[0], dk), lambda b, h, r: (0, h)),
                  pl.BlockSpec((1, dk), lambda b, h, r: (0, h)),
                  pl.BlockSpec((1, dv), lambda b, h, r: (0, 0))],
        out_specs=pl.BlockSpec((rows_blk, dv), lambda b, h, r: (b * nr + r, h)),
        out_shape=jax.ShapeDtypeStruct((t, heads * dv), BF16),
        scratch_shapes=[pltpu.VMEM((dk, dv), F32),
                        pltpu.VMEM((rows_blk, dv), F32)],
        compiler_params=_params("parallel", "parallel", "arbitrary"),
        name="gla",
    )(proj, proj, proj, proj, gates, wup_pad, b_gate, g_norm)


def _rglru_kernel(rx_ref, ry_ref, cw_ref, cb_ref, wx_ref, bx_ref, wa_ref, ba_ref, lam_ref,
                  o_ref, *, rows_t):
    seq, w = rx_ref.shape
    halo = 8
    wxb = wx_ref[...].astype(BF16)
    wab = wa_ref[...].astype(BF16)
    cw = cw_ref[...]
    neg_lam = -lam_ref[...]
    softplus = jnp.maximum(neg_lam, 0.0) + _softplus_neg_abs(neg_lam)
    coef = -RG_C * softplus
    row = lax.broadcasted_iota(jnp.int32, (rows_t, 1), 0)

    def body(t, carry):
        h_prev, tail = carry
        rows = pl.ds(pl.multiple_of(t * rows_t, rows_t), rows_t)
        x = rx_ref[rows, :].astype(F32)
        xe = jnp.concatenate([tail, x], axis=0)
        xc = cb_ref[...] + cw[RG_CONV - 1:RG_CONV] * x
        for s_ in range(1, RG_CONV):
            xc = xc + cw[RG_CONV - 1 - s_:RG_CONV - s_] * pltpu.roll(xe, s_, axis=0)[halo:]
        xcb = xc.astype(BF16)
        gate_x = _sigmoid(_dot(xcb, wxb) + bx_ref[...])
        gate_a = _sigmoid(_dot(xcb, wab) + ba_ref[...])
        log_a = coef * gate_a
        a_dec = jnp.exp(log_a)
        one_m_a2 = -jnp.tanh(log_a) * (a_dec * a_dec + 1.0)
        u = jnp.sqrt(one_m_a2) * (gate_x * xc)
        d = 1
        while d < rows_t:
            if d < halo:
                a_sh = jnp.where(row >= d, pltpu.roll(a_dec, d, axis=0), 1.0)
                u_sh = jnp.where(row >= d, pltpu.roll(u, d, axis=0), 0.0)
            else:
                a_sh = jnp.concatenate([jnp.ones((d, w), F32), a_dec[:rows_t - d]], axis=0)
                u_sh = jnp.concatenate([jnp.zeros((d, w), F32), u[:rows_t - d]], axis=0)
            u = a_dec * u_sh + u
            a_dec = a_dec * a_sh
            d *= 2
        h = u + a_dec * h_prev
        y = ry_ref[rows, :].astype(F32)
        o_ref[rows, :] = (h * _gelu_tanh(y)).astype(o_ref.dtype)
        return h[rows_t - 1:rows_t], x[rows_t - halo:]

    init = (jnp.zeros((1, w), F32), jnp.zeros((halo, w), F32))
    lax.fori_loop(0, seq // rows_t, body, init)


def _rglru(proj, conv_w, conv_b, w_x, b_x, w_a, b_a, lam, *, batch, seq, width, rows_t=256):
    t = proj.shape[0]
    wb = RG_BLOCK
    nb = width // wb
    kern = functools.partial(_rglru_kernel, rows_t=rows_t)
    vec = lambda v: v.reshape(1, width)
    vspec = pl.BlockSpec((1, wb), lambda b, g: (0, g))
    mspec = pl.BlockSpec((None, wb, wb), lambda b, g: (g, 0, 0))
    return pl.pallas_call(
        kern,
        grid=(batch, nb),
        in_specs=[pl.BlockSpec((seq, wb), lambda b, g: (b, nb + g)),
                  pl.BlockSpec((seq, wb), lambda b, g: (b, g)),
                  pl.BlockSpec((RG_CONV, wb), lambda b, g: (0, g)),
                  vspec, mspec, vspec, mspec, vspec, vspec],
        out_specs=pl.BlockSpec((seq, wb), lambda b, g: (b, g)),
        out_shape=jax.ShapeDtypeStruct((t, width), BF16),
        compiler_params=_params("parallel", "parallel"),
        name="rglru",
    )(proj, proj, conv_w, vec(conv_b), w_x, vec(b_x), w_a, vec(b_a), vec(lam))


_TM = 1024
_TN = 512


_TM_AS = 2048
_FOX_TQ = 512


def _ffn(h, g, w_gate, w_up, w_down, layer):
    u = _rmsnorm(h, g, BF16)
    hid = _matmul_as_swiglu(u, w_gate, w_up, layer, tm=_TM_AS, tn=256)
    return _matmul_as_res(hid, w_down, layer, h, tm=1024, tn=256)


def _even_mixer(h, g, w_in, i, w_gate_up, b_gate, g_norm, b_f, w_out, *, batch, seq):
    d = h.shape[1]
    dk, dv = d // 16, d // 8
    qk, vw, fw = GLA_HEADS * dk, GLA_HEADS * dv, d // 2
    fox_heads = fw // FOX_HEAD_DIM
    rank = GLA_GATE_RANK
    main0 = 2 * qk + 2 * vw
    main1 = 3 * fw
    wt_in = jnp.swapaxes(w_in, 1, 2)
    u, gates = _gate_proj_t(h, g, wt_in, i, col_a=main0, n_a=rank,
                            col_b=main0 + rank + main1, n_b=fox_heads)
    proj = _in_proj_as_t(u, wt_in, i, tm=_TM_AS, tn=_TN, n_plain=main0 // _TN,
                         n_total=(main0 + main1) // _TN, shift=rank)
    bias_row = jnp.zeros((1, LANES), F32).at[0, rank:rank + fox_heads].set(b_f)
    c, ct = _fox_cumsum(gates, bias_row, batch=batch, seq=seq, lane0=rank, heads=fox_heads)
    ck = ct.reshape(batch, fox_heads, seq // _FOX_TQ, 1, _FOX_TQ)
    o_fox = _fox_attention(proj, c, ck, batch=batch, seq=seq, heads=fox_heads, col0=main0,
                           lane0=rank, tq=_FOX_TQ)
    wup_pad = jnp.pad(w_gate_up, ((0, LANES - rank), (0, 0)))
    o_gla = _gla(proj, gates, wup_pad, b_gate.reshape(1, qk), g_norm.reshape(1, dv),
                 batch=batch, seq=seq, heads=GLA_HEADS, dk=dk, dv=dv)
    return _matmul_res2(o_gla, o_fox, w_out, i, h, tm=_TM, tn=_TN)


def _odd_mixer(h, g, w_in, i, conv_w, conv_b, w_x, b_x, w_a, b_a, lam, w_out, *, batch, seq):
    d = h.shape[1]
    width = d // 2
    sb_heads = width // SB_HEAD_DIM
    u = _rmsnorm(h, g, BF16)
    proj = _matmul_as(u, w_in, i, tm=_TM_AS, tn=_TN, out_dtype=BF16)
    o_rg = _rglru(proj, conv_w, conv_b, w_x, b_x, w_a, b_a, lam, batch=batch, seq=seq,
                  width=width)
    o_sb = _sb_attention(proj, batch=batch, seq=seq, heads=sb_heads, col0=2 * width)
    return _matmul_res2(o_rg, o_sb, w_out, i, h, tm=_TM, tn=_TN)


def kernel(x, norm_mix, norm_ffn, ffn_w_gate, ffn_w_up, ffn_w_down, ab_w_in, gla_w_gate_up,
           gla_b_gate, gla_norm, fox_b_f, ab_w_out, cd_w_in, rg_conv_w, rg_conv_b, rg_w_x,
           rg_b_x, rg_w_a, rg_b_a, rg_lambda, cd_w_out, final_norm):
    batch, seq, d = x.shape
    h = x.reshape(batch * seq, d)
    for layer in range(norm_mix.shape[0]):
        i = layer // 2
        if layer % 2 == 0:
            h = _even_mixer(h, norm_mix[layer], ab_w_in, i, gla_w_gate_up[i], gla_b_gate[i],
                            gla_norm[i], fox_b_f[i], ab_w_out, batch=batch, seq=seq)
        else:
            h = _odd_mixer(h, norm_mix[layer], cd_w_in, i, rg_conv_w[i], rg_conv_b[i],
                           rg_w_x[i], rg_b_x[i], rg_w_a[i], rg_b_a[i], rg_lambda[i],
                           cd_w_out, batch=batch, seq=seq)
        h = _ffn(h, norm_ffn[layer], ffn_w_gate, ffn_w_up, ffn_w_down, layer)
    return _rmsnorm(h, final_norm, F32).reshape(batch, seq, d)
```

```python
import functools
import math

import jax
import jax.numpy as jnp
from jax import lax
from jax.experimental import pallas as pl
from jax.experimental.pallas import tpu as pltpu

F32 = jnp.float32
BF16 = jnp.bfloat16

EPS = 1e-6
LANES = 128
VMEM_LIMIT = 56 * 1024 * 1024

GLA_HEADS = 4
GLA_GATE_RANK = 16
GLA_GATE_TAU = 16.0
GLA_CHUNK = 64
GLA_SUB = 16
FOX_HEAD_DIM = 128
SB_HEAD_DIM = 128
RG_BLOCK = 256
RG_CONV = 4
RG_C = 8.0

NEG_BIG = -1e30
_NT = (((1,), (1,)), ((), ()))


def _params(*sem):
    return pltpu.CompilerParams(dimension_semantics=sem, vmem_limit_bytes=VMEM_LIMIT)


def _dot(a, b):
    return jnp.dot(a, b, preferred_element_type=F32)


def _dot_nt(a, b):
    return lax.dot_general(a, b, _NT, preferred_element_type=F32)


def _split_bf16(x, n):
    parts = []
    r = x
    for _ in range(n):
        p = r.astype(BF16)
        parts.append(p)
        r = r - p.astype(F32)
    return parts


def _dot_f32(a, b):
    ah, al = _split_bf16(a, 2)
    bh, bl = _split_bf16(b, 2)
    return _dot(ah, bh) + _dot(ah, bl) + _dot(al, bh)


def _dot_exact_lhs(m01, x, n=3):
    acc = None
    for p in _split_bf16(x, n):
        t = _dot(m01, p)
        acc = t if acc is None else acc + t
    return acc


def _softplus_neg_abs(z):
    return jnp.log1p(jnp.exp(-jnp.abs(z)))


def _log_sigmoid(z):
    return jnp.minimum(z, 0.0) - _softplus_neg_abs(z)


def _sigmoid(z):
    return 1.0 / (1.0 + jnp.exp(-z))


def _gelu_tanh(y):
    c = math.sqrt(2.0 / math.pi)
    return y * (0.5 * (1.0 + jnp.tanh(c * (y + 0.044715 * (y * y * y)))))


def _rmsnorm_kernel(x_ref, g_ref, o_ref):
    x = x_ref[...]
    ms = jnp.mean(x * x, axis=-1, keepdims=True)
    o_ref[...] = (x * lax.rsqrt(ms + EPS) * g_ref[...]).astype(o_ref.dtype)


def _rmsnorm(x, g, out_dtype, tm=512):
    t, d = x.shape
    return pl.pallas_call(
        _rmsnorm_kernel,
        grid=(t // tm,),
        in_specs=[pl.BlockSpec((tm, d), lambda i: (i, 0)),
                  pl.BlockSpec((1, d), lambda i: (0, 0))],
        out_specs=pl.BlockSpec((tm, d), lambda i: (i, 0)),
        out_shape=jax.ShapeDtypeStruct((t, d), out_dtype),
        compiler_params=_params("parallel"),
        name="rmsnorm",
    )(x, g.reshape(1, d))


def _gate_proj_kernel(x_ref, g_ref, wa_ref, wb_ref, u_ref, o_ref, *, n_a, n_b):
    x = x_ref[...]
    ms = jnp.mean(x * x, axis=-1, keepdims=True)
    u = x * lax.rsqrt(ms + EPS) * g_ref[...]
    u_ref[...] = u.astype(u_ref.dtype)
    lane = lax.broadcasted_iota(jnp.int32, o_ref.shape, 1)
    o = _dot_f32(u, jnp.concatenate([wa_ref[...], wb_ref[...]], axis=1))
    oa, ob = o[:, :LANES], o[:, LANES:]
    o_ref[...] = jnp.where(lane < n_a, oa, jnp.where(lane < n_a + n_b, ob, 0.0))


def _gate_proj_t_kernel(x_ref, g_ref, wa_ref, wb_ref, u_ref, o_ref):
    x = x_ref[...]
    ms = jnp.mean(x * x, axis=-1, keepdims=True)
    u = x * lax.rsqrt(ms + EPS) * g_ref[...]
    u_ref[...] = u.astype(u_ref.dtype)
    used = wa_ref.shape[0] + wb_ref.shape[0]
    wt = jnp.concatenate([wa_ref[...], wb_ref[...],
                          jnp.zeros((o_ref.shape[1] - used, x.shape[1]), F32)], axis=0)
    uh, ul = _split_bf16(u, 2)
    wh, wl = _split_bf16(wt, 2)
    o_ref[...] = _dot_nt(uh, wh) + _dot_nt(uh, wl) + _dot_nt(ul, wh)


def _gate_proj_t(x, g, wt, layer, *, col_a, n_a, col_b, n_b, tm=256):
    t, d = x.shape
    assert col_a % n_a == 0 and col_b % n_b == 0
    return pl.pallas_call(
        _gate_proj_t_kernel,
        grid=(t // tm,),
        in_specs=[pl.BlockSpec((tm, d), lambda i: (i, 0)),
                  pl.BlockSpec((1, d), lambda i: (0, 0)),
                  pl.BlockSpec((None, n_a, d), lambda i: (layer, col_a // n_a, 0)),
                  pl.BlockSpec((None, n_b, d), lambda i: (layer, col_b // n_b, 0))],
        out_specs=[pl.BlockSpec((tm, d), lambda i: (i, 0)),
                   pl.BlockSpec((tm, LANES), lambda i: (i, 0))],
        out_shape=[jax.ShapeDtypeStruct((t, d), BF16),
                   jax.ShapeDtypeStruct((t, LANES), F32)],
        compiler_params=_params("parallel"),
        name="gate_proj_t",
    )(x, g.reshape(1, d), wt, wt)


def _gate_proj(x, g, w, layer, *, col_a, n_a, col_b, n_b, tm=256):
    t, d = x.shape
    assert col_a % LANES == 0 and col_b % LANES == n_a
    kern = functools.partial(_gate_proj_kernel, n_a=n_a, n_b=n_b)
    return pl.pallas_call(
        kern,
        grid=(t // tm,),
        in_specs=[pl.BlockSpec((tm, d), lambda i: (i, 0)),
                  pl.BlockSpec((1, d), lambda i: (0, 0)),
                  pl.BlockSpec((None, d, LANES), lambda i: (layer, 0, col_a // LANES)),
                  pl.BlockSpec((None, d, LANES), lambda i: (layer, 0, col_b // LANES))],
        out_specs=[pl.BlockSpec((tm, d), lambda i: (i, 0)),
                   pl.BlockSpec((tm, LANES), lambda i: (i, 0))],
        out_shape=[jax.ShapeDtypeStruct((t, d), BF16),
                   jax.ShapeDtypeStruct((t, LANES), F32)],
        compiler_params=_params("parallel"),
        name="gate_proj",
    )(x, g.reshape(1, d), w, w)


_CAST_ROWS = 512


def _cast_rows(w_ref, wb_ref):
    k = w_ref.shape[0]

    def body(r, c):
        rows = pl.ds(pl.multiple_of(r * _CAST_ROWS, _CAST_ROWS), _CAST_ROWS)
        wb_ref[rows, :] = w_ref[rows, :].astype(BF16)
        return c

    lax.fori_loop(0, k // _CAST_ROWS, body, 0)


def _mm_kernel(a_ref, w_ref, o_ref, wb_ref):
    @pl.when(pl.program_id(1) == 0)
    def _():
        _cast_rows(w_ref, wb_ref)

    o_ref[...] = _dot(a_ref[...], wb_ref[...]).astype(o_ref.dtype)


def _mm_res2_kernel(a1_ref, a2_ref, w_ref, r_ref, o_ref, wb_ref):
    @pl.when(pl.program_id(1) == 0)
    def _():
        _cast_rows(w_ref, wb_ref)

    k1 = a1_ref.shape[1]
    o_ref[...] = (r_ref[...] + _dot(a1_ref[...], wb_ref[:k1, :])
                  + _dot(a2_ref[...], wb_ref[k1:, :]))


def _mm_res_bf16w_kernel(a_ref, w_ref, r_ref, o_ref):
    o_ref[...] = r_ref[...] + _dot(a_ref[...], w_ref[...])


def _mm_swiglu_kernel(a_ref, wg_ref, wu_ref, o_ref, wgb_ref, wub_ref):
    @pl.when(pl.program_id(1) == 0)
    def _():
        _cast_rows(wg_ref, wgb_ref)
        _cast_rows(wu_ref, wub_ref)

    a = a_ref[...]
    g = _dot(a, wgb_ref[...])
    u = _dot(a, wub_ref[...])
    o_ref[...] = (g * _sigmoid(g) * u).astype(o_ref.dtype)


def _in_proj_shift_kernel(a_ref, w_ref, wx_ref, o_ref, wb_ref, *, n_plain, shift):
    j = pl.program_id(0)
    first = pl.program_id(1) == 0
    k, tn = w_ref.shape
    width = tn + wx_ref.shape[1]

    @pl.when(first & (j < n_plain))
    def _():
        _cast_rows(w_ref, wb_ref)

    @pl.when(first & (j >= n_plain))
    def _():
        def body(r, c):
            rows = pl.ds(pl.multiple_of(r * _CAST_ROWS, _CAST_ROWS), _CAST_ROWS)
            cat = jnp.concatenate([w_ref[rows, :], wx_ref[rows, :]], axis=1)
            wb_ref[rows, :] = pltpu.roll(cat, width - shift, axis=1)[:, :tn].astype(BF16)
            return c

        lax.fori_loop(0, k // _CAST_ROWS, body, 0)

    o_ref[...] = _dot(a_ref[...], wb_ref[...]).astype(o_ref.dtype)


def _mm_as_kernel(a_ref, w_ref, o_ref):
    o_ref[...] = _dot(a_ref[...], w_ref[...].astype(BF16)).astype(o_ref.dtype)


def _mm_as_swiglu_kernel(a_ref, wg_ref, wu_ref, o_ref):
    a = a_ref[...]
    g = _dot(a, wg_ref[...].astype(BF16))
    u = _dot(a, wu_ref[...].astype(BF16))
    o_ref[...] = (g * _sigmoid(g) * u).astype(o_ref.dtype)


def _in_proj_as_kernel(a_ref, w_ref, wx_ref, o_ref, *, n_plain, shift):
    j = pl.program_id(1)
    tn = w_ref.shape[1]
    width = tn + wx_ref.shape[1]

    @pl.when(j < n_plain)
    def _():
        o_ref[...] = _dot(a_ref[...], w_ref[...].astype(BF16)).astype(o_ref.dtype)

    @pl.when(j >= n_plain)
    def _():
        cat = jnp.concatenate([w_ref[...], wx_ref[...]], axis=1)
        w = pltpu.roll(cat, width - shift, axis=1)[:, :tn].astype(BF16)
        o_ref[...] = _dot(a_ref[...], w).astype(o_ref.dtype)


def _a_spec(tm, k):
    return pl.BlockSpec((tm, k), lambda i, j: (i, 0), pipeline_mode=pl.Buffered(1))


def _wt_spec(layer, k, tn):
    return pl.BlockSpec((None, k, tn), lambda i, j: (layer, 0, j))


def _matmul_as(a, w, layer, *, tm, tn, out_dtype):
    m, k = a.shape
    n = w.shape[2]
    return pl.pallas_call(
        _mm_as_kernel,
        grid=(m // tm, n // tn),
        in_specs=[_a_spec(tm, k), _wt_spec(layer, k, tn)],
        out_specs=pl.BlockSpec((tm, tn), lambda i, j: (i, j)),
        out_shape=jax.ShapeDtypeStruct((m, n), out_dtype),
        compiler_params=_params("parallel", "arbitrary"),
        name="matmul_as",
    )(a, w)


def _matmul_as_swiglu(a, wg, wu, layer, *, tm, tn):
    m, k = a.shape
    n = wg.shape[2]
    return pl.pallas_call(
        _mm_as_swiglu_kernel,
        grid=(m // tm, n // tn),
        in_specs=[_a_spec(tm, k), _wt_spec(layer, k, tn), _wt_spec(layer, k, tn)],
        out_specs=pl.BlockSpec((tm, tn), lambda i, j: (i, j)),
        out_shape=jax.ShapeDtypeStruct((m, n), BF16),
        compiler_params=_params("parallel", "arbitrary"),
        name="matmul_as_swiglu",
    )(a, wg, wu)


def _mm_as_res_kernel(a_ref, w_ref, r_ref, o_ref):
    o_ref[...] = r_ref[...] + _dot(a_ref[...], w_ref[...].astype(BF16))


def _matmul_as_res(a, w, layer, res, *, tm, tn):
    m, k = a.shape
    n = w.shape[2]
    return pl.pallas_call(
        _mm_as_res_kernel,
        grid=(m // tm, n // tn),
        in_specs=[_a_spec(tm, k), _wt_spec(layer, k, tn),
                  pl.BlockSpec((tm, tn), lambda i, j: (i, j))],
        out_specs=pl.BlockSpec((tm, tn), lambda i, j: (i, j)),
        out_shape=jax.ShapeDtypeStruct((m, n), F32),
        compiler_params=_params("parallel", "arbitrary"),
        name="matmul_as_res",
    )(a, w, res)


def _in_proj_as_t_kernel(a_ref, wt_ref, wtx_ref, o_ref, *, n_plain):
    j = pl.program_id(1)
    shift = wtx_ref.shape[0]

    @pl.when(j < n_plain)
    def _():
        o_ref[...] = _dot_nt(a_ref[...], wt_ref[...].astype(BF16)).astype(o_ref.dtype)

    @pl.when(j >= n_plain)
    def _():
        wt = jnp.concatenate([wt_ref[shift:, :], wtx_ref[...]], axis=0).astype(BF16)
        o_ref[...] = _dot_nt(a_ref[...], wt).astype(o_ref.dtype)


def _in_proj_as_t(a, wt, layer, *, tm, tn, n_plain, n_total, shift):
    m, k = a.shape
    assert tn % shift == 0
    kern = functools.partial(_in_proj_as_t_kernel, n_plain=n_plain)
    return pl.pallas_call(
        kern,
        grid=(m // tm, n_total),
        in_specs=[_a_spec(tm, k),
                  pl.BlockSpec((None, tn, k), lambda i, j: (layer, j, 0)),
                  pl.BlockSpec((None, shift, k), lambda i, j: (layer, (j + 1) * (tn // shift), 0))],
        out_specs=pl.BlockSpec((tm, tn), lambda i, j: (i, j)),
        out_shape=jax.ShapeDtypeStruct((m, n_total * tn), BF16),
        compiler_params=_params("parallel", "arbitrary"),
        name="in_proj_as_t",
    )(a, wt, wt)


def _in_proj_as(a, w, layer, *, tm, tn, n_plain, n_total, shift):
    m, k = a.shape
    nx = tn // LANES
    kern = functools.partial(_in_proj_as_kernel, n_plain=n_plain, shift=shift)
    return pl.pallas_call(
        kern,
        grid=(m // tm, n_total),
        in_specs=[_a_spec(tm, k), _wt_spec(layer, k, tn),
                  pl.BlockSpec((None, k, LANES), lambda i, j: (layer, 0, (j + 1) * nx))],
        out_specs=pl.BlockSpec((tm, tn), lambda i, j: (i, j)),
        out_shape=jax.ShapeDtypeStruct((m, n_total * tn), BF16),
        compiler_params=_params("parallel", "arbitrary"),
        name="in_proj_as",
    )(a, w, w)


def _w_spec(layer, k, tn):
    return pl.BlockSpec((None, k, tn), lambda j, i: (layer, 0, j))


def _matmul(a, w, layer, *, tm, tn, out_dtype):
    m, k = a.shape
    n = w.shape[2]
    return pl.pallas_call(
        _mm_kernel,
        grid=(n // tn, m // tm),
        in_specs=[pl.BlockSpec((tm, k), lambda j, i: (i, 0)), _w_spec(layer, k, tn)],
        out_specs=pl.BlockSpec((tm, tn), lambda j, i: (i, j)),
        out_shape=jax.ShapeDtypeStruct((m, n), out_dtype),
        scratch_shapes=[pltpu.VMEM((k, tn), BF16)],
        compiler_params=_params("arbitrary", "arbitrary"),
        name="matmul",
    )(a, w)


def _matmul_res2(a1, a2, w, layer, res, *, tm, tn):
    m, k1 = a1.shape
    k2 = a2.shape[1]
    k, n = w.shape[1:]
    assert k == k1 + k2
    return pl.pallas_call(
        _mm_res2_kernel,
        grid=(n // tn, m // tm),
        in_specs=[pl.BlockSpec((tm, k1), lambda j, i: (i, 0)),
                  pl.BlockSpec((tm, k2), lambda j, i: (i, 0)),
                  _w_spec(layer, k, tn),
                  pl.BlockSpec((tm, tn), lambda j, i: (i, j))],
        out_specs=pl.BlockSpec((tm, tn), lambda j, i: (i, j)),
        out_shape=jax.ShapeDtypeStruct((m, n), F32),
        scratch_shapes=[pltpu.VMEM((k, tn), BF16)],
        compiler_params=_params("arbitrary", "arbitrary"),
        name="matmul_res2",
    )(a1, a2, w, res)


def _matmul_res_bf16w(a, w, layer, res, *, tm, tn):
    m, k = a.shape
    n = w.shape[2]
    return pl.pallas_call(
        _mm_res_bf16w_kernel,
        grid=(n // tn, m // tm),
        in_specs=[pl.BlockSpec((tm, k), lambda j, i: (i, 0)),
                  _w_spec(layer, k, tn),
                  pl.BlockSpec((tm, tn), lambda j, i: (i, j))],
        out_specs=pl.BlockSpec((tm, tn), lambda j, i: (i, j)),
        out_shape=jax.ShapeDtypeStruct((m, n), F32),
        compiler_params=_params("arbitrary", "arbitrary"),
        name="matmul_res_bf16w",
    )(a, w, res)


def _matmul_swiglu(a, wg, wu, layer, *, tm, tn):
    m, k = a.shape
    n = wg.shape[2]
    return pl.pallas_call(
        _mm_swiglu_kernel,
        grid=(n // tn, m // tm),
        in_specs=[pl.BlockSpec((tm, k), lambda j, i: (i, 0)),
                  _w_spec(layer, k, tn), _w_spec(layer, k, tn)],
        out_specs=pl.BlockSpec((tm, tn), lambda j, i: (i, j)),
        out_shape=jax.ShapeDtypeStruct((m, n), BF16),
        scratch_shapes=[pltpu.VMEM((k, tn), BF16), pltpu.VMEM((k, tn), BF16)],
        compiler_params=_params("arbitrary", "arbitrary"),
        name="matmul_swiglu",
    )(a, wg, wu)


def _in_proj_shift(a, w, layer, *, tm, tn, n_plain, n_total, shift):
    m, k = a.shape
    nx = tn // LANES
    kern = functools.partial(_in_proj_shift_kernel, n_plain=n_plain, shift=shift)
    return pl.pallas_call(
        kern,
        grid=(n_total, m // tm),
        in_specs=[pl.BlockSpec((tm, k), lambda j, i: (i, 0)),
                  _w_spec(layer, k, tn),
                  pl.BlockSpec((None, k, LANES), lambda j, i: (layer, 0, (j + 1) * nx))],
        out_specs=pl.BlockSpec((tm, tn), lambda j, i: (i, j)),
        out_shape=jax.ShapeDtypeStruct((m, n_total * tn), BF16),
        scratch_shapes=[pltpu.VMEM((k, tn), BF16)],
        compiler_params=_params("arbitrary", "arbitrary"),
        name="in_proj_shift",
    )(a, w, w)


def _fox_cumsum_kernel(g_ref, b_ref, c_ref, ct_ref, c_sc, *, blk, lane0, heads):
    s = g_ref.shape[0]
    row = lax.broadcasted_iota(jnp.int32, (blk, blk), 0)
    col = lax.broadcasted_iota(jnp.int32, (blk, blk), 1)
    tri = (col <= row).astype(BF16)
    carry = jnp.zeros((1, g_ref.shape[1]), F32)
    for n in range(s // blk):
        rows = slice(n * blk, (n + 1) * blk)
        lf = _log_sigmoid(g_ref[rows, :] + b_ref[...])
        c = _dot_exact_lhs(tri, lf) + carry
        c_sc[rows, :] = c
        carry = c[blk - 1:blk, :]
    c_all = c_sc[...]
    c_ref[...] = c_all
    ct_ref[...] = c_all.T[lane0:lane0 + heads, :]


def _fox_cumsum(gates, bias_row, *, batch, seq, lane0, heads, blk=256):
    t, n = gates.shape
    kern = functools.partial(_fox_cumsum_kernel, blk=blk, lane0=lane0, heads=heads)
    return pl.pallas_call(
        kern,
        grid=(batch,),
        in_specs=[pl.BlockSpec((seq, n), lambda b: (b, 0)),
                  pl.BlockSpec((1, n), lambda b: (0, 0))],
        out_specs=[pl.BlockSpec((seq, n), lambda b: (b, 0)),
                   pl.BlockSpec((None, heads, seq), lambda b: (b, 0, 0))],
        out_shape=[jax.ShapeDtypeStruct((t, n), F32),
                   jax.ShapeDtypeStruct((batch, heads, seq), F32)],
        scratch_shapes=[pltpu.VMEM((seq, n), F32)],
        compiler_params=_params("parallel"),
        name="fox_cumsum",
    )(gates, bias_row)


def _fox_kernel(q_ref, k_ref, v_ref, c_ref, ck_ref, o_ref, *, tq, lane0):
    h = pl.program_id(1)
    nq = q_ref.shape[0] // tq

    def q_block(qi, carry):
        _fox_q_block(qi, h, q_ref, k_ref, v_ref, c_ref, ck_ref, o_ref, tq=tq, lane0=lane0)
        return carry

    lax.fori_loop(0, nq, q_block, 0)


def _fox_q_block(qi, h, q_ref, k_ref, v_ref, c_ref, ck_ref, o_ref, *, tq, lane0):
    dh = q_ref.shape[1]
    scale = dh ** -0.5
    q_rows = pl.ds(pl.multiple_of(qi * tq, tq), tq)
    q = q_ref[q_rows, :]
    c_blk = c_ref[q_rows, :]
    lane = lax.broadcasted_iota(jnp.int32, c_blk.shape, 1)
    cq = jnp.sum(jnp.where(lane == lane0 + h, c_blk, 0.0), axis=1, keepdims=True)

    def logits(kb):
        rows = pl.ds(pl.multiple_of(kb * tq, tq), tq)
        s = _dot_nt(q, k_ref[rows, :]) * scale
        return s + cq - ck_ref[kb], v_ref[rows, :]

    s, v = logits(qi)
    tpos = lax.broadcasted_iota(jnp.int32, (tq, tq), 0)
    spos = lax.broadcasted_iota(jnp.int32, (tq, tq), 1)
    s = jnp.where(spos <= tpos, s, NEG_BIG)
    m0 = jnp.max(s, axis=1, keepdims=True)
    p = jnp.exp(s - m0)
    l0 = jnp.sum(p, axis=1, keepdims=True)
    acc0 = _dot(p.astype(BF16), v)

    def body(kb, carry):
        m, l, acc = carry
        s, v = logits(kb)
        m_new = jnp.maximum(m, jnp.max(s, axis=1, keepdims=True))
        alpha = jnp.exp(m - m_new)
        p = jnp.exp(s - m_new)
        l = alpha * l + jnp.sum(p, axis=1, keepdims=True)
        acc = alpha * acc + _dot(p.astype(BF16), v)
        return m_new, l, acc

    _, l, acc = lax.fori_loop(0, qi, body, (m0, l0, acc0))
    o_ref[q_rows, :] = (acc / l).astype(o_ref.dtype)


def _fox_attention(proj, c, ck, *, batch, seq, heads, col0, lane0, tq=256):
    t = proj.shape[0]
    dh = FOX_HEAD_DIM
    nq = seq // tq
    qb = col0 // dh
    kern = functools.partial(_fox_kernel, tq=tq, lane0=lane0)
    return pl.pallas_call(
        kern,
        grid=(batch, heads),
        in_specs=[pl.BlockSpec((seq, dh), lambda b, h: (b, qb + h)),
                  pl.BlockSpec((seq, dh), lambda b, h: (b, qb + heads + h)),
                  pl.BlockSpec((seq, dh), lambda b, h: (b, qb + 2 * heads + h)),
                  pl.BlockSpec((seq, c.shape[1]), lambda b, h: (b, 0)),
                  pl.BlockSpec((None, None, nq, 1, tq), lambda b, h: (b, h, 0, 0, 0))],
        out_specs=pl.BlockSpec((seq, dh), lambda b, h: (b, h)),
        out_shape=jax.ShapeDtypeStruct((t, heads * dh), BF16),
        compiler_params=_params("parallel", "parallel"),
        name="fox_attention",
    )(proj, proj, proj, c, ck)


def _sb_kernel(q_ref, k_ref, v_ref, o_ref, *, tq, tk):
    def q_block(qi, carry):
        _sb_q_block(qi, q_ref, k_ref, v_ref, o_ref, tq=tq, tk=tk)
        return carry

    lax.fori_loop(0, q_ref.shape[0] // tq, q_block, 0)


def _sb_q_block(qi, q_ref, k_ref, v_ref, o_ref, *, tq, tk):
    dh = q_ref.shape[1]
    scale = dh ** -0.5
    q_rows = pl.ds(pl.multiple_of(qi * tq, tq), tq)
    q = q_ref[q_rows, :]
    nd = tq // tk
    jj = lax.broadcasted_iota(jnp.int32, (tk, tk), 0)
    ss = lax.broadcasted_iota(jnp.int32, (tk, tk), 1)
    upper = (jj > ss).astype(BF16)
    row = lax.broadcasted_iota(jnp.int32, (tq, tk), 0)
    col = lax.broadcasted_iota(jnp.int32, (tq, tk), 1)

    def block(kb, run, mask):
        rows = pl.ds(pl.multiple_of(kb * tk, tk), tk)
        z = _dot_nt(q, k_ref[rows, :]) * scale
        sp = jnp.log(1.0 + jnp.exp(-jnp.abs(z)))
        log_beta = jnp.minimum(z, 0.0) - sp
        log_1m = log_beta - z
        if mask is not None:
            log_1m = jnp.where(mask, log_1m, 0.0)
        hi, lo = _split_bf16(log_1m, 2)
        suffix = _dot(hi, upper) + _dot(lo, upper)
        a = jnp.exp(log_beta + suffix + run)
        if mask is not None:
            a = jnp.where(mask, a, 0.0)
        pv = _dot(a.astype(BF16), v_ref[rows, :])
        return pv, jnp.sum(log_1m, axis=1, keepdims=True)

    run = jnp.zeros((tq, 1), F32)
    acc = jnp.zeros((tq, dh), F32)
    for d in range(nd):
        off = (nd - 1 - d) * tk
        pv, rs = block((qi + 1) * nd - 1 - d, run, (off + col) < row)
        acc = acc + pv
        run = run + rs

    def body(p, carry):
        run, acc = carry
        for d in range(nd):
            pv, rs = block((qi - p) * nd - 1 - d, run, None)
            acc = acc + pv
            run = run + rs
        return run, acc

    _, acc = lax.fori_loop(0, qi, body, (run, acc))
    o_ref[q_rows, :] = acc.astype(o_ref.dtype)


def _sb_attention(proj, *, batch, seq, heads, col0, tq=512, tk=256):
    t = proj.shape[0]
    dh = SB_HEAD_DIM
    nq = seq // tq
    qb = col0 // dh
    kern = functools.partial(_sb_kernel, tq=tq, tk=tk)
    return pl.pallas_call(
        kern,
        grid=(batch, heads),
        in_specs=[pl.BlockSpec((seq, dh), lambda b, h: (b, qb + h)),
                  pl.BlockSpec((seq, dh), lambda b, h: (b, qb + heads + h)),
                  pl.BlockSpec((seq, dh), lambda b, h: (b, qb + 2 * heads + h))],
        out_specs=pl.BlockSpec((seq, dh), lambda b, h: (b, h)),
        out_shape=jax.ShapeDtypeStruct((t, heads * dh), BF16),
        compiler_params=_params("parallel", "parallel"),
        name="sb_attention",
    )(proj, proj, proj)


def _gla_kernel(q_ref, k_ref, v_ref, gg_ref, gate_ref, wup_ref, bg_ref, gn_ref, o_ref,
                state_ref, osc_ref, *, rows_blk, chunk, sub):
    state_ref[...] = jnp.zeros_like(state_ref)

    def row_block(r, carry):
        _gla_row_block(pl.multiple_of(r * rows_blk, rows_blk), q_ref, k_ref, v_ref, gg_ref,
                       gate_ref, wup_ref, bg_ref, gn_ref, o_ref, state_ref, osc_ref,
                       rows_blk=rows_blk, chunk=chunk, sub=sub)
        return carry

    lax.fori_loop(0, q_ref.shape[0] // rows_blk, row_block, 0)


def _gla_row_block(base, q_ref, k_ref, v_ref, gg_ref, gate_ref, wup_ref, bg_ref, gn_ref, o_ref,
                   state_ref, osc_ref, *, rows_blk, chunk, sub):
    dk = q_ref.shape[1]
    n_chunks = rows_blk // chunk
    n_sub = chunk // sub
    q_scale = dk ** -0.5
    blk_rows = pl.ds(base, rows_blk)

    z = _dot_f32(gate_ref[blk_rows, :], wup_ref[...]) + bg_ref[...]
    a = _log_sigmoid(z) / GLA_GATE_TAU
    row = lax.broadcasted_iota(jnp.int32, (rows_blk, rows_blk), 0)
    col = lax.broadcasted_iota(jnp.int32, (rows_blk, rows_blk), 1)
    shift = int(math.log2(chunk))
    same = lax.shift_right_logical(row, shift) == lax.shift_right_logical(col, shift)
    tri = ((col <= row) & same).astype(BF16)
    b = _dot_exact_lhs(tri, a)

    key_row = lax.broadcasted_iota(jnp.int32, (chunk, 1), 0)
    key_lane = lax.broadcasted_iota(jnp.int32, (sub, chunk), 1)
    band_row = lax.broadcasted_iota(jnp.int32, (sub, chunk), 0)
    for c in range(n_chunks):
        r0 = c * chunk
        bc = b[r0:r0 + chunk]
        c_rows = pl.ds(pl.multiple_of(base + r0, chunk), chunk)
        qc = q_ref[c_rows, :].astype(F32) * q_scale
        kc = k_ref[c_rows, :].astype(F32)
        vc = v_ref[c_rows, :]
        state = state_ref[...]
        inter = _dot((qc * jnp.exp(bc)).astype(BF16), state.astype(BF16))
        bands = []
        for i in range(n_sub):
            lo, hi = i * sub, (i + 1) * sub
            qs, ks, bs = qc[lo:hi], kc[lo:hi], bc[lo:hi]
            if i == 0:
                s = jnp.zeros((sub, chunk), F32)
            else:
                ref = bc[lo - 1:lo]
                qi = (qs * jnp.exp(bs - ref)).astype(BF16)
                kdec = jnp.exp(jnp.minimum(ref - bc, 0.0))
                ki = jnp.where(key_row < lo, kc * kdec, 0.0).astype(BF16)
                s = _dot_nt(qi, ki)
            for s_ in range(sub):
                dec = jnp.exp(jnp.minimum(bs - bs[s_:s_ + 1], 0.0))
                col = jnp.sum(qs * ks[s_:s_ + 1] * dec, axis=1, keepdims=True)
                s = jnp.where((key_lane == lo + s_) & (band_row >= s_), col, s)
            bands.append(s)
        scores = jnp.concatenate(bands, axis=0)
        osc_ref[r0:r0 + chunk, :] = inter + _dot(scores.astype(BF16), vc)

        bl = bc[chunk - 1:chunk]
        kd = kc * jnp.exp(bl - bc)
        stacked = jnp.concatenate([kd, jnp.broadcast_to(jnp.exp(bl), (chunk, dk))], axis=0)
        st = stacked.T
        vpad = jnp.concatenate([vc, jnp.zeros_like(vc)], axis=0)
        state_ref[...] = state * st[:, chunk:chunk + 1] + _dot(st.astype(BF16), vpad)

    o = osc_ref[...]
    ms = jnp.mean(o * o, axis=-1, keepdims=True)
    y = o * lax.rsqrt(ms + EPS) * gn_ref[...]
    g = gg_ref[blk_rows, :].astype(F32)
    o_ref[blk_rows, :] = (y * (g * _sigmoid(g))).astype(o_ref.dtype)


def _gla(proj, gates, wup_pad, b_gate, g_norm, *, batch, seq, heads, dk, dv, rows_blk=256):
    t = proj.shape[0]
    nr = seq // rows_blk
    kb0 = heads
    vb0 = 2 * heads * dk // dv
    gb0 = vb0 + heads
    kern = functools.partial(_gla_kernel, rows_blk=rows_blk, chunk=GLA_CHUNK, sub=GLA_SUB)
    return pl.pallas_call(
        kern,
        grid=(batch, heads),
        in_specs=[pl.BlockSpec((seq, dk), lambda b, h: (b, h)),
                  pl.BlockSpec((seq, dk), lambda b, h: (b, kb0 + h)),
                  pl.BlockSpec((seq, dv), lambda b, h: (b, vb0 + h)),
                  pl.BlockSpec((seq, dv), lambda b, h: (b, gb0 + h)),
                  pl.BlockSpec((seq, gates.shape[1]), lambda b, h: (b, 0)),
                  pl.BlockSpec((wup_pad.shape[0], dk), lambda b, h: (0, h)),
                  pl.BlockSpec((1, dk), lambda b, h: (0, h)),
                  pl.BlockSpec((1, dv), lambda b, h: (0, 0))],
        out_specs=pl.BlockSpec((seq, dv), lambda b, h: (b, h)),
        out_shape=jax.ShapeDtypeStruct((t, heads * dv), BF16),
        scratch_shapes=[pltpu.VMEM((dk, dv), F32),
                        pltpu.VMEM((rows_blk, dv), F32)],
        compiler_params=_params("parallel", "parallel"),
        name="gla",
    )(proj, proj, proj, proj, gates, wup_pad, b_gate, g_norm)


def _rglru_kernel(rx_ref, ry_ref, cw_ref, cb_ref, wx_ref, bx_ref, wa_ref, ba_ref, lam_ref,
                  o_ref, *, rows_t):
    seq, w = rx_ref.shape
    halo = 8
    wxb = wx_ref[...].astype(BF16)
    wab = wa_ref[...].astype(BF16)
    cw = cw_ref[...]
    neg_lam = -lam_ref[...]
    softplus = jnp.maximum(neg_lam, 0.0) + _softplus_neg_abs(neg_lam)
    coef = -RG_C * softplus
    row = lax.broadcasted_iota(jnp.int32, (rows_t, 1), 0)

    def body(t, carry):
        h_prev, tail = carry
        rows = pl.ds(pl.multiple_of(t * rows_t, rows_t), rows_t)
        x = rx_ref[rows, :].astype(F32)
        xe = jnp.concatenate([tail, x], axis=0)
        xc = cb_ref[...] + cw[RG_CONV - 1:RG_CONV] * x
        for s_ in range(1, RG_CONV):
            xc = xc + cw[RG_CONV - 1 - s_:RG_CONV - s_] * pltpu.roll(xe, s_, axis=0)[halo:]
        xcb = xc.astype(BF16)
        gate_x = _sigmoid(_dot(xcb, wxb) + bx_ref[...])
        gate_a = _sigmoid(_dot(xcb, wab) + ba_ref[...])
        log_a = coef * gate_a
        a_dec = jnp.exp(log_a)
        one_m_a2 = -jnp.tanh(log_a) * (a_dec * a_dec + 1.0)
        u = jnp.sqrt(one_m_a2) * (gate_x * xc)
        d = 1
        while d < rows_t:
            if d < halo:
                a_sh = jnp.where(row >= d, pltpu.roll(a_dec, d, axis=0), 1.0)
                u_sh = jnp.where(row >= d, pltpu.roll(u, d, axis=0), 0.0)
            else:
                a_sh = jnp.concatenate([jnp.ones((d, w), F32), a_dec[:rows_t - d]], axis=0)
                u_sh = jnp.concatenate([jnp.zeros((d, w), F32), u[:rows_t - d]], axis=0)
            u = a_dec * u_sh + u
            a_dec = a_dec * a_sh
            d *= 2
        h = u + a_dec * h_prev
        y = ry_ref[rows, :].astype(F32)
        o_ref[rows, :] = (h * _gelu_tanh(y)).astype(o_ref.dtype)
        return h[rows_t - 1:rows_t], x[rows_t - halo:]

    init = (jnp.zeros((1, w), F32), jnp.zeros((halo, w), F32))
    lax.fori_loop(0, seq // rows_t, body, init)


def _rglru(proj, conv_w, conv_b, w_x, b_x, w_a, b_a, lam, *, batch, seq, width, rows_t=256):
    t = proj.shape[0]
    wb = RG_BLOCK
    nb = width // wb
    kern = functools.partial(_rglru_kernel, rows_t=rows_t)
    vec = lambda v: v.reshape(1, width)
    vspec = pl.BlockSpec((1, wb), lambda b, g: (0, g))
    mspec = pl.BlockSpec((None, wb, wb), lambda b, g: (g, 0, 0))
    return pl.pallas_call(
        kern,
        grid=(batch, nb),
        in_specs=[pl.BlockSpec((seq, wb), lambda b, g: (b, nb + g)),
                  pl.BlockSpec((seq, wb), lambda b, g: (b, g)),
                  pl.BlockSpec((RG_CONV, wb), lambda b, g: (0, g)),
                  vspec, mspec, vspec, mspec, vspec, vspec],
        out_specs=pl.BlockSpec((seq, wb), lambda b, g: (b, g)),
        out_shape=jax.ShapeDtypeStruct((t, width), BF16),
        compiler_params=_params("parallel", "parallel"),
        name="rglru",
    )(proj, proj, conv_w, vec(conv_b), w_x, vec(b_x), w_a, vec(b_a), vec(lam))


_TM = 1024
_TN = 512


_TM_AS = 2048
_FOX_TQ = 512


def _ffn(h, g, w_gate, w_up, w_down, layer):
    u = _rmsnorm(h, g, BF16)
    hid = _matmul_as_swiglu(u, w_gate, w_up, layer, tm=_TM_AS, tn=256)
    return _matmul_as_res(hid, w_down, layer, h, tm=1024, tn=256)


def _even_mixer(h, g, w_in, i, w_gate_up, b_gate, g_norm, b_f, w_out, *, batch, seq):
    d = h.shape[1]
    dk, dv = d // 16, d // 8
    qk, vw, fw = GLA_HEADS * dk, GLA_HEADS * dv, d // 2
    fox_heads = fw // FOX_HEAD_DIM
    rank = GLA_GATE_RANK
    main0 = 2 * qk + 2 * vw
    main1 = 3 * fw
    wt_in = jnp.swapaxes(w_in, 1, 2)
    u, gates = _gate_proj_t(h, g, wt_in, i, col_a=main0, n_a=rank,
                            col_b=main0 + rank + main1, n_b=fox_heads)
    proj = _in_proj_as_t(u, wt_in, i, tm=_TM_AS, tn=_TN, n_plain=main0 // _TN,
                         n_total=(main0 + main1) // _TN, shift=rank)
    bias_row = jnp.zeros((1, LANES), F32).at[0, rank:rank + fox_heads].set(b_f)
    c, ct = _fox_cumsum(gates, bias_row, batch=batch, seq=seq, lane0=rank, heads=fox_heads)
    ck = ct.reshape(batch, fox_heads, seq // _FOX_TQ, 1, _FOX_TQ)
    o_fox = _fox_attention(proj, c, ck, batch=batch, seq=seq, heads=fox_heads, col0=main0,
                           lane0=rank, tq=_FOX_TQ)
    wup_pad = jnp.pad(w_gate_up, ((0, LANES - rank), (0, 0)))
    o_gla = _gla(proj, gates, wup_pad, b_gate.reshape(1, qk), g_norm.reshape(1, dv),
                 batch=batch, seq=seq, heads=GLA_HEADS, dk=dk, dv=dv)
    return _matmul_res2(o_gla, o_fox, w_out, i, h, tm=_TM, tn=_TN)


def _odd_mixer(h, g, w_in, i, conv_w, conv_b, w_x, b_x, w_a, b_a, lam, w_out, *, batch, seq):
    d = h.shape[1]
    width = d // 2
    sb_heads = width // SB_HEAD_DIM
    u = _rmsnorm(h, g, BF16)
    proj = _matmul_as(u, w_in, i, tm=_TM_AS, tn=_TN, out_dtype=BF16)
    o_rg = _rglru(proj, conv_w, conv_b, w_x, b_x, w_a, b_a, lam, batch=batch, seq=seq,
                  width=width)
    o_sb = _sb_attention(proj, batch=batch, seq=seq, heads=sb_heads, col0=2 * width)
    return _matmul_res2(o_rg, o_sb, w_out, i, h, tm=_TM, tn=_TN)


def kernel(x, norm_mix, norm_ffn, ffn_w_gate, ffn_w_up, ffn_w_down, ab_w_in, gla_w_gate_up,
           gla_b_gate, gla_norm, fox_b_f, ab_w_out, cd_w_in, rg_conv_w, rg_conv_b, rg_w_x,
           rg_b_x, rg_w_a, rg_b_a, rg_lambda, cd_w_out, final_norm):
    batch, seq, d = x.shape
    h = x.reshape(batch * seq, d)
    for layer in range(norm_mix.shape[0]):
        i = layer // 2
        if layer % 2 == 0:
            h = _even_mixer(h, norm_mix[layer], ab_w_in, i, gla_w_gate_up[i], gla_b_gate[i],
                            gla_norm[i], fox_b_f[i], ab_w_out, batch=batch, seq=seq)
        else:
            h = _odd_mixer(h, norm_mix[layer], cd_w_in, i, rg_conv_w[i], rg_conv_b[i],
                           rg_w_x[i], rg_b_x[i], rg_w_a[i], rg_b_a[i], rg_lambda[i],
                           cd_w_out, batch=batch, seq=seq)
        h = _ffn(h, norm_ffn[layer], ffn_w_gate, ffn_w_up, ffn_w_down, layer)
    return _rmsnorm(h, final_norm, F32).reshape(batch, seq, d)
```

```python
import functools
import math

import jax
import jax.numpy as jnp
from jax import lax
from jax.experimental import pallas as pl
from jax.experimental.pallas import tpu as pltpu

F32 = jnp.float32
BF16 = jnp.bfloat16

EPS = 1e-6
LANES = 128
VMEM_LIMIT = 56 * 1024 * 1024

GLA_HEADS = 4
GLA_GATE_RANK = 16
GLA_GATE_TAU = 16.0
GLA_CHUNK = 64
GLA_SUB = 16
FOX_HEAD_DIM = 128
SB_HEAD_DIM = 128
RG_BLOCK = 256
RG_CONV = 4
RG_C = 8.0

TM_AS = 2048
TN = 512
TN_SWIGLU = 256
TM_DOWN, TN_DOWN = 1024, 256
TM_OUT = 1024
FOX_TQ = 512
SB_TQ, SB_TK = 512, 512

NEG_BIG = -1e30
_NT = (((1,), (1,)), ((), ()))


def _params(*sem):
    return pltpu.CompilerParams(dimension_semantics=sem, vmem_limit_bytes=VMEM_LIMIT)


def _dot(a, b):
    return jnp.dot(a, b, preferred_element_type=F32)


def _dot_nt(a, b):
    return lax.dot_general(a, b, _NT, preferred_element_type=F32)


def _split_bf16(x, n):
    parts = []
    r = x
    for _ in range(n):
        p = r.astype(BF16)
        parts.append(p)
        r = r - p.astype(F32)
    return parts


def _dot_f32(a, b):
    ah, al = _split_bf16(a, 2)
    bh, bl = _split_bf16(b, 2)
    return _dot(ah, bh) + _dot(ah, bl) + _dot(al, bh)


def _dot_exact_lhs(m01, x, n=3):
    acc = None
    for p in _split_bf16(x, n):
        t = _dot(m01, p)
        acc = t if acc is None else acc + t
    return acc


def _softplus_neg_abs(z):
    return jnp.log1p(jnp.exp(-jnp.abs(z)))


def _log_sigmoid(z):
    return jnp.minimum(z, 0.0) - _softplus_neg_abs(z)


def _sigmoid(z):
    return 1.0 / (1.0 + jnp.exp(-z))


def _gelu_tanh(y):
    c = math.sqrt(2.0 / math.pi)
    return y * (0.5 * (1.0 + jnp.tanh(c * (y + 0.044715 * (y * y * y)))))


def _rmsnorm_kernel(x_ref, g_ref, o_ref):
    x = x_ref[...]
    ms = jnp.mean(x * x, axis=-1, keepdims=True)
    o_ref[...] = (x * lax.rsqrt(ms + EPS) * g_ref[...]).astype(o_ref.dtype)


def _rmsnorm(x, g, out_dtype, tm=512):
    t, d = x.shape
    return pl.pallas_call(
        _rmsnorm_kernel,
        grid=(t // tm,),
        in_specs=[pl.BlockSpec((tm, d), lambda i: (i, 0)),
                  pl.BlockSpec((1, d), lambda i: (0, 0))],
        out_specs=pl.BlockSpec((tm, d), lambda i: (i, 0)),
        out_shape=jax.ShapeDtypeStruct((t, d), out_dtype),
        compiler_params=_params("parallel"),
        name="rmsnorm",
    )(x, g.reshape(1, d))


def _gate_proj_t_kernel(x_ref, g_ref, wa_ref, wb_ref, u_ref, o_ref):
    x = x_ref[...]
    ms = jnp.mean(x * x, axis=-1, keepdims=True)
    u = x * lax.rsqrt(ms + EPS) * g_ref[...]
    u_ref[...] = u.astype(u_ref.dtype)
    used = wa_ref.shape[0] + wb_ref.shape[0]
    wt = jnp.concatenate([wa_ref[...], wb_ref[...],
                          jnp.zeros((o_ref.shape[1] - used, x.shape[1]), F32)], axis=0)
    uh, ul = _split_bf16(u, 2)
    wh, wl = _split_bf16(wt, 2)
    o_ref[...] = _dot_nt(uh, wh) + _dot_nt(uh, wl) + _dot_nt(ul, wh)


def _gate_proj_t(x, g, wt, layer, *, col_a, n_a, col_b, n_b, tm=256):
    t, d = x.shape
    assert col_a % n_a == 0 and col_b % n_b == 0
    return pl.pallas_call(
        _gate_proj_t_kernel,
        grid=(t // tm,),
        in_specs=[pl.BlockSpec((tm, d), lambda i: (i, 0)),
                  pl.BlockSpec((1, d), lambda i: (0, 0)),
                  pl.BlockSpec((None, n_a, d), lambda i: (layer, col_a // n_a, 0)),
                  pl.BlockSpec((None, n_b, d), lambda i: (layer, col_b // n_b, 0))],
        out_specs=[pl.BlockSpec((tm, d), lambda i: (i, 0)),
                   pl.BlockSpec((tm, LANES), lambda i: (i, 0))],
        out_shape=[jax.ShapeDtypeStruct((t, d), BF16),
                   jax.ShapeDtypeStruct((t, LANES), F32)],
        compiler_params=_params("parallel"),
        name="gate_proj_t",
    )(x, g.reshape(1, d), wt, wt)


def _mm_as_kernel(a_ref, w_ref, o_ref):
    o_ref[...] = _dot(a_ref[...], w_ref[...].astype(BF16)).astype(o_ref.dtype)


def _mm_as_swiglu_kernel(a_ref, wg_ref, wu_ref, o_ref):
    a = a_ref[...]
    g = _dot(a, wg_ref[...].astype(BF16))
    u = _dot(a, wu_ref[...].astype(BF16))
    o_ref[...] = (g * _sigmoid(g) * u).astype(o_ref.dtype)


def _mm_as_res_kernel(a_ref, w_ref, r_ref, o_ref):
    o_ref[...] = r_ref[...] + _dot(a_ref[...], w_ref[...].astype(BF16))


def _a_spec(tm, k):
    return pl.BlockSpec((tm, k), lambda i, j: (i, 0), pipeline_mode=pl.Buffered(1))


def _wt_spec(layer, k, tn):
    return pl.BlockSpec((None, k, tn), lambda i, j: (layer, 0, j))


def _matmul_as(a, w, layer, *, tm, tn, out_dtype):
    m, k = a.shape
    n = w.shape[2]
    return pl.pallas_call(
        _mm_as_kernel,
        grid=(m // tm, n // tn),
        in_specs=[_a_spec(tm, k), _wt_spec(layer, k, tn)],
        out_specs=pl.BlockSpec((tm, tn), lambda i, j: (i, j)),
        out_shape=jax.ShapeDtypeStruct((m, n), out_dtype),
        compiler_params=_params("parallel", "arbitrary"),
        name="matmul_as",
    )(a, w)


def _matmul_as_swiglu(a, wg, wu, layer, *, tm, tn):
    m, k = a.shape
    n = wg.shape[2]
    return pl.pallas_call(
        _mm_as_swiglu_kernel,
        grid=(m // tm, n // tn),
        in_specs=[_a_spec(tm, k), _wt_spec(layer, k, tn), _wt_spec(layer, k, tn)],
        out_specs=pl.BlockSpec((tm, tn), lambda i, j: (i, j)),
        out_shape=jax.ShapeDtypeStruct((m, n), BF16),
        compiler_params=_params("parallel", "arbitrary"),
        name="matmul_as_swiglu",
    )(a, wg, wu)


def _matmul_as_res(a, w, layer, res, *, tm, tn):
    m, k = a.shape
    n = w.shape[2]
    return pl.pallas_call(
        _mm_as_res_kernel,
        grid=(m // tm, n // tn),
        in_specs=[_a_spec(tm, k), _wt_spec(layer, k, tn),
                  pl.BlockSpec((tm, tn), lambda i, j: (i, j))],
        out_specs=pl.BlockSpec((tm, tn), lambda i, j: (i, j)),
        out_shape=jax.ShapeDtypeStruct((m, n), F32),
        compiler_params=_params("parallel", "arbitrary"),
        name="matmul_as_res",
    )(a, w, res)


def _in_proj_as_t_kernel(a_ref, wt_ref, wtx_ref, o_ref, *, n_plain):
    j = pl.program_id(1)
    shift = wtx_ref.shape[0]

    @pl.when(j < n_plain)
    def _():
        o_ref[...] = _dot_nt(a_ref[...], wt_ref[...].astype(BF16)).astype(o_ref.dtype)

    @pl.when(j >= n_plain)
    def _():
        wt = jnp.concatenate([wt_ref[shift:, :], wtx_ref[...]], axis=0).astype(BF16)
        o_ref[...] = _dot_nt(a_ref[...], wt).astype(o_ref.dtype)


def _in_proj_as_t(a, wt, layer, *, tm, tn, n_plain, n_total, shift):
    m, k = a.shape
    assert tn % shift == 0
    kern = functools.partial(_in_proj_as_t_kernel, n_plain=n_plain)
    return pl.pallas_call(
        kern,
        grid=(m // tm, n_total),
        in_specs=[_a_spec(tm, k),
                  pl.BlockSpec((None, tn, k), lambda i, j: (layer, j, 0)),
                  pl.BlockSpec((None, shift, k), lambda i, j: (layer, (j + 1) * (tn // shift), 0))],
        out_specs=pl.BlockSpec((tm, tn), lambda i, j: (i, j)),
        out_shape=jax.ShapeDtypeStruct((m, n_total * tn), BF16),
        compiler_params=_params("parallel", "arbitrary"),
        name="in_proj_as_t",
    )(a, wt, wt)


_CAST_ROWS = 512


def _cast_rows(w_ref, wb_ref):
    k = w_ref.shape[0]

    def body(r, c):
        rows = pl.ds(pl.multiple_of(r * _CAST_ROWS, _CAST_ROWS), _CAST_ROWS)
        wb_ref[rows, :] = w_ref[rows, :].astype(BF16)
        return c

    lax.fori_loop(0, k // _CAST_ROWS, body, 0)


def _mm_res2_kernel(a1_ref, a2_ref, w_ref, r_ref, o_ref, wb_ref):
    @pl.when(pl.program_id(1) == 0)
    def _():
        _cast_rows(w_ref, wb_ref)

    k1 = a1_ref.shape[1]
    o_ref[...] = (r_ref[...] + _dot(a1_ref[...], wb_ref[:k1, :])
                  + _dot(a2_ref[...], wb_ref[k1:, :]))


def _matmul_res2(a1, a2, w, layer, res, *, tm, tn):
    m, k1 = a1.shape
    k2 = a2.shape[1]
    k, n = w.shape[1:]
    assert k == k1 + k2
    return pl.pallas_call(
        _mm_res2_kernel,
        grid=(n // tn, m // tm),
        in_specs=[pl.BlockSpec((tm, k1), lambda j, i: (i, 0)),
                  pl.BlockSpec((tm, k2), lambda j, i: (i, 0)),
                  pl.BlockSpec((None, k, tn), lambda j, i: (layer, 0, j)),
                  pl.BlockSpec((tm, tn), lambda j, i: (i, j))],
        out_specs=pl.BlockSpec((tm, tn), lambda j, i: (i, j)),
        out_shape=jax.ShapeDtypeStruct((m, n), F32),
        scratch_shapes=[pltpu.VMEM((k, tn), BF16)],
        compiler_params=_params("arbitrary", "arbitrary"),
        name="matmul_res2",
    )(a1, a2, w, res)


def _fox_cumsum_kernel(g_ref, b_ref, c_ref, ct_ref, c_sc, *, blk, lane0, heads):
    s = g_ref.shape[0]
    row = lax.broadcasted_iota(jnp.int32, (blk, blk), 0)
    col = lax.broadcasted_iota(jnp.int32, (blk, blk), 1)
    tri = (col <= row).astype(BF16)
    carry = jnp.zeros((1, g_ref.shape[1]), F32)
    for n in range(s // blk):
        rows = slice(n * blk, (n + 1) * blk)
        lf = _log_sigmoid(g_ref[rows, :] + b_ref[...])
        c = _dot_exact_lhs(tri, lf) + carry
        c_sc[rows, :] = c
        carry = c[blk - 1:blk, :]
    c_all = c_sc[...]
    c_ref[...] = c_all
    ct_ref[...] = c_all.T[lane0:lane0 + heads, :]


def _fox_cumsum(gates, bias_row, *, batch, seq, lane0, heads, blk=256):
    t, n = gates.shape
    kern = functools.partial(_fox_cumsum_kernel, blk=blk, lane0=lane0, heads=heads)
    return pl.pallas_call(
        kern,
        grid=(batch,),
        in_specs=[pl.BlockSpec((seq, n), lambda b: (b, 0)),
                  pl.BlockSpec((1, n), lambda b: (0, 0))],
        out_specs=[pl.BlockSpec((seq, n), lambda b: (b, 0)),
                   pl.BlockSpec((None, heads, seq), lambda b: (b, 0, 0))],
        out_shape=[jax.ShapeDtypeStruct((t, n), F32),
                   jax.ShapeDtypeStruct((batch, heads, seq), F32)],
        scratch_shapes=[pltpu.VMEM((seq, n), F32)],
        compiler_params=_params("parallel"),
        name="fox_cumsum",
    )(gates, bias_row)


def _fox_kernel(q_ref, k_ref, v_ref, c_ref, ck_ref, o_ref, *, tq, lane0):
    h = pl.program_id(1)
    nq = q_ref.shape[0] // tq

    def q_block(qi, carry):
        _fox_q_block(qi, h, q_ref, k_ref, v_ref, c_ref, ck_ref, o_ref, tq=tq, lane0=lane0)
        return carry

    lax.fori_loop(0, nq, q_block, 0)


def _fox_q_block(qi, h, q_ref, k_ref, v_ref, c_ref, ck_ref, o_ref, *, tq, lane0):
    dh = q_ref.shape[1]
    scale = dh ** -0.5
    q_rows = pl.ds(pl.multiple_of(qi * tq, tq), tq)
    q = q_ref[q_rows, :]
    c_blk = c_ref[q_rows, :]
    lane = lax.broadcasted_iota(jnp.int32, c_blk.shape, 1)
    cq = jnp.sum(jnp.where(lane == lane0 + h, c_blk, 0.0), axis=1, keepdims=True)

    def logits(kb):
        rows = pl.ds(pl.multiple_of(kb * tq, tq), tq)
        s = _dot_nt(q, k_ref[rows, :]) * scale
        return s + cq - ck_ref[kb], v_ref[rows, :]

    s, v = logits(qi)
    tpos = lax.broadcasted_iota(jnp.int32, (tq, tq), 0)
    spos = lax.broadcasted_iota(jnp.int32, (tq, tq), 1)
    s = jnp.where(spos <= tpos, s, NEG_BIG)
    m0 = jnp.max(s, axis=1, keepdims=True)
    p = jnp.exp(s - m0)
    l0 = jnp.sum(p, axis=1, keepdims=True)
    acc0 = _dot(p.astype(BF16), v)

    def body(kb, carry):
        m, l, acc = carry
        s, v = logits(kb)
        m_new = jnp.maximum(m, jnp.max(s, axis=1, keepdims=True))
        alpha = jnp.exp(m - m_new)
        p = jnp.exp(s - m_new)
        l = alpha * l + jnp.sum(p, axis=1, keepdims=True)
        acc = alpha * acc + _dot(p.astype(BF16), v)
        return m_new, l, acc

    _, l, acc = lax.fori_loop(0, qi, body, (m0, l0, acc0))
    o_ref[q_rows, :] = (acc / l).astype(o_ref.dtype)


def _fox_attention(proj, c, ck, *, batch, seq, heads, col0, lane0, tq):
    t = proj.shape[0]
    dh = FOX_HEAD_DIM
    nq = seq // tq
    qb = col0 // dh
    kern = functools.partial(_fox_kernel, tq=tq, lane0=lane0)
    return pl.pallas_call(
        kern,
        grid=(batch, heads),
        in_specs=[pl.BlockSpec((seq, dh), lambda b, h: (b, qb + h)),
                  pl.BlockSpec((seq, dh), lambda b, h: (b, qb + heads + h)),
                  pl.BlockSpec((seq, dh), lambda b, h: (b, qb + 2 * heads + h)),
                  pl.BlockSpec((seq, c.shape[1]), lambda b, h: (b, 0)),
                  pl.BlockSpec((None, None, nq, 1, tq), lambda b, h: (b, h, 0, 0, 0))],
        out_specs=pl.BlockSpec((seq, dh), lambda b, h: (b, h)),
        out_shape=jax.ShapeDtypeStruct((t, heads * dh), BF16),
        compiler_params=_params("parallel", "parallel"),
        name="fox_attention",
    )(proj, proj, proj, c, ck)


def _sb_kernel(q_ref, k_ref, v_ref, o_ref, *, tq, tk):
    def q_block(qi, carry):
        _sb_q_block(qi, q_ref, k_ref, v_ref, o_ref, tq=tq, tk=tk)
        return carry

    lax.fori_loop(0, q_ref.shape[0] // tq, q_block, 0)


def _sb_q_block(qi, q_ref, k_ref, v_ref, o_ref, *, tq, tk):
    dh = q_ref.shape[1]
    scale = dh ** -0.5
    q_rows = pl.ds(pl.multiple_of(qi * tq, tq), tq)
    q = q_ref[q_rows, :]
    nd = tq // tk
    jj = lax.broadcasted_iota(jnp.int32, (tk, tk), 0)
    ss = lax.broadcasted_iota(jnp.int32, (tk, tk), 1)
    upper = (jj > ss).astype(BF16)
    row = lax.broadcasted_iota(jnp.int32, (tq, tk), 0)
    col = lax.broadcasted_iota(jnp.int32, (tq, tk), 1)

    def block(kb, run, mask):
        rows = pl.ds(pl.multiple_of(kb * tk, tk), tk)
        z = _dot_nt(q, k_ref[rows, :]) * scale
        sp = jnp.log(1.0 + jnp.exp(-jnp.abs(z)))
        log_beta = jnp.minimum(z, 0.0) - sp
        log_1m = log_beta - z
        if mask is not None:
            log_1m = jnp.where(mask, log_1m, 0.0)
        hi, lo = _split_bf16(log_1m, 2)
        suffix = _dot(hi, upper) + _dot(lo, upper)
        a = jnp.exp(log_beta + suffix + run)
        if mask is not None:
            a = jnp.where(mask, a, 0.0)
        pv = _dot(a.astype(BF16), v_ref[rows, :])
        return pv, jnp.sum(log_1m, axis=1, keepdims=True)

    run = jnp.zeros((tq, 1), F32)
    acc = jnp.zeros((tq, dh), F32)
    for d in range(nd):
        off = (nd - 1 - d) * tk
        pv, rs = block((qi + 1) * nd - 1 - d, run, (off + col) < row)
        acc = acc + pv
        run = run + rs

    def body(p, carry):
        run, acc = carry
        for d in range(nd):
            pv, rs = block((qi - p) * nd - 1 - d, run, None)
            acc = acc + pv
            run = run + rs
        return run, acc

    _, acc = lax.fori_loop(0, qi, body, (run, acc))
    o_ref[q_rows, :] = acc.astype(o_ref.dtype)


def _sb_attention(proj, *, batch, seq, heads, col0, tq, tk):
    t = proj.shape[0]
    dh = SB_HEAD_DIM
    qb = col0 // dh
    kern = functools.partial(_sb_kernel, tq=tq, tk=tk)
    return pl.pallas_call(
        kern,
        grid=(batch, heads),
        in_specs=[pl.BlockSpec((seq, dh), lambda b, h: (b, qb + h)),
                  pl.BlockSpec((seq, dh), lambda b, h: (b, qb + heads + h)),
                  pl.BlockSpec((seq, dh), lambda b, h: (b, qb + 2 * heads + h))],
        out_specs=pl.BlockSpec((seq, dh), lambda b, h: (b, h)),
        out_shape=jax.ShapeDtypeStruct((t, heads * dh), BF16),
        compiler_params=_params("parallel", "parallel"),
        name="sb_attention",
    )(proj, proj, proj)


def _gla_kernel(q_ref, k_ref, v_ref, gg_ref, gate_ref, wup_ref, bg_ref, gn_ref, o_ref,
                state_ref, osc_ref, *, rows_blk, chunk, sub):
    state_ref[...] = jnp.zeros_like(state_ref)

    def row_block(r, carry):
        _gla_row_block(pl.multiple_of(r * rows_blk, rows_blk), q_ref, k_ref, v_ref, gg_ref,
                       gate_ref, wup_ref, bg_ref, gn_ref, o_ref, state_ref, osc_ref,
                       rows_blk=rows_blk, chunk=chunk, sub=sub)
        return carry

    lax.fori_loop(0, q_ref.shape[0] // rows_blk, row_block, 0)


def _gla_row_block(base, q_ref, k_ref, v_ref, gg_ref, gate_ref, wup_ref, bg_ref, gn_ref, o_ref,
                   state_ref, osc_ref, *, rows_blk, chunk, sub):
    dk = q_ref.shape[1]
    n_chunks = rows_blk // chunk
    n_sub = chunk // sub
    q_scale = dk ** -0.5
    blk_rows = pl.ds(base, rows_blk)

    z = _dot_f32(gate_ref[blk_rows, :], wup_ref[...]) + bg_ref[...]
    a = _log_sigmoid(z) / GLA_GATE_TAU
    row = lax.broadcasted_iota(jnp.int32, (rows_blk, rows_blk), 0)
    col = lax.broadcasted_iota(jnp.int32, (rows_blk, rows_blk), 1)
    shift = int(math.log2(chunk))
    same = lax.shift_right_logical(row, shift) == lax.shift_right_logical(col, shift)
    tri = ((col <= row) & same).astype(BF16)
    b = _dot_exact_lhs(tri, a)

    key_row = lax.broadcasted_iota(jnp.int32, (chunk, 1), 0)
    key_lane = lax.broadcasted_iota(jnp.int32, (sub, chunk), 1)
    band_row = lax.broadcasted_iota(jnp.int32, (sub, chunk), 0)
    for c in range(n_chunks):
        r0 = c * chunk
        bc = b[r0:r0 + chunk]
        c_rows = pl.ds(pl.multiple_of(base + r0, chunk), chunk)
        qc = q_ref[c_rows, :].astype(F32) * q_scale
        kc = k_ref[c_rows, :].astype(F32)
        vc = v_ref[c_rows, :]
        state = state_ref[...]
        inter = _dot((qc * jnp.exp(bc)).astype(BF16), state.astype(BF16))
        bands = []
        for i in range(n_sub):
            lo, hi = i * sub, (i + 1) * sub
            qs, ks, bs = qc[lo:hi], kc[lo:hi], bc[lo:hi]
            if i == 0:
                s = jnp.zeros((sub, chunk), F32)
            else:
                ref = bc[lo - 1:lo]
                qi = (qs * jnp.exp(bs - ref)).astype(BF16)
                kdec = jnp.exp(jnp.minimum(ref - bc, 0.0))
                ki = jnp.where(key_row < lo, kc * kdec, 0.0).astype(BF16)
                s = _dot_nt(qi, ki)
            for s_ in range(sub):
                dec = jnp.exp(jnp.minimum(bs - bs[s_:s_ + 1], 0.0))
                col = jnp.sum(qs * ks[s_:s_ + 1] * dec, axis=1, keepdims=True)
                s = jnp.where((key_lane == lo + s_) & (band_row >= s_), col, s)
            bands.append(s)
        scores = jnp.concatenate(bands, axis=0)
        osc_ref[r0:r0 + chunk, :] = inter + _dot(scores.astype(BF16), vc)

        bl = bc[chunk - 1:chunk]
        kd = kc * jnp.exp(bl - bc)
        stacked = jnp.concatenate([kd, jnp.broadcast_to(jnp.exp(bl), (chunk, dk))], axis=0)
        st = stacked.T
        vpad = jnp.concatenate([vc, jnp.zeros_like(vc)], axis=0)
        state_ref[...] = state * st[:, chunk:chunk + 1] + _dot(st.astype(BF16), vpad)

    o = osc_ref[...]
    ms = jnp.mean(o * o, axis=-1, keepdims=True)
    y = o * lax.rsqrt(ms + EPS) * gn_ref[...]
    g = gg_ref[blk_rows, :].astype(F32)
    o_ref[blk_rows, :] = (y * (g * _sigmoid(g))).astype(o_ref.dtype)


def _gla(proj, gates, wup_pad, b_gate, g_norm, *, batch, seq, heads, dk, dv, rows_blk=256):
    t = proj.shape[0]
    kb0 = heads
    vb0 = 2 * heads * dk // dv
    gb0 = vb0 + heads
    kern = functools.partial(_gla_kernel, rows_blk=rows_blk, chunk=GLA_CHUNK, sub=GLA_SUB)
    return pl.pallas_call(
        kern,
        grid=(batch, heads),
        in_specs=[pl.BlockSpec((seq, dk), lambda b, h: (b, h)),
                  pl.BlockSpec((seq, dk), lambda b, h: (b, kb0 + h)),
                  pl.BlockSpec((seq, dv), lambda b, h: (b, vb0 + h)),
                  pl.BlockSpec((seq, dv), lambda b, h: (b, gb0 + h)),
                  pl.BlockSpec((seq, gates.shape[1]), lambda b, h: (b, 0)),
                  pl.BlockSpec((wup_pad.shape[0], dk), lambda b, h: (0, h)),
                  pl.BlockSpec((1, dk), lambda b, h: (0, h)),
                  pl.BlockSpec((1, dv), lambda b, h: (0, 0))],
        out_specs=pl.BlockSpec((seq, dv), lambda b, h: (b, h)),
        out_shape=jax.ShapeDtypeStruct((t, heads * dv), BF16),
        scratch_shapes=[pltpu.VMEM((dk, dv), F32),
                        pltpu.VMEM((rows_blk, dv), F32)],
        compiler_params=_params("parallel", "parallel"),
        name="gla",
    )(proj, proj, proj, proj, gates, wup_pad, b_gate, g_norm)


def _rglru_kernel(rx_ref, ry_ref, cw_ref, cb_ref, wx_ref, bx_ref, wa_ref, ba_ref, lam_ref,
                  o_ref, *, rows_t):
    seq, w = rx_ref.shape
    halo = 8
    wxb = wx_ref[...].astype(BF16)
    wab = wa_ref[...].astype(BF16)
    cw = cw_ref[...]
    neg_lam = -lam_ref[...]
    softplus = jnp.maximum(neg_lam, 0.0) + _softplus_neg_abs(neg_lam)
    coef = -RG_C * softplus
    row = lax.broadcasted_iota(jnp.int32, (rows_t, 1), 0)

    def body(t, carry):
        h_prev, tail = carry
        rows = pl.ds(pl.multiple_of(t * rows_t, rows_t), rows_t)
        x = rx_ref[rows, :].astype(F32)
        xe = jnp.concatenate([tail, x], axis=0)
        xc = cb_ref[...] + cw[RG_CONV - 1:RG_CONV] * x
        for s_ in range(1, RG_CONV):
            xc = xc + cw[RG_CONV - 1 - s_:RG_CONV - s_] * pltpu.roll(xe, s_, axis=0)[halo:]
        xcb = xc.astype(BF16)
        gate_x = _sigmoid(_dot(xcb, wxb) + bx_ref[...])
        gate_a = _sigmoid(_dot(xcb, wab) + ba_ref[...])
        log_a = coef * gate_a
        a_dec = jnp.exp(log_a)
        one_m_a2 = -jnp.tanh(log_a) * (a_dec * a_dec + 1.0)
        u = jnp.sqrt(one_m_a2) * (gate_x * xc)
        d = 1
        while d < rows_t:
            if d < halo:
                a_sh = jnp.where(row >= d, pltpu.roll(a_dec, d, axis=0), 1.0)
                u_sh = jnp.where(row >= d, pltpu.roll(u, d, axis=0), 0.0)
            else:
                a_sh = jnp.concatenate([jnp.ones((d, w), F32), a_dec[:rows_t - d]], axis=0)
                u_sh = jnp.concatenate([jnp.zeros((d, w), F32), u[:rows_t - d]], axis=0)
            u = a_dec * u_sh + u
            a_dec = a_dec * a_sh
            d *= 2
        h = u + a_dec * h_prev
        y = ry_ref[rows, :].astype(F32)
        o_ref[rows, :] = (h * _gelu_tanh(y)).astype(o_ref.dtype)
        return h[rows_t - 1:rows_t], x[rows_t - halo:]

    init = (jnp.zeros((1, w), F32), jnp.zeros((halo, w), F32))
    lax.fori_loop(0, seq // rows_t, body, init)


def _rglru(proj, conv_w, conv_b, w_x, b_x, w_a, b_a, lam, *, batch, seq, width, rows_t=256):
    t = proj.shape[0]
    wb = RG_BLOCK
    nb = width // wb
    kern = functools.partial(_rglru_kernel, rows_t=rows_t)
    vec = lambda v: v.reshape(1, width)
    vspec = pl.BlockSpec((1, wb), lambda b, g: (0, g))
    mspec = pl.BlockSpec((None, wb, wb), lambda b, g: (g, 0, 0))
    return pl.pallas_call(
        kern,
        grid=(batch, nb),
        in_specs=[pl.BlockSpec((seq, wb), lambda b, g: (b, nb + g)),
                  pl.BlockSpec((seq, wb), lambda b, g: (b, g)),
                  pl.BlockSpec((RG_CONV, wb), lambda b, g: (0, g)),
                  vspec, mspec, vspec, mspec, vspec, vspec],
        out_specs=pl.BlockSpec((seq, wb), lambda b, g: (b, g)),
        out_shape=jax.ShapeDtypeStruct((t, width), BF16),
        compiler_params=_params("parallel", "parallel"),
        name="rglru",
    )(proj, proj, conv_w, vec(conv_b), w_x, vec(b_x), w_a, vec(b_a), vec(lam))


def _ffn(h, g, w_gate, w_up, w_down, layer):
    u = _rmsnorm(h, g, BF16)
    hid = _matmul_as_swiglu(u, w_gate, w_up, layer, tm=TM_AS, tn=TN_SWIGLU)
    return _matmul_as_res(hid, w_down, layer, h, tm=TM_DOWN, tn=TN_DOWN)


def _even_mixer(h, g, w_in, i, w_gate_up, b_gate, g_norm, b_f, w_out, *, batch, seq):
    d = h.shape[1]
    dk, dv = d // 16, d // 8
    qk, vw, fw = GLA_HEADS * dk, GLA_HEADS * dv, d // 2
    fox_heads = fw // FOX_HEAD_DIM
    rank = GLA_GATE_RANK
    main0 = 2 * qk + 2 * vw
    main1 = 3 * fw
    wt_in = jnp.swapaxes(w_in, 1, 2)
    u, gates = _gate_proj_t(h, g, wt_in, i, col_a=main0, n_a=rank,
                            col_b=main0 + rank + main1, n_b=fox_heads)
    proj = _in_proj_as_t(u, wt_in, i, tm=TM_AS, tn=TN, n_plain=main0 // TN,
                         n_total=(main0 + main1) // TN, shift=rank)
    bias_row = jnp.zeros((1, LANES), F32).at[0, rank:rank + fox_heads].set(b_f)
    c, ct = _fox_cumsum(gates, bias_row, batch=batch, seq=seq, lane0=rank, heads=fox_heads)
    ck = ct.reshape(batch, fox_heads, seq // FOX_TQ, 1, FOX_TQ)
    o_fox = _fox_attention(proj, c, ck, batch=batch, seq=seq, heads=fox_heads, col0=main0,
                           lane0=rank, tq=FOX_TQ)
    wup_pad = jnp.pad(w_gate_up, ((0, LANES - rank), (0, 0)))
    o_gla = _gla(proj, gates, wup_pad, b_gate.reshape(1, qk), g_norm.reshape(1, dv),
                 batch=batch, seq=seq, heads=GLA_HEADS, dk=dk, dv=dv)
    return _matmul_res2(o_gla, o_fox, w_out, i, h, tm=TM_OUT, tn=TN)


def _odd_mixer(h, g, w_in, i, conv_w, conv_b, w_x, b_x, w_a, b_a, lam, w_out, *, batch, seq):
    d = h.shape[1]
    width = d // 2
    sb_heads = width // SB_HEAD_DIM
    u = _rmsnorm(h, g, BF16)
    proj = _matmul_as(u, w_in, i, tm=TM_AS, tn=TN, out_dtype=BF16)
    o_rg = _rglru(proj, conv_w, conv_b, w_x, b_x, w_a, b_a, lam, batch=batch, seq=seq,
                  width=width)
    o_sb = _sb_attention(proj, batch=batch, seq=seq, heads=sb_heads, col0=2 * width,
                         tq=SB_TQ, tk=SB_TK)
    return _matmul_res2(o_rg, o_sb, w_out, i, h, tm=TM_OUT, tn=TN)


def kernel(x, norm_mix, norm_ffn, ffn_w_gate, ffn_w_up, ffn_w_down, ab_w_in, gla_w_gate_up,
           gla_b_gate, gla_norm, fox_b_f, ab_w_out, cd_w_in, rg_conv_w, rg_conv_b, rg_w_x,
           rg_b_x, rg_w_a, rg_b_a, rg_lambda, cd_w_out, final_norm):
    batch, seq, d = x.shape
    h = x.reshape(batch * seq, d)
    for layer in range(norm_mix.shape[0]):
        i = layer // 2
        if layer % 2 == 0:
            h = _even_mixer(h, norm_mix[layer], ab_w_in, i, gla_w_gate_up[i], gla_b_gate[i],
                            gla_norm[i], fox_b_f[i], ab_w_out, batch=batch, seq=seq)
        else:
            h = _odd_mixer(h, norm_mix[layer], cd_w_in, i, rg_conv_w[i], rg_conv_b[i],
                           rg_w_x[i], rg_b_x[i], rg_w_a[i], rg_b_a[i], rg_lambda[i],
                           cd_w_out, batch=batch, seq=seq)
        h = _ffn(h, norm_ffn[layer], ffn_w_gate, ffn_w_up, ffn_w_down, layer)
    return _rmsnorm(h, final_norm, F32).reshape(batch, seq, d)
```

```python
import functools
import math

import jax
import jax.numpy as jnp
from jax import lax
from jax.experimental import pallas as pl
from jax.experimental.pallas import tpu as pltpu

F32 = jnp.float32
BF16 = jnp.bfloat16

EPS = 1e-6
LANES = 128
VMEM_LIMIT = 56 * 1024 * 1024

GLA_HEADS = 4
GLA_GATE_RANK = 16
GLA_GATE_TAU = 16.0
GLA_CHUNK = 64
GLA_SUB = 16
FOX_HEAD_DIM = 128
SB_HEAD_DIM = 128
RG_BLOCK = 256
RG_CONV = 4
RG_C = 8.0

TM_AS = 2048
TN = 512
TN_SWIGLU = 256
TM_DOWN, TN_DOWN = 1024, 256
TM_OUT = 1024
FOX_TQ = 1024
SB_TQ, SB_TK = 1024, 256

NEG_BIG = -1e30
_NT = (((1,), (1,)), ((), ()))


def _params(*sem):
    return pltpu.CompilerParams(dimension_semantics=sem, vmem_limit_bytes=VMEM_LIMIT)


def _dot(a, b):
    return jnp.dot(a, b, preferred_element_type=F32)


def _dot_nt(a, b):
    return lax.dot_general(a, b, _NT, preferred_element_type=F32)


def _split_bf16(x, n):
    parts = []
    r = x
    for _ in range(n):
        p = r.astype(BF16)
        parts.append(p)
        r = r - p.astype(F32)
    return parts


def _dot_f32(a, b):
    ah, al = _split_bf16(a, 2)
    bh, bl = _split_bf16(b, 2)
    return _dot(ah, bh) + _dot(ah, bl) + _dot(al, bh)


def _dot_exact_lhs(m01, x, n=3):
    acc = None
    for p in _split_bf16(x, n):
        t = _dot(m01, p)
        acc = t if acc is None else acc + t
    return acc


def _softplus_neg_abs(z):
    return jnp.log1p(jnp.exp(-jnp.abs(z)))


def _log_sigmoid(z):
    return jnp.minimum(z, 0.0) - _softplus_neg_abs(z)


def _sigmoid(z):
    return 1.0 / (1.0 + jnp.exp(-z))


def _gelu_tanh(y):
    c = math.sqrt(2.0 / math.pi)
    return y * (0.5 * (1.0 + jnp.tanh(c * (y + 0.044715 * (y * y * y)))))


def _rmsnorm_kernel(x_ref, g_ref, o_ref):
    x = x_ref[...]
    ms = jnp.mean(x * x, axis=-1, keepdims=True)
    o_ref[...] = (x * lax.rsqrt(ms + EPS) * g_ref[...]).astype(o_ref.dtype)


def _rmsnorm(x, g, out_dtype, tm=512):
    t, d = x.shape
    return pl.pallas_call(
        _rmsnorm_kernel,
        grid=(t // tm,),
        in_specs=[pl.BlockSpec((tm, d), lambda i: (i, 0)),
                  pl.BlockSpec((1, d), lambda i: (0, 0))],
        out_specs=pl.BlockSpec((tm, d), lambda i: (i, 0)),
        out_shape=jax.ShapeDtypeStruct((t, d), out_dtype),
        compiler_params=_params("parallel"),
        name="rmsnorm",
    )(x, g.reshape(1, d))


def _gate_proj_t_kernel(x_ref, g_ref, wa_ref, wb_ref, u_ref, o_ref):
    x = x_ref[...]
    ms = jnp.mean(x * x, axis=-1, keepdims=True)
    u = x * lax.rsqrt(ms + EPS) * g_ref[...]
    u_ref[...] = u.astype(u_ref.dtype)
    used = wa_ref.shape[0] + wb_ref.shape[0]
    wt = jnp.concatenate([wa_ref[...], wb_ref[...],
                          jnp.zeros((o_ref.shape[1] - used, x.shape[1]), F32)], axis=0)
    uh, ul = _split_bf16(u, 2)
    wh, wl = _split_bf16(wt, 2)
    o_ref[...] = _dot_nt(uh, wh) + _dot_nt(uh, wl) + _dot_nt(ul, wh)


def _gate_proj_t(x, g, wt, layer, *, col_a, n_a, col_b, n_b, tm=256):
    t, d = x.shape
    assert col_a % n_a == 0 and col_b % n_b == 0
    return pl.pallas_call(
        _gate_proj_t_kernel,
        grid=(t // tm,),
        in_specs=[pl.BlockSpec((tm, d), lambda i: (i, 0)),
                  pl.BlockSpec((1, d), lambda i: (0, 0)),
                  pl.BlockSpec((None, n_a, d), lambda i: (layer, col_a // n_a, 0)),
                  pl.BlockSpec((None, n_b, d), lambda i: (layer, col_b // n_b, 0))],
        out_specs=[pl.BlockSpec((tm, d), lambda i: (i, 0)),
                   pl.BlockSpec((tm, LANES), lambda i: (i, 0))],
        out_shape=[jax.ShapeDtypeStruct((t, d), BF16),
                   jax.ShapeDtypeStruct((t, LANES), F32)],
        compiler_params=_params("parallel"),
        name="gate_proj_t",
    )(x, g.reshape(1, d), wt, wt)


def _mm_as_kernel(a_ref, w_ref, o_ref):
    o_ref[...] = _dot(a_ref[...], w_ref[...].astype(BF16)).astype(o_ref.dtype)


def _mm_as_swiglu_kernel(a_ref, wg_ref, wu_ref, o_ref):
    a = a_ref[...]
    g = _dot(a, wg_ref[...].astype(BF16))
    u = _dot(a, wu_ref[...].astype(BF16))
    o_ref[...] = (g * _sigmoid(g) * u).astype(o_ref.dtype)


def _mm_as_res_kernel(a_ref, w_ref, r_ref, o_ref):
    o_ref[...] = r_ref[...] + _dot(a_ref[...], w_ref[...].astype(BF16))


def _a_spec(tm, k):
    return pl.BlockSpec((tm, k), lambda i, j: (i, 0), pipeline_mode=pl.Buffered(1))


def _wt_spec(layer, k, tn):
    return pl.BlockSpec((None, k, tn), lambda i, j: (layer, 0, j))


def _matmul_as(a, w, layer, *, tm, tn, out_dtype):
    m, k = a.shape
    n = w.shape[2]
    return pl.pallas_call(
        _mm_as_kernel,
        grid=(m // tm, n // tn),
        in_specs=[_a_spec(tm, k), _wt_spec(layer, k, tn)],
        out_specs=pl.BlockSpec((tm, tn), lambda i, j: (i, j)),
        out_shape=jax.ShapeDtypeStruct((m, n), out_dtype),
        compiler_params=_params("parallel", "arbitrary"),
        name="matmul_as",
    )(a, w)


def _matmul_as_swiglu(a, wg, wu, layer, *, tm, tn):
    m, k = a.shape
    n = wg.shape[2]
    return pl.pallas_call(
        _mm_as_swiglu_kernel,
        grid=(m // tm, n // tn),
        in_specs=[_a_spec(tm, k), _wt_spec(layer, k, tn), _wt_spec(layer, k, tn)],
        out_specs=pl.BlockSpec((tm, tn), lambda i, j: (i, j)),
        out_shape=jax.ShapeDtypeStruct((m, n), BF16),
        compiler_params=_params("parallel", "arbitrary"),
        name="matmul_as_swiglu",
    )(a, wg, wu)


def _matmul_as_res(a, w, layer, res, *, tm, tn):
    m, k = a.shape
    n = w.shape[2]
    return pl.pallas_call(
        _mm_as_res_kernel,
        grid=(m // tm, n // tn),
        in_specs=[_a_spec(tm, k), _wt_spec(layer, k, tn),
                  pl.BlockSpec((tm, tn), lambda i, j: (i, j))],
        out_specs=pl.BlockSpec((tm, tn), lambda i, j: (i, j)),
        out_shape=jax.ShapeDtypeStruct((m, n), F32),
        compiler_params=_params("parallel", "arbitrary"),
        name="matmul_as_res",
    )(a, w, res)


def _in_proj_as_t_kernel(a_ref, wt_ref, wtx_ref, o_ref, *, n_plain):
    j = pl.program_id(1)
    shift = wtx_ref.shape[0]

    @pl.when(j < n_plain)
    def _():
        o_ref[...] = _dot_nt(a_ref[...], wt_ref[...].astype(BF16)).astype(o_ref.dtype)

    @pl.when(j >= n_plain)
    def _():
        wt = jnp.concatenate([wt_ref[shift:, :], wtx_ref[...]], axis=0).astype(BF16)
        o_ref[...] = _dot_nt(a_ref[...], wt).astype(o_ref.dtype)


def _in_proj_as_t(a, wt, layer, *, tm, tn, n_plain, n_total, shift):
    m, k = a.shape
    assert tn % shift == 0
    kern = functools.partial(_in_proj_as_t_kernel, n_plain=n_plain)
    return pl.pallas_call(
        kern,
        grid=(m // tm, n_total),
        in_specs=[_a_spec(tm, k),
                  pl.BlockSpec((None, tn, k), lambda i, j: (layer, j, 0)),
                  pl.BlockSpec((None, shift, k), lambda i, j: (layer, (j + 1) * (tn // shift), 0))],
        out_specs=pl.BlockSpec((tm, tn), lambda i, j: (i, j)),
        out_shape=jax.ShapeDtypeStruct((m, n_total * tn), BF16),
        compiler_params=_params("parallel", "arbitrary"),
        name="in_proj_as_t",
    )(a, wt, wt)


_CAST_ROWS = 512


def _cast_rows(w_ref, wb_ref):
    k = w_ref.shape[0]

    def body(r, c):
        rows = pl.ds(pl.multiple_of(r * _CAST_ROWS, _CAST_ROWS), _CAST_ROWS)
        wb_ref[rows, :] = w_ref[rows, :].astype(BF16)
        return c

    lax.fori_loop(0, k // _CAST_ROWS, body, 0)


def _mm_res2_kernel(a1_ref, a2_ref, w_ref, r_ref, o_ref, wb_ref):
    @pl.when(pl.program_id(1) == 0)
    def _():
        _cast_rows(w_ref, wb_ref)

    k1 = a1_ref.shape[1]
    o_ref[...] = (r_ref[...] + _dot(a1_ref[...], wb_ref[:k1, :])
                  + _dot(a2_ref[...], wb_ref[k1:, :]))


def _matmul_res2(a1, a2, w, layer, res, *, tm, tn):
    m, k1 = a1.shape
    k2 = a2.shape[1]
    k, n = w.shape[1:]
    assert k == k1 + k2
    return pl.pallas_call(
        _mm_res2_kernel,
        grid=(n // tn, m // tm),
        in_specs=[pl.BlockSpec((tm, k1), lambda j, i: (i, 0)),
                  pl.BlockSpec((tm, k2), lambda j, i: (i, 0)),
                  pl.BlockSpec((None, k, tn), lambda j, i: (layer, 0, j)),
                  pl.BlockSpec((tm, tn), lambda j, i: (i, j))],
        out_specs=pl.BlockSpec((tm, tn), lambda j, i: (i, j)),
        out_shape=jax.ShapeDtypeStruct((m, n), F32),
        scratch_shapes=[pltpu.VMEM((k, tn), BF16)],
        compiler_params=_params("arbitrary", "arbitrary"),
        name="matmul_res2",
    )(a1, a2, w, res)


def _fox_cumsum_kernel(g_ref, b_ref, c_ref, ct_ref, c_sc, *, blk, lane0, heads):
    s = g_ref.shape[0]
    row = lax.broadcasted_iota(jnp.int32, (blk, blk), 0)
    col = lax.broadcasted_iota(jnp.int32, (blk, blk), 1)
    tri = (col <= row).astype(BF16)
    carry = jnp.zeros((1, g_ref.shape[1]), F32)
    for n in range(s // blk):
        rows = slice(n * blk, (n + 1) * blk)
        lf = _log_sigmoid(g_ref[rows, :] + b_ref[...])
        c = _dot_exact_lhs(tri, lf) + carry
        c_sc[rows, :] = c
        carry = c[blk - 1:blk, :]
    c_all = c_sc[...]
    c_ref[...] = c_all
    ct_ref[...] = c_all.T[lane0:lane0 + heads, :]


def _fox_cumsum(gates, bias_row, *, batch, seq, lane0, heads, blk=256):
    t, n = gates.shape
    kern = functools.partial(_fox_cumsum_kernel, blk=blk, lane0=lane0, heads=heads)
    return pl.pallas_call(
        kern,
        grid=(batch,),
        in_specs=[pl.BlockSpec((seq, n), lambda b: (b, 0)),
                  pl.BlockSpec((1, n), lambda b: (0, 0))],
        out_specs=[pl.BlockSpec((seq, n), lambda b: (b, 0)),
                   pl.BlockSpec((None, heads, seq), lambda b: (b, 0, 0))],
        out_shape=[jax.ShapeDtypeStruct((t, n), F32),
                   jax.ShapeDtypeStruct((batch, heads, seq), F32)],
        scratch_shapes=[pltpu.VMEM((seq, n), F32)],
        compiler_params=_params("parallel"),
        name="fox_cumsum",
    )(gates, bias_row)


def _fox_kernel(q_ref, k_ref, v_ref, c_ref, ck_ref, o_ref, *, tq, lane0):
    h = pl.program_id(1)
    nq = q_ref.shape[0] // tq

    def q_block(qi, carry):
        _fox_q_block(qi, h, q_ref, k_ref, v_ref, c_ref, ck_ref, o_ref, tq=tq, lane0=lane0)
        return carry

    lax.fori_loop(0, nq, q_block, 0)


def _fox_q_block(qi, h, q_ref, k_ref, v_ref, c_ref, ck_ref, o_ref, *, tq, lane0):
    dh = q_ref.shape[1]
    scale = dh ** -0.5
    q_rows = pl.ds(pl.multiple_of(qi * tq, tq), tq)
    q = q_ref[q_rows, :]
    c_blk = c_ref[q_rows, :]
    lane = lax.broadcasted_iota(jnp.int32, c_blk.shape, 1)
    cq = jnp.sum(jnp.where(lane == lane0 + h, c_blk, 0.0), axis=1, keepdims=True)

    def logits(kb):
        rows = pl.ds(pl.multiple_of(kb * tq, tq), tq)
        s = _dot_nt(q, k_ref[rows, :]) * scale
        return s + cq - ck_ref[kb], v_ref[rows, :]

    s, v = logits(qi)
    tpos = lax.broadcasted_iota(jnp.int32, (tq, tq), 0)
    spos = lax.broadcasted_iota(jnp.int32, (tq, tq), 1)
    s = jnp.where(spos <= tpos, s, NEG_BIG)
    m0 = jnp.max(s, axis=1, keepdims=True)
    p = jnp.exp(s - m0)
    l0 = jnp.sum(p, axis=1, keepdims=True)
    acc0 = _dot(p.astype(BF16), v)

    def body(kb, carry):
        m, l, acc = carry
        s, v = logits(kb)
        m_new = jnp.maximum(m, jnp.max(s, axis=1, keepdims=True))
        alpha = jnp.exp(m - m_new)
        p = jnp.exp(s - m_new)
        l = alpha * l + jnp.sum(p, axis=1, keepdims=True)
        acc = alpha * acc + _dot(p.astype(BF16), v)
        return m_new, l, acc

    _, l, acc = lax.fori_loop(0, qi, body, (m0, l0, acc0))
    o_ref[q_rows, :] = (acc / l).astype(o_ref.dtype)


def _fox_attention(proj, c, ck, *, batch, seq, heads, col0, lane0, tq):
    t = proj.shape[0]
    dh = FOX_HEAD_DIM
    nq = seq // tq
    qb = col0 // dh
    kern = functools.partial(_fox_kernel, tq=tq, lane0=lane0)
    return pl.pallas_call(
        kern,
        grid=(batch, heads),
        in_specs=[pl.BlockSpec((seq, dh), lambda b, h: (b, qb + h)),
                  pl.BlockSpec((seq, dh), lambda b, h: (b, qb + heads + h)),
                  pl.BlockSpec((seq, dh), lambda b, h: (b, qb + 2 * heads + h)),
                  pl.BlockSpec((seq, c.shape[1]), lambda b, h: (b, 0)),
                  pl.BlockSpec((None, None, nq, 1, tq), lambda b, h: (b, h, 0, 0, 0))],
        out_specs=pl.BlockSpec((seq, dh), lambda b, h: (b, h)),
        out_shape=jax.ShapeDtypeStruct((t, heads * dh), BF16),
        compiler_params=_params("parallel", "parallel"),
        name="fox_attention",
    )(proj, proj, proj, c, ck)


def _sb_kernel(q_ref, k_ref, v_ref, o_ref, *, tq, tk):
    def q_block(qi, carry):
        _sb_q_block(qi, q_ref, k_ref, v_ref, o_ref, tq=tq, tk=tk)
        return carry

    lax.fori_loop(0, q_ref.shape[0] // tq, q_block, 0)


def _sb_q_block(qi, q_ref, k_ref, v_ref, o_ref, *, tq, tk):
    dh = q_ref.shape[1]
    scale = dh ** -0.5
    q_rows = pl.ds(pl.multiple_of(qi * tq, tq), tq)
    q = q_ref[q_rows, :]
    nd = tq // tk
    jj = lax.broadcasted_iota(jnp.int32, (tk, tk), 0)
    ss = lax.broadcasted_iota(jnp.int32, (tk, tk), 1)
    upper = (jj > ss).astype(BF16)
    row = lax.broadcasted_iota(jnp.int32, (tq, tk), 0)
    col = lax.broadcasted_iota(jnp.int32, (tq, tk), 1)

    def block(kb, run, mask):
        rows = pl.ds(pl.multiple_of(kb * tk, tk), tk)
        z = _dot_nt(q, k_ref[rows, :]) * scale
        sp = jnp.log(1.0 + jnp.exp(-jnp.abs(z)))
        log_beta = jnp.minimum(z, 0.0) - sp
        log_1m = log_beta - z
        if mask is not None:
            log_1m = jnp.where(mask, log_1m, 0.0)
        hi, lo = _split_bf16(log_1m, 2)
        suffix = _dot(hi, upper) + _dot(lo, upper)
        a = jnp.exp(log_beta + suffix + run)
        if mask is not None:
            a = jnp.where(mask, a, 0.0)
        pv = _dot(a.astype(BF16), v_ref[rows, :])
        return pv, jnp.sum(log_1m, axis=1, keepdims=True)

    run = jnp.zeros((tq, 1), F32)
    acc = jnp.zeros((tq, dh), F32)
    for d in range(nd):
        off = (nd - 1 - d) * tk
        pv, rs = block((qi + 1) * nd - 1 - d, run, (off + col) < row)
        acc = acc + pv
        run = run + rs

    def body(p, carry):
        run, acc = carry
        for d in range(nd):
            pv, rs = block((qi - p) * nd - 1 - d, run, None)
            acc = acc + pv
            run = run + rs
        return run, acc

    _, acc = lax.fori_loop(0, qi, body, (run, acc))
    o_ref[q_rows, :] = acc.astype(o_ref.dtype)


def _sb_attention(proj, *, batch, seq, heads, col0, tq, tk):
    t = proj.shape[0]
    dh = SB_HEAD_DIM
    qb = col0 // dh
    kern = functools.partial(_sb_kernel, tq=tq, tk=tk)
    return pl.pallas_call(
        kern,
        grid=(batch, heads),
        in_specs=[pl.BlockSpec((seq, dh), lambda b, h: (b, qb + h)),
                  pl.BlockSpec((seq, dh), lambda b, h: (b, qb + heads + h)),
                  pl.BlockSpec((seq, dh), lambda b, h: (b, qb + 2 * heads + h))],
        out_specs=pl.BlockSpec((seq, dh), lambda b, h: (b, h)),
        out_shape=jax.ShapeDtypeStruct((t, heads * dh), BF16),
        compiler_params=_params("parallel", "parallel"),
        name="sb_attention",
    )(proj, proj, proj)


def _gla_kernel(q_ref, k_ref, v_ref, gg_ref, gate_ref, wup_ref, bg_ref, gn_ref, o_ref,
                state_ref, osc_ref, *, rows_blk, chunk, sub):
    state_ref[...] = jnp.zeros_like(state_ref)

    def row_block(r, carry):
        _gla_row_block(pl.multiple_of(r * rows_blk, rows_blk), q_ref, k_ref, v_ref, gg_ref,
                       gate_ref, wup_ref, bg_ref, gn_ref, o_ref, state_ref, osc_ref,
                       rows_blk=rows_blk, chunk=chunk, sub=sub)
        return carry

    lax.fori_loop(0, q_ref.shape[0] // rows_blk, row_block, 0)


def _gla_row_block(base, q_ref, k_ref, v_ref, gg_ref, gate_ref, wup_ref, bg_ref, gn_ref, o_ref,
                   state_ref, osc_ref, *, rows_blk, chunk, sub):
    dk = q_ref.shape[1]
    n_chunks = rows_blk // chunk
    n_sub = chunk // sub
    q_scale = dk ** -0.5
    blk_rows = pl.ds(base, rows_blk)

    z = _dot_f32(gate_ref[blk_rows, :], wup_ref[...]) + bg_ref[...]
    a = _log_sigmoid(z) / GLA_GATE_TAU
    row = lax.broadcasted_iota(jnp.int32, (rows_blk, rows_blk), 0)
    col = lax.broadcasted_iota(jnp.int32, (rows_blk, rows_blk), 1)
    shift = int(math.log2(chunk))
    same = lax.shift_right_logical(row, shift) == lax.shift_right_logical(col, shift)
    tri = ((col <= row) & same).astype(BF16)
    b = _dot_exact_lhs(tri, a)

    key_row = lax.broadcasted_iota(jnp.int32, (chunk, 1), 0)
    key_lane = lax.broadcasted_iota(jnp.int32, (sub, chunk), 1)
    band_row = lax.broadcasted_iota(jnp.int32, (sub, chunk), 0)
    for c in range(n_chunks):
        r0 = c * chunk
        bc = b[r0:r0 + chunk]
        c_rows = pl.ds(pl.multiple_of(base + r0, chunk), chunk)
        qc = q_ref[c_rows, :].astype(F32) * q_scale
        kc = k_ref[c_rows, :].astype(F32)
        vc = v_ref[c_rows, :]
        state = state_ref[...]
        inter = _dot((qc * jnp.exp(bc)).astype(BF16), state.astype(BF16))
        bands = []
        for i in range(n_sub):
            lo, hi = i * sub, (i + 1) * sub
            qs, ks, bs = qc[lo:hi], kc[lo:hi], bc[lo:hi]
            if i == 0:
                s = jnp.zeros((sub, chunk), F32)
            else:
                ref = bc[lo - 1:lo]
                qi = (qs * jnp.exp(bs - ref)).astype(BF16)
                kdec = jnp.exp(jnp.minimum(ref - bc, 0.0))
                ki = jnp.where(key_row < lo, kc * kdec, 0.0).astype(BF16)
                s = _dot_nt(qi, ki)
            for s_ in range(sub):
                dec = jnp.exp(jnp.minimum(bs - bs[s_:s_ + 1], 0.0))
                col = jnp.sum(qs * ks[s_:s_ + 1] * dec, axis=1, keepdims=True)
                s = jnp.where((key_lane == lo + s_) & (band_row >= s_), col, s)
            bands.append(s)
        scores = jnp.concatenate(bands, axis=0)
        osc_ref[r0:r0 + chunk, :] = inter + _dot(scores.astype(BF16), vc)

        bl = bc[chunk - 1:chunk]
        kd = kc * jnp.exp(bl - bc)
        stacked = jnp.concatenate([kd, jnp.broadcast_to(jnp.exp(bl), (chunk, dk))], axis=0)
        st = stacked.T
        vpad = jnp.concatenate([vc, jnp.zeros_like(vc)], axis=0)
        state_ref[...] = state * st[:, chunk:chunk + 1] + _dot(st.astype(BF16), vpad)

    o = osc_ref[...]
    ms = jnp.mean(o * o, axis=-1, keepdims=True)
    y = o * lax.rsqrt(ms + EPS) * gn_ref[...]
    g = gg_ref[blk_rows, :].astype(F32)
    o_ref[blk_rows, :] = (y * (g * _sigmoid(g))).astype(o_ref.dtype)


def _gla(proj, gates, wup_pad, b_gate, g_norm, *, batch, seq, heads, dk, dv, rows_blk=256):
    t = proj.shape[0]
    kb0 = heads
    vb0 = 2 * heads * dk // dv
    gb0 = vb0 + heads
    kern = functools.partial(_gla_kernel, rows_blk=rows_blk, chunk=GLA_CHUNK, sub=GLA_SUB)
    return pl.pallas_call(
        kern,
        grid=(batch, heads),
        in_specs=[pl.BlockSpec((seq, dk), lambda b, h: (b, h)),
                  pl.BlockSpec((seq, dk), lambda b, h: (b, kb0 + h)),
                  pl.BlockSpec((seq, dv), lambda b, h: (b, vb0 + h)),
                  pl.BlockSpec((seq, dv), lambda b, h: (b, gb0 + h)),
                  pl.BlockSpec((seq, gates.shape[1]), lambda b, h: (b, 0)),
                  pl.BlockSpec((wup_pad.shape[0], dk), lambda b, h: (0, h)),
                  pl.BlockSpec((1, dk), lambda b, h: (0, h)),
                  pl.BlockSpec((1, dv), lambda b, h: (0, 0))],
        out_specs=pl.BlockSpec((seq, dv), lambda b, h: (b, h)),
        out_shape=jax.ShapeDtypeStruct((t, heads * dv), BF16),
        scratch_shapes=[pltpu.VMEM((dk, dv), F32),
                        pltpu.VMEM((rows_blk, dv), F32)],
        compiler_params=_params("parallel", "parallel"),
        name="gla",
    )(proj, proj, proj, proj, gates, wup_pad, b_gate, g_norm)


def _rglru_kernel(rx_ref, ry_ref, cw_ref, cb_ref, wx_ref, bx_ref, wa_ref, ba_ref, lam_ref,
                  o_ref, *, rows_t):
    seq, w = rx_ref.shape
    halo = 8
    wxb = wx_ref[...].astype(BF16)
    wab = wa_ref[...].astype(BF16)
    cw = cw_ref[...]
    neg_lam = -lam_ref[...]
    softplus = jnp.maximum(neg_lam, 0.0) + _softplus_neg_abs(neg_lam)
    coef = -RG_C * softplus
    row = lax.broadcasted_iota(jnp.int32, (rows_t, 1), 0)

    def body(t, carry):
        h_prev, tail = carry
        rows = pl.ds(pl.multiple_of(t * rows_t, rows_t), rows_t)
        x = rx_ref[rows, :].astype(F32)
        xe = jnp.concatenate([tail, x], axis=0)
        xc = cb_ref[...] + cw[RG_CONV - 1:RG_CONV] * x
        for s_ in range(1, RG_CONV):
            xc = xc + cw[RG_CONV - 1 - s_:RG_CONV - s_] * pltpu.roll(xe, s_, axis=0)[halo:]
        xcb = xc.astype(BF16)
        gate_x = _sigmoid(_dot(xcb, wxb) + bx_ref[...])
        gate_a = _sigmoid(_dot(xcb, wab) + ba_ref[...])
        log_a = coef * gate_a
        a_dec = jnp.exp(log_a)
        one_m_a2 = -jnp.tanh(log_a) * (a_dec * a_dec + 1.0)
        u = jnp.sqrt(one_m_a2) * (gate_x * xc)
        d = 1
        while d < rows_t:
            if d < halo:
                a_sh = jnp.where(row >= d, pltpu.roll(a_dec, d, axis=0), 1.0)
                u_sh = jnp.where(row >= d, pltpu.roll(u, d, axis=0), 0.0)
            else:
                a_sh = jnp.concatenate([jnp.ones((d, w), F32), a_dec[:rows_t - d]], axis=0)
                u_sh = jnp.concatenate([jnp.zeros((d, w), F32), u[:rows_t - d]], axis=0)
            u = a_dec * u_sh + u
            a_dec = a_dec * a_sh
            d *= 2
        h = u + a_dec * h_prev
        y = ry_ref[rows, :].astype(F32)
        o_ref[rows, :] = (h * _gelu_tanh(y)).astype(o_ref.dtype)
        return h[rows_t - 1:rows_t], x[rows_t - halo:]

    init = (jnp.zeros((1, w), F32), jnp.zeros((halo, w), F32))
    lax.fori_loop(0, seq // rows_t, body, init)


def _rglru(proj, conv_w, conv_b, w_x, b_x, w_a, b_a, lam, *, batch, seq, width, rows_t=256):
    t = proj.shape[0]
    wb = RG_BLOCK
    nb = width // wb
    kern = functools.partial(_rglru_kernel, rows_t=rows_t)
    vec = lambda v: v.reshape(1, width)
    vspec = pl.BlockSpec((1, wb), lambda b, g: (0, g))
    mspec = pl.BlockSpec((None, wb, wb), lambda b, g: (g, 0, 0))
    return pl.pallas_call(
        kern,
        grid=(batch, nb),
        in_specs=[pl.BlockSpec((seq, wb), lambda b, g: (b, nb + g)),
                  pl.BlockSpec((seq, wb), lambda b, g: (b, g)),
                  pl.BlockSpec((RG_CONV, wb), lambda b, g: (0, g)),
                  vspec, mspec, vspec, mspec, vspec, vspec],
        out_specs=pl.BlockSpec((seq, wb), lambda b, g: (b, g)),
        out_shape=jax.ShapeDtypeStruct((t, width), BF16),
        compiler_params=_params("parallel", "parallel"),
        name="rglru",
    )(proj, proj, conv_w, vec(conv_b), w_x, vec(b_x), w_a, vec(b_a), vec(lam))


def _ffn(h, g, w_gate, w_up, w_down, layer):
    u = _rmsnorm(h, g, BF16)
    hid = _matmul_as_swiglu(u, w_gate, w_up, layer, tm=TM_AS, tn=TN_SWIGLU)
    return _matmul_as_res(hid, w_down, layer, h, tm=TM_DOWN, tn=TN_DOWN)


def _even_mixer(h, g, w_in, i, w_gate_up, b_gate, g_norm, b_f, w_out, *, batch, seq):
    d = h.shape[1]
    dk, dv = d // 16, d // 8
    qk, vw, fw = GLA_HEADS * dk, GLA_HEADS * dv, d // 2
    fox_heads = fw // FOX_HEAD_DIM
    rank = GLA_GATE_RANK
    main0 = 2 * qk + 2 * vw
    main1 = 3 * fw
    wt_in = jnp.swapaxes(w_in, 1, 2)
    u, gates = _gate_proj_t(h, g, wt_in, i, col_a=main0, n_a=rank,
                            col_b=main0 + rank + main1, n_b=fox_heads)
    proj = _in_proj_as_t(u, wt_in, i, tm=TM_AS, tn=TN, n_plain=main0 // TN,
                         n_total=(main0 + main1) // TN, shift=rank)
    bias_row = jnp.zeros((1, LANES), F32).at[0, rank:rank + fox_heads].set(b_f)
    c, ct = _fox_cumsum(gates, bias_row, batch=batch, seq=seq, lane0=rank, heads=fox_heads)
    ck = ct.reshape(batch, fox_heads, seq // FOX_TQ, 1, FOX_TQ)
    o_fox = _fox_attention(proj, c, ck, batch=batch, seq=seq, heads=fox_heads, col0=main0,
                           lane0=rank, tq=FOX_TQ)
    wup_pad = jnp.pad(w_gate_up, ((0, LANES - rank), (0, 0)))
    o_gla = _gla(proj, gates, wup_pad, b_gate.reshape(1, qk), g_norm.reshape(1, dv),
                 batch=batch, seq=seq, heads=GLA_HEADS, dk=dk, dv=dv)
    return _matmul_res2(o_gla, o_fox, w_out, i, h, tm=TM_OUT, tn=TN)


def _odd_mixer(h, g, w_in, i, conv_w, conv_b, w_x, b_x, w_a, b_a, lam, w_out, *, batch, seq):
    d = h.shape[1]
    width = d // 2
    sb_heads = width // SB_HEAD_DIM
    u = _rmsnorm(h, g, BF16)
    proj = _matmul_as(u, w_in, i, tm=TM_AS, tn=TN, out_dtype=BF16)
    o_rg = _rglru(proj, conv_w, conv_b, w_x, b_x, w_a, b_a, lam, batch=batch, seq=seq,
                  width=width)
    o_sb = _sb_attention(proj, batch=batch, seq=seq, heads=sb_heads, col0=2 * width,
                         tq=SB_TQ, tk=SB_TK)
    return _matmul_res2(o_rg, o_sb, w_out, i, h, tm=TM_OUT, tn=TN)


def kernel(x, norm_mix, norm_ffn, ffn_w_gate, ffn_w_up, ffn_w_down, ab_w_in, gla_w_gate_up,
           gla_b_gate, gla_norm, fox_b_f, ab_w_out, cd_w_in, rg_conv_w, rg_conv_b, rg_w_x,
           rg_b_x, rg_w_a, rg_b_a, rg_lambda, cd_w_out, final_norm):
    batch, seq, d = x.shape
    h = x.reshape(batch * seq, d)
    for layer in range(norm_mix.shape[0]):
        i = layer // 2
        if layer % 2 == 0:
            h = _even_mixer(h, norm_mix[layer], ab_w_in, i, gla_w_gate_up[i], gla_b_gate[i],
                            gla_norm[i], fox_b_f[i], ab_w_out, batch=batch, seq=seq)
        else:
            h = _odd_mixer(h, norm_mix[layer], cd_w_in, i, rg_conv_w[i], rg_conv_b[i],
                           rg_w_x[i], rg_b_x[i], rg_w_a[i], rg_b_a[i], rg_lambda[i],
                           cd_w_out, batch=batch, seq=seq)
        h = _ffn(h, norm_ffn[layer], ffn_w_gate, ffn_w_up, ffn_w_down, layer)
    return _rmsnorm(h, final_norm, F32).reshape(batch, seq, d)
```

```python
import functools
import math

import jax
import jax.numpy as jnp
from jax import lax
from jax.experimental import pallas as pl
from jax.experimental.pallas import tpu as pltpu

F32 = jnp.float32
BF16 = jnp.bfloat16

EPS = 1e-6
LANES = 128
VMEM_LIMIT = 56 * 1024 * 1024

GLA_HEADS = 4
GLA_GATE_RANK = 16
GLA_GATE_TAU = 16.0
GLA_CHUNK = 64
GLA_SUB = 16
FOX_HEAD_DIM = 128
SB_HEAD_DIM = 128
RG_BLOCK = 256
RG_CONV = 4
RG_C = 8.0

TM_AS = 2048
TN = 512
TN_SWIGLU = 256
TM_DOWN, TN_DOWN = 1024, 256
TM_OUT = 1024
FOX_TQ = 1024
SB_TQ, SB_TK = 1024, 256

NEG_BIG = -1e30
_NT = (((1,), (1,)), ((), ()))


def _params(*sem):
    return pltpu.CompilerParams(dimension_semantics=sem, vmem_limit_bytes=VMEM_LIMIT)


def _dot(a, b):
    return jnp.dot(a, b, preferred_element_type=F32)


def _dot_nt(a, b):
    return lax.dot_general(a, b, _NT, preferred_element_type=F32)


def _split_bf16(x, n):
    parts = []
    r = x
    for _ in range(n):
        p = r.astype(BF16)
        parts.append(p)
        r = r - p.astype(F32)
    return parts


def _dot_f32(a, b):
    ah, al = _split_bf16(a, 2)
    bh, bl = _split_bf16(b, 2)
    return _dot(ah, bh) + _dot(ah, bl) + _dot(al, bh)


def _dot_exact_lhs(m01, x, n=3):
    acc = None
    for p in _split_bf16(x, n):
        t = _dot(m01, p)
        acc = t if acc is None else acc + t
    return acc


def _softplus_neg_abs(z):
    return jnp.log1p(jnp.exp(-jnp.abs(z)))


def _log_sigmoid(z):
    return jnp.minimum(z, 0.0) - _softplus_neg_abs(z)


def _sigmoid(z):
    return 1.0 / (1.0 + jnp.exp(-z))


def _gelu_tanh(y):
    c = math.sqrt(2.0 / math.pi)
    return y * (0.5 * (1.0 + jnp.tanh(c * (y + 0.044715 * (y * y * y)))))


def _rmsnorm_kernel(x_ref, g_ref, o_ref):
    x = x_ref[...]
    ms = jnp.mean(x * x, axis=-1, keepdims=True)
    o_ref[...] = (x * lax.rsqrt(ms + EPS) * g_ref[...]).astype(o_ref.dtype)


def _rmsnorm(x, g, out_dtype, tm=512):
    t, d = x.shape
    return pl.pallas_call(
        _rmsnorm_kernel,
        grid=(t // tm,),
        in_specs=[pl.BlockSpec((tm, d), lambda i: (i, 0)),
                  pl.BlockSpec((1, d), lambda i: (0, 0))],
        out_specs=pl.BlockSpec((tm, d), lambda i: (i, 0)),
        out_shape=jax.ShapeDtypeStruct((t, d), out_dtype),
        compiler_params=_params("parallel"),
        name="rmsnorm",
    )(x, g.reshape(1, d))


def _gate_proj_t_kernel(x_ref, g_ref, wa_ref, wb_ref, u_ref, o_ref):
    x = x_ref[...]
    ms = jnp.mean(x * x, axis=-1, keepdims=True)
    u = x * lax.rsqrt(ms + EPS) * g_ref[...]
    u_ref[...] = u.astype(u_ref.dtype)
    used = wa_ref.shape[0] + wb_ref.shape[0]
    wt = jnp.concatenate([wa_ref[...], wb_ref[...],
                          jnp.zeros((o_ref.shape[1] - used, x.shape[1]), F32)], axis=0)
    uh, ul = _split_bf16(u, 2)
    wh, wl = _split_bf16(wt, 2)
    o_ref[...] = _dot_nt(uh, wh) + _dot_nt(uh, wl) + _dot_nt(ul, wh)


def _gate_proj_t(x, g, wt, layer, *, col_a, n_a, col_b, n_b, tm=256):
    t, d = x.shape
    assert col_a % n_a == 0 and col_b % n_b == 0
    return pl.pallas_call(
        _gate_proj_t_kernel,
        grid=(t // tm,),
        in_specs=[pl.BlockSpec((tm, d), lambda i: (i, 0)),
                  pl.BlockSpec((1, d), lambda i: (0, 0)),
                  pl.BlockSpec((None, n_a, d), lambda i: (layer, col_a // n_a, 0)),
                  pl.BlockSpec((None, n_b, d), lambda i: (layer, col_b // n_b, 0))],
        out_specs=[pl.BlockSpec((tm, d), lambda i: (i, 0)),
                   pl.BlockSpec((tm, LANES), lambda i: (i, 0))],
        out_shape=[jax.ShapeDtypeStruct((t, d), BF16),
                   jax.ShapeDtypeStruct((t, LANES), F32)],
        compiler_params=_params("parallel"),
        name="gate_proj_t",
    )(x, g.reshape(1, d), wt, wt)


def _mm_as_kernel(a_ref, w_ref, o_ref):
    o_ref[...] = _dot(a_ref[...], w_ref[...].astype(BF16)).astype(o_ref.dtype)


def _mm_as_swiglu_kernel(a_ref, wg_ref, wu_ref, o_ref):
    a = a_ref[...]
    g = _dot(a, wg_ref[...].astype(BF16))
    u = _dot(a, wu_ref[...].astype(BF16))
    o_ref[...] = (g * _sigmoid(g) * u).astype(o_ref.dtype)


def _mm_as_res_kernel(a_ref, w_ref, r_ref, o_ref):
    o_ref[...] = r_ref[...] + _dot(a_ref[...], w_ref[...].astype(BF16))


def _a_spec(tm, k):
    return pl.BlockSpec((tm, k), lambda i, j: (i, 0), pipeline_mode=pl.Buffered(1))


def _wt_spec(layer, k, tn):
    return pl.BlockSpec((None, k, tn), lambda i, j: (layer, 0, j))


def _matmul_as(a, w, layer, *, tm, tn, out_dtype):
    m, k = a.shape
    n = w.shape[2]
    return pl.pallas_call(
        _mm_as_kernel,
        grid=(m // tm, n // tn),
        in_specs=[_a_spec(tm, k), _wt_spec(layer, k, tn)],
        out_specs=pl.BlockSpec((tm, tn), lambda i, j: (i, j)),
        out_shape=jax.ShapeDtypeStruct((m, n), out_dtype),
        compiler_params=_params("parallel", "arbitrary"),
        name="matmul_as",
    )(a, w)


def _matmul_as_swiglu(a, wg, wu, layer, *, tm, tn):
    m, k = a.shape
    n = wg.shape[2]
    return pl.pallas_call(
        _mm_as_swiglu_kernel,
        grid=(m // tm, n // tn),
        in_specs=[_a_spec(tm, k), _wt_spec(layer, k, tn), _wt_spec(layer, k, tn)],
        out_specs=pl.BlockSpec((tm, tn), lambda i, j: (i, j)),
        out_shape=jax.ShapeDtypeStruct((m, n), BF16),
        compiler_params=_params("parallel", "arbitrary"),
        name="matmul_as_swiglu",
    )(a, wg, wu)


def _matmul_as_res(a, w, layer, res, *, tm, tn):
    m, k = a.shape
    n = w.shape[2]
    return pl.pallas_call(
        _mm_as_res_kernel,
        grid=(m // tm, n // tn),
        in_specs=[_a_spec(tm, k), _wt_spec(layer, k, tn),
                  pl.BlockSpec((tm, tn), lambda i, j: (i, j))],
        out_specs=pl.BlockSpec((tm, tn), lambda i, j: (i, j)),
        out_shape=jax.ShapeDtypeStruct((m, n), F32),
        compiler_params=_params("parallel", "arbitrary"),
        name="matmul_as_res",
    )(a, w, res)


def _in_proj_as_t_kernel(a_ref, wt_ref, wtx_ref, o_ref, *, n_plain):
    j = pl.program_id(1)
    shift = wtx_ref.shape[0]

    @pl.when(j < n_plain)
    def _():
        o_ref[...] = _dot_nt(a_ref[...], wt_ref[...].astype(BF16)).astype(o_ref.dtype)

    @pl.when(j >= n_plain)
    def _():
        wt = jnp.concatenate([wt_ref[shift:, :], wtx_ref[...]], axis=0).astype(BF16)
        o_ref[...] = _dot_nt(a_ref[...], wt).astype(o_ref.dtype)


def _in_proj_as_t(a, wt, layer, *, tm, tn, n_plain, n_total, shift):
    m, k = a.shape
    assert tn % shift == 0
    kern = functools.partial(_in_proj_as_t_kernel, n_plain=n_plain)
    return pl.pallas_call(
        kern,
        grid=(m // tm, n_total),
        in_specs=[_a_spec(tm, k),
                  pl.BlockSpec((None, tn, k), lambda i, j: (layer, j, 0)),
                  pl.BlockSpec((None, shift, k), lambda i, j: (layer, (j + 1) * (tn // shift), 0))],
        out_specs=pl.BlockSpec((tm, tn), lambda i, j: (i, j)),
        out_shape=jax.ShapeDtypeStruct((m, n_total * tn), BF16),
        compiler_params=_params("parallel", "arbitrary"),
        name="in_proj_as_t",
    )(a, wt, wt)


_CAST_ROWS = 512


def _cast_rows(w_ref, wb_ref):
    k = w_ref.shape[0]

    def body(r, c):
        rows = pl.ds(pl.multiple_of(r * _CAST_ROWS, _CAST_ROWS), _CAST_ROWS)
        wb_ref[rows, :] = w_ref[rows, :].astype(BF16)
        return c

    lax.fori_loop(0, k // _CAST_ROWS, body, 0)


def _mm_res2_kernel(a1_ref, a2_ref, w_ref, r_ref, o_ref, wb_ref):
    @pl.when(pl.program_id(1) == 0)
    def _():
        _cast_rows(w_ref, wb_ref)

    k1 = a1_ref.shape[1]
    o_ref[...] = (r_ref[...] + _dot(a1_ref[...], wb_ref[:k1, :])
                  + _dot(a2_ref[...], wb_ref[k1:, :]))


def _matmul_res2(a1, a2, w, layer, res, *, tm, tn):
    m, k1 = a1.shape
    k2 = a2.shape[1]
    k, n = w.shape[1:]
    assert k == k1 + k2
    return pl.pallas_call(
        _mm_res2_kernel,
        grid=(n // tn, m // tm),
        in_specs=[pl.BlockSpec((tm, k1), lambda j, i: (i, 0)),
                  pl.BlockSpec((tm, k2), lambda j, i: (i, 0)),
                  pl.BlockSpec((None, k, tn), lambda j, i: (layer, 0, j)),
                  pl.BlockSpec((tm, tn), lambda j, i: (i, j))],
        out_specs=pl.BlockSpec((tm, tn), lambda j, i: (i, j)),
        out_shape=jax.ShapeDtypeStruct((m, n), F32),
        scratch_shapes=[pltpu.VMEM((k, tn), BF16)],
        compiler_params=_params("arbitrary", "arbitrary"),
        name="matmul_res2",
    )(a1, a2, w, res)


def _fox_cumsum_kernel(g_ref, b_ref, c_ref, ct_ref, c_sc, *, blk, lane0, heads):
    s = g_ref.shape[0]
    row = lax.broadcasted_iota(jnp.int32, (blk, blk), 0)
    col = lax.broadcasted_iota(jnp.int32, (blk, blk), 1)
    tri = (col <= row).astype(BF16)
    carry = jnp.zeros((1, g_ref.shape[1]), F32)
    for n in range(s // blk):
        rows = slice(n * blk, (n + 1) * blk)
        lf = _log_sigmoid(g_ref[rows, :] + b_ref[...])
        c = _dot_exact_lhs(tri, lf) + carry
        c_sc[rows, :] = c
        carry = c[blk - 1:blk, :]
    c_all = c_sc[...]
    c_ref[...] = c_all
    ct_ref[...] = c_all.T[lane0:lane0 + heads, :]


def _fox_cumsum(gates, bias_row, *, batch, seq, lane0, heads, blk=256):
    t, n = gates.shape
    kern = functools.partial(_fox_cumsum_kernel, blk=blk, lane0=lane0, heads=heads)
    return pl.pallas_call(
        kern,
        grid=(batch,),
        in_specs=[pl.BlockSpec((seq, n), lambda b: (b, 0)),
                  pl.BlockSpec((1, n), lambda b: (0, 0))],
        out_specs=[pl.BlockSpec((seq, n), lambda b: (b, 0)),
                   pl.BlockSpec((None, heads, seq), lambda b: (b, 0, 0))],
        out_shape=[jax.ShapeDtypeStruct((t, n), F32),
                   jax.ShapeDtypeStruct((batch, heads, seq), F32)],
        scratch_shapes=[pltpu.VMEM((seq, n), F32)],
        compiler_params=_params("parallel"),
        name="fox_cumsum",
    )(gates, bias_row)


def _fox_kernel(q_ref, k_ref, v_ref, c_ref, ck_ref, o_ref, *, tq, lane0):
    h = pl.program_id(1)
    for qi in range(q_ref.shape[0] // tq):
        _fox_q_block(qi, h, q_ref, k_ref, v_ref, c_ref, ck_ref, o_ref, tq=tq, lane0=lane0)


def _fox_q_block(qi, h, q_ref, k_ref, v_ref, c_ref, ck_ref, o_ref, *, tq, lane0):
    dh = q_ref.shape[1]
    scale = dh ** -0.5
    q_rows = pl.ds(qi * tq, tq)
    q = q_ref[q_rows, :]
    c_blk = c_ref[q_rows, :]
    lane = lax.broadcasted_iota(jnp.int32, c_blk.shape, 1)
    cq = jnp.sum(jnp.where(lane == lane0 + h, c_blk, 0.0), axis=1, keepdims=True)

    def logits(kb):
        rows = pl.ds(kb * tq, tq)
        s = _dot_nt(q, k_ref[rows, :]) * scale
        return s + cq - ck_ref[kb], v_ref[rows, :]

    s, v = logits(qi)
    tpos = lax.broadcasted_iota(jnp.int32, (tq, tq), 0)
    spos = lax.broadcasted_iota(jnp.int32, (tq, tq), 1)
    s = jnp.where(spos <= tpos, s, NEG_BIG)
    m0 = jnp.max(s, axis=1, keepdims=True)
    p = jnp.exp(s - m0)
    l0 = jnp.sum(p, axis=1, keepdims=True)
    acc0 = _dot(p.astype(BF16), v)

    def body(kb, carry):
        m, l, acc = carry
        s, v = logits(kb)
        m_new = jnp.maximum(m, jnp.max(s, axis=1, keepdims=True))
        alpha = jnp.exp(m - m_new)
        p = jnp.exp(s - m_new)
        l = alpha * l + jnp.sum(p, axis=1, keepdims=True)
        acc = alpha * acc + _dot(p.astype(BF16), v)
        return m_new, l, acc

    carry = (m0, l0, acc0)
    for kb in range(qi):
        carry = body(kb, carry)
    _, l, acc = carry
    o_ref[q_rows, :] = (acc / l).astype(o_ref.dtype)


def _fox_attention(proj, c, ck, *, batch, seq, heads, col0, lane0, tq):
    t = proj.shape[0]
    dh = FOX_HEAD_DIM
    nq = seq // tq
    qb = col0 // dh
    kern = functools.partial(_fox_kernel, tq=tq, lane0=lane0)
    return pl.pallas_call(
        kern,
        grid=(batch, heads),
        in_specs=[pl.BlockSpec((seq, dh), lambda b, h: (b, qb + h)),
                  pl.BlockSpec((seq, dh), lambda b, h: (b, qb + heads + h)),
                  pl.BlockSpec((seq, dh), lambda b, h: (b, qb + 2 * heads + h)),
                  pl.BlockSpec((seq, c.shape[1]), lambda b, h: (b, 0)),
                  pl.BlockSpec((None, None, nq, 1, tq), lambda b, h: (b, h, 0, 0, 0))],
        out_specs=pl.BlockSpec((seq, dh), lambda b, h: (b, h)),
        out_shape=jax.ShapeDtypeStruct((t, heads * dh), BF16),
        compiler_params=_params("parallel", "parallel"),
        name="fox_attention",
    )(proj, proj, proj, c, ck)


def _sb_kernel(q_ref, k_ref, v_ref, o_ref, *, tq, tk):
    for qi in range(q_ref.shape[0] // tq):
        _sb_q_block(qi, q_ref, k_ref, v_ref, o_ref, tq=tq, tk=tk)


def _sb_q_block(qi, q_ref, k_ref, v_ref, o_ref, *, tq, tk):
    dh = q_ref.shape[1]
    scale = dh ** -0.5
    q_rows = pl.ds(qi * tq, tq)
    q = q_ref[q_rows, :]
    nd = tq // tk
    jj = lax.broadcasted_iota(jnp.int32, (tk, tk), 0)
    ss = lax.broadcasted_iota(jnp.int32, (tk, tk), 1)
    upper = (jj > ss).astype(BF16)
    reps = tk // dh

    def block(q_rows_, kb, run, mask):
        rows = pl.ds(kb * tk, tk)
        z = _dot_nt(q_rows_, k_ref[rows, :]) * scale
        sp = jnp.log(1.0 + jnp.exp(-jnp.abs(z)))
        log_beta = jnp.minimum(z, 0.0) - sp
        log_1m = log_beta - z
        if mask is not None:
            log_1m = jnp.where(mask, log_1m, 0.0)
        hi, lo = _split_bf16(log_1m, 2)
        suffix = _dot(hi, upper) + _dot(lo, upper)
        a = jnp.exp(log_beta + suffix + jnp.concatenate([run] * reps, axis=1))
        if mask is not None:
            a = jnp.where(mask, a, 0.0)
        pv = _dot(a.astype(BF16), v_ref[rows, :])
        rs = jnp.sum(log_1m, axis=1, keepdims=True)
        return pv, jnp.broadcast_to(rs, (rs.shape[0], dh))

    run = acc = None
    for d in range(nd):
        off = (nd - 1 - d) * tk
        n_rows = tq - off
        run_rows = jnp.zeros((n_rows, dh), F32) if run is None else run[off:]
        tri = (lax.broadcasted_iota(jnp.int32, (n_rows, tk), 1)
               < lax.broadcasted_iota(jnp.int32, (n_rows, tk), 0))
        pv, rs = block(q[off:], qi * nd + nd - 1 - d, run_rows, tri)
        if off:
            pv = jnp.concatenate([jnp.zeros((off, dh), F32), pv], axis=0)
            rs = jnp.concatenate([jnp.zeros((off, dh), F32), rs], axis=0)
        acc = pv if acc is None else acc + pv
        run = rs if run is None else run + rs

    for kb in range(qi * nd - 1, -1, -1):
        pv, rs = block(q, kb, run, None)
        acc = acc + pv
        run = run + rs
    o_ref[q_rows, :] = acc.astype(o_ref.dtype)


def _sb_attention(proj, *, batch, seq, heads, col0, tq, tk):
    t = proj.shape[0]
    dh = SB_HEAD_DIM
    qb = col0 // dh
    kern = functools.partial(_sb_kernel, tq=tq, tk=tk)
    return pl.pallas_call(
        kern,
        grid=(batch, heads),
        in_specs=[pl.BlockSpec((seq, dh), lambda b, h: (b, qb + h)),
                  pl.BlockSpec((seq, dh), lambda b, h: (b, qb + heads + h)),
                  pl.BlockSpec((seq, dh), lambda b, h: (b, qb + 2 * heads + h))],
        out_specs=pl.BlockSpec((seq, dh), lambda b, h: (b, h)),
        out_shape=jax.ShapeDtypeStruct((t, heads * dh), BF16),
        compiler_params=_params("parallel", "parallel"),
        name="sb_attention",
    )(proj, proj, proj)


def _gla_kernel(q_ref, k_ref, v_ref, gg_ref, gate_ref, wup_ref, bg_ref, gn_ref, o_ref,
                state_ref, osc_ref, *, rows_blk, chunk, sub):
    state_ref[...] = jnp.zeros_like(state_ref)

    def row_block(r, carry):
        _gla_row_block(pl.multiple_of(r * rows_blk, rows_blk), q_ref, k_ref, v_ref, gg_ref,
                       gate_ref, wup_ref, bg_ref, gn_ref, o_ref, state_ref, osc_ref,
                       rows_blk=rows_blk, chunk=chunk, sub=sub)
        return carry

    lax.fori_loop(0, q_ref.shape[0] // rows_blk, row_block, 0)


def _gla_row_block(base, q_ref, k_ref, v_ref, gg_ref, gate_ref, wup_ref, bg_ref, gn_ref, o_ref,
                   state_ref, osc_ref, *, rows_blk, chunk, sub):
    dk = q_ref.shape[1]
    n_chunks = rows_blk // chunk
    n_sub = chunk // sub
    q_scale = dk ** -0.5
    blk_rows = pl.ds(base, rows_blk)

    z = _dot_f32(gate_ref[blk_rows, :], wup_ref[...]) + bg_ref[...]
    a = _log_sigmoid(z) / GLA_GATE_TAU
    row = lax.broadcasted_iota(jnp.int32, (rows_blk, rows_blk), 0)
    col = lax.broadcasted_iota(jnp.int32, (rows_blk, rows_blk), 1)
    shift = int(math.log2(chunk))
    same = lax.shift_right_logical(row, shift) == lax.shift_right_logical(col, shift)
    tri = ((col <= row) & same).astype(BF16)
    b = _dot_exact_lhs(tri, a)

    key_row = lax.broadcasted_iota(jnp.int32, (chunk, 1), 0)
    key_lane = lax.broadcasted_iota(jnp.int32, (sub, chunk), 1)
    band_row = lax.broadcasted_iota(jnp.int32, (sub, chunk), 0)
    for c in range(n_chunks):
        r0 = c * chunk
        bc = b[r0:r0 + chunk]
        c_rows = pl.ds(pl.multiple_of(base + r0, chunk), chunk)
        qc = q_ref[c_rows, :].astype(F32) * q_scale
        kc = k_ref[c_rows, :].astype(F32)
        vc = v_ref[c_rows, :]
        state = state_ref[...]
        inter = _dot((qc * jnp.exp(bc)).astype(BF16), state.astype(BF16))
        bands = []
        for i in range(n_sub):
            lo, hi = i * sub, (i + 1) * sub
            qs, ks, bs = qc[lo:hi], kc[lo:hi], bc[lo:hi]
            if i == 0:
                s = jnp.zeros((sub, chunk), F32)
            else:
                ref = bc[lo - 1:lo]
                qi = (qs * jnp.exp(bs - ref)).astype(BF16)
                kdec = jnp.exp(jnp.minimum(ref - bc, 0.0))
                ki = jnp.where(key_row < lo, kc * kdec, 0.0).astype(BF16)
                s = _dot_nt(qi, ki)
            for s_ in range(sub):
                dec = jnp.exp(jnp.minimum(bs - bs[s_:s_ + 1], 0.0))
                col = jnp.sum(qs * ks[s_:s_ + 1] * dec, axis=1, keepdims=True)
                s = jnp.where((key_lane == lo + s_) & (band_row >= s_), col, s)
            bands.append(s)
        scores = jnp.concatenate(bands, axis=0)
        osc_ref[r0:r0 + chunk, :] = inter + _dot(scores.astype(BF16), vc)

        bl = bc[chunk - 1:chunk]
        kd = kc * jnp.exp(bl - bc)
        stacked = jnp.concatenate([kd, jnp.broadcast_to(jnp.exp(bl), (chunk, dk))], axis=0)
        st = stacked.T
        vpad = jnp.concatenate([vc, jnp.zeros_like(vc)], axis=0)
        state_ref[...] = state * st[:, chunk:chunk + 1] + _dot(st.astype(BF16), vpad)

    o = osc_ref[...]
    ms = jnp.mean(o * o, axis=-1, keepdims=True)
    y = o * lax.rsqrt(ms + EPS) * gn_ref[...]
    g = gg_ref[blk_rows, :].astype(F32)
    o_ref[blk_rows, :] = (y * (g * _sigmoid(g))).astype(o_ref.dtype)


def _gla(proj, gates, wup_pad, b_gate, g_norm, *, batch, seq, heads, dk, dv, rows_blk=256):
    t = proj.shape[0]
    kb0 = heads
    vb0 = 2 * heads * dk // dv
    gb0 = vb0 + heads
    kern = functools.partial(_gla_kernel, rows_blk=rows_blk, chunk=GLA_CHUNK, sub=GLA_SUB)
    return pl.pallas_call(
        kern,
        grid=(batch, heads),
        in_specs=[pl.BlockSpec((seq, dk), lambda b, h: (b, h)),
                  pl.BlockSpec((seq, dk), lambda b, h: (b, kb0 + h)),
                  pl.BlockSpec((seq, dv), lambda b, h: (b, vb0 + h)),
                  pl.BlockSpec((seq, dv), lambda b, h: (b, gb0 + h)),
                  pl.BlockSpec((seq, gates.shape[1]), lambda b, h: (b, 0)),
                  pl.BlockSpec((wup_pad.shape[0], dk), lambda b, h: (0, h)),
                  pl.BlockSpec((1, dk), lambda b, h: (0, h)),
                  pl.BlockSpec((1, dv), lambda b, h: (0, 0))],
        out_specs=pl.BlockSpec((seq, dv), lambda b, h: (b, h)),
        out_shape=jax.ShapeDtypeStruct((t, heads * dv), BF16),
        scratch_shapes=[pltpu.VMEM((dk, dv), F32),
                        pltpu.VMEM((rows_blk, dv), F32)],
        compiler_params=_params("parallel", "parallel"),
        name="gla",
    )(proj, proj, proj, proj, gates, wup_pad, b_gate, g_norm)


def _rglru_kernel(rx_ref, ry_ref, cw_ref, cb_ref, wx_ref, bx_ref, wa_ref, ba_ref, lam_ref,
                  o_ref, *, rows_t):
    seq, w = rx_ref.shape
    halo = 8
    wxb = wx_ref[...].astype(BF16)
    wab = wa_ref[...].astype(BF16)
    cw = cw_ref[...]
    neg_lam = -lam_ref[...]
    softplus = jnp.maximum(neg_lam, 0.0) + _softplus_neg_abs(neg_lam)
    coef = -RG_C * softplus
    row = lax.broadcasted_iota(jnp.int32, (rows_t, 1), 0)

    def body(t, carry):
        h_prev, tail = carry
        rows = pl.ds(pl.multiple_of(t * rows_t, rows_t), rows_t)
        x = rx_ref[rows, :].astype(F32)
        xe = jnp.concatenate([tail, x], axis=0)
        xc = cb_ref[...] + cw[RG_CONV - 1:RG_CONV] * x
        for s_ in range(1, RG_CONV):
            xc = xc + cw[RG_CONV - 1 - s_:RG_CONV - s_] * pltpu.roll(xe, s_, axis=0)[halo:]
        xcb = xc.astype(BF16)
        gate_x = _sigmoid(_dot(xcb, wxb) + bx_ref[...])
        gate_a = _sigmoid(_dot(xcb, wab) + ba_ref[...])
        log_a = coef * gate_a
        a_dec = jnp.exp(log_a)
        one_m_a2 = -jnp.tanh(log_a) * (a_dec * a_dec + 1.0)
        u = jnp.sqrt(one_m_a2) * (gate_x * xc)
        d = 1
        while d < rows_t:
            if d < halo:
                a_sh = jnp.where(row >= d, pltpu.roll(a_dec, d, axis=0), 1.0)
                u_sh = jnp.where(row >= d, pltpu.roll(u, d, axis=0), 0.0)
            else:
                a_sh = jnp.concatenate([jnp.ones((d, w), F32), a_dec[:rows_t - d]], axis=0)
                u_sh = jnp.concatenate([jnp.zeros((d, w), F32), u[:rows_t - d]], axis=0)
            u = a_dec * u_sh + u
            a_dec = a_dec * a_sh
            d *= 2
        h = u + a_dec * h_prev
        y = ry_ref[rows, :].astype(F32)
        o_ref[rows, :] = (h * _gelu_tanh(y)).astype(o_ref.dtype)
        return h[rows_t - 1:rows_t], x[rows_t - halo:]

    init = (jnp.zeros((1, w), F32), jnp.zeros((halo, w), F32))
    lax.fori_loop(0, seq // rows_t, body, init)


def _rglru(proj, conv_w, conv_b, w_x, b_x, w_a, b_a, lam, *, batch, seq, width, rows_t=256):
    t = proj.shape[0]
    wb = RG_BLOCK
    nb = width // wb
    kern = functools.partial(_rglru_kernel, rows_t=rows_t)
    vec = lambda v: v.reshape(1, width)
    vspec = pl.BlockSpec((1, wb), lambda b, g: (0, g))
    mspec = pl.BlockSpec((None, wb, wb), lambda b, g: (g, 0, 0))
    return pl.pallas_call(
        kern,
        grid=(batch, nb),
        in_specs=[pl.BlockSpec((seq, wb), lambda b, g: (b, nb + g)),
                  pl.BlockSpec((seq, wb), lambda b, g: (b, g)),
                  pl.BlockSpec((RG_CONV, wb), lambda b, g: (0, g)),
                  vspec, mspec, vspec, mspec, vspec, vspec],
        out_specs=pl.BlockSpec((seq, wb), lambda b, g: (b, g)),
        out_shape=jax.ShapeDtypeStruct((t, width), BF16),
        compiler_params=_params("parallel", "parallel"),
        name="rglru",
    )(proj, proj, conv_w, vec(conv_b), w_x, vec(b_x), w_a, vec(b_a), vec(lam))


def _ffn(h, g, w_gate, w_up, w_down, layer):
    u = _rmsnorm(h, g, BF16)
    hid = _matmul_as_swiglu(u, w_gate, w_up, layer, tm=TM_AS, tn=TN_SWIGLU)
    return _matmul_as_res(hid, w_down, layer, h, tm=TM_DOWN, tn=TN_DOWN)


def _even_mixer(h, g, w_in, i, w_gate_up, b_gate, g_norm, b_f, w_out, *, batch, seq):
    d = h.shape[1]
    dk, dv = d // 16, d // 8
    qk, vw, fw = GLA_HEADS * dk, GLA_HEADS * dv, d // 2
    fox_heads = fw // FOX_HEAD_DIM
    rank = GLA_GATE_RANK
    main0 = 2 * qk + 2 * vw
    main1 = 3 * fw
    wt_in = jnp.swapaxes(w_in, 1, 2)
    u, gates = _gate_proj_t(h, g, wt_in, i, col_a=main0, n_a=rank,
                            col_b=main0 + rank + main1, n_b=fox_heads)
    proj = _in_proj_as_t(u, wt_in, i, tm=TM_AS, tn=TN, n_plain=main0 // TN,
                         n_total=(main0 + main1) // TN, shift=rank)
    bias_row = jnp.zeros((1, LANES), F32).at[0, rank:rank + fox_heads].set(b_f)
    c, ct = _fox_cumsum(gates, bias_row, batch=batch, seq=seq, lane0=rank, heads=fox_heads)
    ck = ct.reshape(batch, fox_heads, seq // FOX_TQ, 1, FOX_TQ)
    o_fox = _fox_attention(proj, c, ck, batch=batch, seq=seq, heads=fox_heads, col0=main0,
                           lane0=rank, tq=FOX_TQ)
    wup_pad = jnp.pad(w_gate_up, ((0, LANES - rank), (0, 0)))
    o_gla = _gla(proj, gates, wup_pad, b_gate.reshape(1, qk), g_norm.reshape(1, dv),
                 batch=batch, seq=seq, heads=GLA_HEADS, dk=dk, dv=dv)
    return _matmul_res2(o_gla, o_fox, w_out, i, h, tm=TM_OUT, tn=TN)


def _odd_mixer(h, g, w_in, i, conv_w, conv_b, w_x, b_x, w_a, b_a, lam, w_out, *, batch, seq):
    d = h.shape[1]
    width = d // 2
    sb_heads = width // SB_HEAD_DIM
    u = _rmsnorm(h, g, BF16)
    proj = _matmul_as(u, w_in, i, tm=TM_AS, tn=TN, out_dtype=BF16)
    o_rg = _rglru(proj, conv_w, conv_b, w_x, b_x, w_a, b_a, lam, batch=batch, seq=seq,
                  width=width)
    o_sb = _sb_attention(proj, batch=batch, seq=seq, heads=sb_heads, col0=2 * width,
                         tq=SB_TQ, tk=SB_TK)
    return _matmul_res2(o_rg, o_sb, w_out, i, h, tm=TM_OUT, tn=TN)


def kernel(x, norm_mix, norm_ffn, ffn_w_gate, ffn_w_up, ffn_w_down, ab_w_in, gla_w_gate_up,
           gla_b_gate, gla_norm, fox_b_f, ab_w_out, cd_w_in, rg_conv_w, rg_conv_b, rg_w_x,
           rg_b_x, rg_w_a, rg_b_a, rg_lambda, cd_w_out, final_norm):
    batch, seq, d = x.shape
    h = x.reshape(batch * seq, d)
    for layer in range(norm_mix.shape[0]):
        i = layer // 2
        if layer % 2 == 0:
            h = _even_mixer(h, norm_mix[layer], ab_w_in, i, gla_w_gate_up[i], gla_b_gate[i],
                            gla_norm[i], fox_b_f[i], ab_w_out, batch=batch, seq=seq)
        else:
            h = _odd_mixer(h, norm_mix[layer], cd_w_in, i, rg_conv_w[i], rg_conv_b[i],
                           rg_w_x[i], rg_b_x[i], rg_w_a[i], rg_b_a[i], rg_lambda[i],
                           cd_w_out, batch=batch, seq=seq)
        h = _ffn(h, norm_ffn[layer], ffn_w_gate, ffn_w_up, ffn_w_down, layer)
    return _rmsnorm(h, final_norm, F32).reshape(batch, seq, d)
```

```python
import functools
import math

import jax
import jax.numpy as jnp
from jax import lax
from jax.experimental import pallas as pl
from jax.experimental.pallas import tpu as pltpu

F32 = jnp.float32
BF16 = jnp.bfloat16

EPS = 1e-6
LANES = 128
VMEM_LIMIT = 56 * 1024 * 1024

GLA_HEADS = 4
GLA_GATE_RANK = 16
GLA_GATE_TAU = 16.0
GLA_CHUNK = 64
GLA_SUB = 16
FOX_HEAD_DIM = 128
SB_HEAD_DIM = 128
RG_BLOCK = 256
RG_CONV = 4
RG_C = 8.0

TM_AS = 2048
TN = 512
TN_SWIGLU = 256
TM_DOWN, TN_DOWN = 1024, 256
TM_OUT = 1024
FOX_TQ = 1024
SB_TQ, SB_TK = 2048, 256

NEG_BIG = -1e30
_NT = (((1,), (1,)), ((), ()))


def _params(*sem):
    return pltpu.CompilerParams(dimension_semantics=sem, vmem_limit_bytes=VMEM_LIMIT)


def _dot(a, b):
    return jnp.dot(a, b, preferred_element_type=F32)


def _dot_nt(a, b):
    return lax.dot_general(a, b, _NT, preferred_element_type=F32)


def _split_bf16(x, n):
    parts = []
    r = x
    for _ in range(n):
        p = r.astype(BF16)
        parts.append(p)
        r = r - p.astype(F32)
    return parts


def _dot_f32(a, b):
    ah, al = _split_bf16(a, 2)
    bh, bl = _split_bf16(b, 2)
    return _dot(ah, bh) + _dot(ah, bl) + _dot(al, bh)


def _dot_exact_lhs(m01, x, n=3):
    acc = None
    for p in _split_bf16(x, n):
        t = _dot(m01, p)
        acc = t if acc is None else acc + t
    return acc


def _softplus_neg_abs(z):
    return jnp.log1p(jnp.exp(-jnp.abs(z)))


def _log_sigmoid(z):
    return jnp.minimum(z, 0.0) - _softplus_neg_abs(z)


def _sigmoid(z):
    return 1.0 / (1.0 + jnp.exp(-z))


def _gelu_tanh(y):
    c = math.sqrt(2.0 / math.pi)
    return y * (0.5 * (1.0 + jnp.tanh(c * (y + 0.044715 * (y * y * y)))))


def _rmsnorm_kernel(x_ref, g_ref, o_ref):
    x = x_ref[...]
    ms = jnp.mean(x * x, axis=-1, keepdims=True)
    o_ref[...] = (x * lax.rsqrt(ms + EPS) * g_ref[...]).astype(o_ref.dtype)


def _rmsnorm(x, g, out_dtype, tm=512):
    t, d = x.shape
    return pl.pallas_call(
        _rmsnorm_kernel,
        grid=(t // tm,),
        in_specs=[pl.BlockSpec((tm, d), lambda i: (i, 0)),
                  pl.BlockSpec((1, d), lambda i: (0, 0))],
        out_specs=pl.BlockSpec((tm, d), lambda i: (i, 0)),
        out_shape=jax.ShapeDtypeStruct((t, d), out_dtype),
        compiler_params=_params("parallel"),
        name="rmsnorm",
    )(x, g.reshape(1, d))


def _gate_proj_t_kernel(x_ref, g_ref, wa_ref, wb_ref, u_ref, o_ref):
    x = x_ref[...]
    ms = jnp.mean(x * x, axis=-1, keepdims=True)
    u = x * lax.rsqrt(ms + EPS) * g_ref[...]
    u_ref[...] = u.astype(u_ref.dtype)
    used = wa_ref.shape[0] + wb_ref.shape[0]
    wt = jnp.concatenate([wa_ref[...], wb_ref[...],
                          jnp.zeros((o_ref.shape[1] - used, x.shape[1]), F32)], axis=0)
    uh, ul = _split_bf16(u, 2)
    wh, wl = _split_bf16(wt, 2)
    o_ref[...] = _dot_nt(uh, wh) + _dot_nt(uh, wl) + _dot_nt(ul, wh)


def _gate_proj_t(x, g, wt, layer, *, col_a, n_a, col_b, n_b, tm=256):
    t, d = x.shape
    assert col_a % n_a == 0 and col_b % n_b == 0
    return pl.pallas_call(
        _gate_proj_t_kernel,
        grid=(t // tm,),
        in_specs=[pl.BlockSpec((tm, d), lambda i: (i, 0)),
                  pl.BlockSpec((1, d), lambda i: (0, 0)),
                  pl.BlockSpec((None, n_a, d), lambda i: (layer, col_a // n_a, 0)),
                  pl.BlockSpec((None, n_b, d), lambda i: (layer, col_b // n_b, 0))],
        out_specs=[pl.BlockSpec((tm, d), lambda i: (i, 0)),
                   pl.BlockSpec((tm, LANES), lambda i: (i, 0))],
        out_shape=[jax.ShapeDtypeStruct((t, d), BF16),
                   jax.ShapeDtypeStruct((t, LANES), F32)],
        compiler_params=_params("parallel"),
        name="gate_proj_t",
    )(x, g.reshape(1, d), wt, wt)


def _mm_as_kernel(a_ref, w_ref, o_ref):
    o_ref[...] = _dot(a_ref[...], w_ref[...].astype(BF16)).astype(o_ref.dtype)


def _mm_as_swiglu_kernel(a_ref, wg_ref, wu_ref, o_ref):
    a = a_ref[...]
    g = _dot(a, wg_ref[...].astype(BF16))
    u = _dot(a, wu_ref[...].astype(BF16))
    o_ref[...] = (g * _sigmoid(g) * u).astype(o_ref.dtype)


def _mm_as_res_kernel(a_ref, w_ref, r_ref, o_ref):
    o_ref[...] = r_ref[...] + _dot(a_ref[...], w_ref[...].astype(BF16))


def _a_spec(tm, k):
    return pl.BlockSpec((tm, k), lambda i, j: (i, 0), pipeline_mode=pl.Buffered(1))


def _wt_spec(layer, k, tn):
    return pl.BlockSpec((None, k, tn), lambda i, j: (layer, 0, j))


def _matmul_as(a, w, layer, *, tm, tn, out_dtype):
    m, k = a.shape
    n = w.shape[2]
    return pl.pallas_call(
        _mm_as_kernel,
        grid=(m // tm, n // tn),
        in_specs=[_a_spec(tm, k), _wt_spec(layer, k, tn)],
        out_specs=pl.BlockSpec((tm, tn), lambda i, j: (i, j)),
        out_shape=jax.ShapeDtypeStruct((m, n), out_dtype),
        compiler_params=_params("parallel", "arbitrary"),
        name="matmul_as",
    )(a, w)


def _matmul_as_swiglu(a, wg, wu, layer, *, tm, tn):
    m, k = a.shape
    n = wg.shape[2]
    return pl.pallas_call(
        _mm_as_swiglu_kernel,
        grid=(m // tm, n // tn),
        in_specs=[_a_spec(tm, k), _wt_spec(layer, k, tn), _wt_spec(layer, k, tn)],
        out_specs=pl.BlockSpec((tm, tn), lambda i, j: (i, j)),
        out_shape=jax.ShapeDtypeStruct((m, n), BF16),
        compiler_params=_params("parallel", "arbitrary"),
        name="matmul_as_swiglu",
    )(a, wg, wu)


def _matmul_as_res(a, w, layer, res, *, tm, tn):
    m, k = a.shape
    n = w.shape[2]
    return pl.pallas_call(
        _mm_as_res_kernel,
        grid=(m // tm, n // tn),
        in_specs=[_a_spec(tm, k), _wt_spec(layer, k, tn),
                  pl.BlockSpec((tm, tn), lambda i, j: (i, j))],
        out_specs=pl.BlockSpec((tm, tn), lambda i, j: (i, j)),
        out_shape=jax.ShapeDtypeStruct((m, n), F32),
        compiler_params=_params("parallel", "arbitrary"),
        name="matmul_as_res",
    )(a, w, res)


def _in_proj_as_t_kernel(a_ref, wt_ref, wtx_ref, o_ref, *, n_plain):
    j = pl.program_id(1)
    shift = wtx_ref.shape[0]

    @pl.when(j < n_plain)
    def _():
        o_ref[...] = _dot_nt(a_ref[...], wt_ref[...].astype(BF16)).astype(o_ref.dtype)

    @pl.when(j >= n_plain)
    def _():
        wt = jnp.concatenate([wt_ref[shift:, :], wtx_ref[...]], axis=0).astype(BF16)
        o_ref[...] = _dot_nt(a_ref[...], wt).astype(o_ref.dtype)


def _in_proj_as_t(a, wt, layer, *, tm, tn, n_plain, n_total, shift):
    m, k = a.shape
    assert tn % shift == 0
    kern = functools.partial(_in_proj_as_t_kernel, n_plain=n_plain)
    return pl.pallas_call(
        kern,
        grid=(m // tm, n_total),
        in_specs=[_a_spec(tm, k),
                  pl.BlockSpec((None, tn, k), lambda i, j: (layer, j, 0)),
                  pl.BlockSpec((None, shift, k), lambda i, j: (layer, (j + 1) * (tn // shift), 0))],
        out_specs=pl.BlockSpec((tm, tn), lambda i, j: (i, j)),
        out_shape=jax.ShapeDtypeStruct((m, n_total * tn), BF16),
        compiler_params=_params("parallel", "arbitrary"),
        name="in_proj_as_t",
    )(a, wt, wt)


_CAST_ROWS = 512


def _cast_rows(w_ref, wb_ref):
    k = w_ref.shape[0]

    def body(r, c):
        rows = pl.ds(pl.multiple_of(r * _CAST_ROWS, _CAST_ROWS), _CAST_ROWS)
        wb_ref[rows, :] = w_ref[rows, :].astype(BF16)
        return c

    lax.fori_loop(0, k // _CAST_ROWS, body, 0)


def _mm_res2_kernel(a1_ref, a2_ref, w_ref, r_ref, o_ref, wb_ref):
    @pl.when(pl.program_id(1) == 0)
    def _():
        _cast_rows(w_ref, wb_ref)

    k1 = a1_ref.shape[1]
    o_ref[...] = (r_ref[...] + _dot(a1_ref[...], wb_ref[:k1, :])
                  + _dot(a2_ref[...], wb_ref[k1:, :]))


def _matmul_res2(a1, a2, w, layer, res, *, tm, tn):
    m, k1 = a1.shape
    k2 = a2.shape[1]
    k, n = w.shape[1:]
    assert k == k1 + k2
    return pl.pallas_call(
        _mm_res2_kernel,
        grid=(n // tn, m // tm),
        in_specs=[pl.BlockSpec((tm, k1), lambda j, i: (i, 0)),
                  pl.BlockSpec((tm, k2), lambda j, i: (i, 0)),
                  pl.BlockSpec((None, k, tn), lambda j, i: (layer, 0, j)),
                  pl.BlockSpec((tm, tn), lambda j, i: (i, j))],
        out_specs=pl.BlockSpec((tm, tn), lambda j, i: (i, j)),
        out_shape=jax.ShapeDtypeStruct((m, n), F32),
        scratch_shapes=[pltpu.VMEM((k, tn), BF16)],
        compiler_params=_params("arbitrary", "arbitrary"),
        name="matmul_res2",
    )(a1, a2, w, res)


def _fox_cumsum_kernel(g_ref, b_ref, c_ref, ct_ref, c_sc, *, blk, lane0, heads):
    s = g_ref.shape[0]
    row = lax.broadcasted_iota(jnp.int32, (blk, blk), 0)
    col = lax.broadcasted_iota(jnp.int32, (blk, blk), 1)
    tri = (col <= row).astype(BF16)
    carry = jnp.zeros((1, g_ref.shape[1]), F32)
    for n in range(s // blk):
        rows = slice(n * blk, (n + 1) * blk)
        lf = _log_sigmoid(g_ref[rows, :] + b_ref[...])
        c = _dot_exact_lhs(tri, lf) + carry
        c_sc[rows, :] = c
        carry = c[blk - 1:blk, :]
    c_all = c_sc[...]
    c_ref[...] = c_all
    ct_ref[...] = c_all.T[lane0:lane0 + heads, :]


def _fox_cumsum(gates, bias_row, *, batch, seq, lane0, heads, blk=256):
    t, n = gates.shape
    kern = functools.partial(_fox_cumsum_kernel, blk=blk, lane0=lane0, heads=heads)
    return pl.pallas_call(
        kern,
        grid=(batch,),
        in_specs=[pl.BlockSpec((seq, n), lambda b: (b, 0)),
                  pl.BlockSpec((1, n), lambda b: (0, 0))],
        out_specs=[pl.BlockSpec((seq, n), lambda b: (b, 0)),
                   pl.BlockSpec((None, heads, seq), lambda b: (b, 0, 0))],
        out_shape=[jax.ShapeDtypeStruct((t, n), F32),
                   jax.ShapeDtypeStruct((batch, heads, seq), F32)],
        scratch_shapes=[pltpu.VMEM((seq, n), F32)],
        compiler_params=_params("parallel"),
        name="fox_cumsum",
    )(gates, bias_row)


def _fox_kernel(q_ref, k_ref, v_ref, c_ref, ck_ref, o_ref, *, tq, lane0):
    h = pl.program_id(1)
    for qi in range(q_ref.shape[0] // tq):
        _fox_q_block(qi, h, q_ref, k_ref, v_ref, c_ref, ck_ref, o_ref, tq=tq, lane0=lane0)


def _fox_q_block(qi, h, q_ref, k_ref, v_ref, c_ref, ck_ref, o_ref, *, tq, lane0):
    dh = q_ref.shape[1]
    scale = dh ** -0.5
    q_rows = pl.ds(qi * tq, tq)
    q = q_ref[q_rows, :]
    c_blk = c_ref[q_rows, :]
    lane = lax.broadcasted_iota(jnp.int32, c_blk.shape, 1)
    cq = jnp.sum(jnp.where(lane == lane0 + h, c_blk, 0.0), axis=1, keepdims=True)

    def logits(kb):
        rows = pl.ds(kb * tq, tq)
        s = _dot_nt(q, k_ref[rows, :]) * scale
        return s + cq - ck_ref[kb], v_ref[rows, :]

    s, v = logits(qi)
    tpos = lax.broadcasted_iota(jnp.int32, (tq, tq), 0)
    spos = lax.broadcasted_iota(jnp.int32, (tq, tq), 1)
    s = jnp.where(spos <= tpos, s, NEG_BIG)
    m0 = jnp.max(s, axis=1, keepdims=True)
    p = jnp.exp(s - m0)
    l0 = jnp.sum(p, axis=1, keepdims=True)
    acc0 = _dot(p.astype(BF16), v)

    def body(kb, carry):
        m, l, acc = carry
        s, v = logits(kb)
        m_new = jnp.maximum(m, jnp.max(s, axis=1, keepdims=True))
        alpha = jnp.exp(m - m_new)
        p = jnp.exp(s - m_new)
        l = alpha * l + jnp.sum(p, axis=1, keepdims=True)
        acc = alpha * acc + _dot(p.astype(BF16), v)
        return m_new, l, acc

    carry = (m0, l0, acc0)
    for kb in range(qi):
        carry = body(kb, carry)
    _, l, acc = carry
    o_ref[q_rows, :] = (acc / l).astype(o_ref.dtype)


def _fox_attention(proj, c, ck, *, batch, seq, heads, col0, lane0, tq):
    t = proj.shape[0]
    dh = FOX_HEAD_DIM
    nq = seq // tq
    qb = col0 // dh
    kern = functools.partial(_fox_kernel, tq=tq, lane0=lane0)
    return pl.pallas_call(
        kern,
        grid=(batch, heads),
        in_specs=[pl.BlockSpec((seq, dh), lambda b, h: (b, qb + h)),
                  pl.BlockSpec((seq, dh), lambda b, h: (b, qb + heads + h)),
                  pl.BlockSpec((seq, dh), lambda b, h: (b, qb + 2 * heads + h)),
                  pl.BlockSpec((seq, c.shape[1]), lambda b, h: (b, 0)),
                  pl.BlockSpec((None, None, nq, 1, tq), lambda b, h: (b, h, 0, 0, 0))],
        out_specs=pl.BlockSpec((seq, dh), lambda b, h: (b, h)),
        out_shape=jax.ShapeDtypeStruct((t, heads * dh), BF16),
        compiler_params=_params("parallel", "parallel"),
        name="fox_attention",
    )(proj, proj, proj, c, ck)


def _sb_kernel(q_ref, k_ref, v_ref, o_ref, *, tq, tk):
    for qi in range(q_ref.shape[0] // tq):
        _sb_q_block(qi, q_ref, k_ref, v_ref, o_ref, tq=tq, tk=tk)


def _sb_q_block(qi, q_ref, k_ref, v_ref, o_ref, *, tq, tk):
    dh = q_ref.shape[1]
    scale = dh ** -0.5
    q_rows = pl.ds(qi * tq, tq)
    q = q_ref[q_rows, :]
    nd = tq // tk
    jj = lax.broadcasted_iota(jnp.int32, (tk, tk), 0)
    ss = lax.broadcasted_iota(jnp.int32, (tk, tk), 1)
    upper = (jj > ss).astype(BF16)
    reps = tk // dh

    def block(q_rows_, kb, run, mask):
        rows = pl.ds(kb * tk, tk)
        z = _dot_nt(q_rows_, k_ref[rows, :]) * scale
        sp = jnp.log(1.0 + jnp.exp(-jnp.abs(z)))
        log_beta = jnp.minimum(z, 0.0) - sp
        log_1m = log_beta - z
        if mask is not None:
            log_1m = jnp.where(mask, log_1m, 0.0)
        hi, lo = _split_bf16(log_1m, 2)
        suffix = _dot(hi, upper) + _dot(lo, upper)
        a = jnp.exp(log_beta + suffix + jnp.concatenate([run] * reps, axis=1))
        if mask is not None:
            a = jnp.where(mask, a, 0.0)
        pv = _dot(a.astype(BF16), v_ref[rows, :])
        rs = jnp.sum(log_1m, axis=1, keepdims=True)
        return pv, jnp.broadcast_to(rs, (rs.shape[0], dh))

    run = acc = None
    for d in range(nd):
        off = (nd - 1 - d) * tk
        n_rows = tq - off
        run_rows = jnp.zeros((n_rows, dh), F32) if run is None else run[off:]
        tri = (lax.broadcasted_iota(jnp.int32, (n_rows, tk), 1)
               < lax.broadcasted_iota(jnp.int32, (n_rows, tk), 0))
        pv, rs = block(q[off:], qi * nd + nd - 1 - d, run_rows, tri)
        if off:
            pv = jnp.concatenate([jnp.zeros((off, dh), F32), pv], axis=0)
            rs = jnp.concatenate([jnp.zeros((off, dh), F32), rs], axis=0)
        acc = pv if acc is None else acc + pv
        run = rs if run is None else run + rs

    for kb in range(qi * nd - 1, -1, -1):
        pv, rs = block(q, kb, run, None)
        acc = acc + pv
        run = run + rs
    o_ref[q_rows, :] = acc.astype(o_ref.dtype)


def _sb_attention(proj, *, batch, seq, heads, col0, tq, tk):
    t = proj.shape[0]
    dh = SB_HEAD_DIM
    qb = col0 // dh
    kern = functools.partial(_sb_kernel, tq=tq, tk=tk)
    return pl.pallas_call(
        kern,
        grid=(batch, heads),
        in_specs=[pl.BlockSpec((seq, dh), lambda b, h: (b, qb + h)),
                  pl.BlockSpec((seq, dh), lambda b, h: (b, qb + heads + h)),
                  pl.BlockSpec((seq, dh), lambda b, h: (b, qb + 2 * heads + h))],
        out_specs=pl.BlockSpec((seq, dh), lambda b, h: (b, h)),
        out_shape=jax.ShapeDtypeStruct((t, heads * dh), BF16),
        compiler_params=_params("parallel", "parallel"),
        name="sb_attention",
    )(proj, proj, proj)


def _gla_kernel(q_ref, k_ref, v_ref, gg_ref, gate_ref, wup_ref, bg_ref, gn_ref, o_ref,
                state_ref, osc_ref, *, rows_blk, chunk, sub):
    state_ref[...] = jnp.zeros_like(state_ref)

    def row_block(r, carry):
        _gla_row_block(pl.multiple_of(r * rows_blk, rows_blk), q_ref, k_ref, v_ref, gg_ref,
                       gate_ref, wup_ref, bg_ref, gn_ref, o_ref, state_ref, osc_ref,
                       rows_blk=rows_blk, chunk=chunk, sub=sub)
        return carry

    lax.fori_loop(0, q_ref.shape[0] // rows_blk, row_block, 0)


def _gla_row_block(base, q_ref, k_ref, v_ref, gg_ref, gate_ref, wup_ref, bg_ref, gn_ref, o_ref,
                   state_ref, osc_ref, *, rows_blk, chunk, sub):
    dk = q_ref.shape[1]
    n_chunks = rows_blk // chunk
    n_sub = chunk // sub
    q_scale = dk ** -0.5
    blk_rows = pl.ds(base, rows_blk)

    z = _dot_f32(gate_ref[blk_rows, :], wup_ref[...]) + bg_ref[...]
    a = _log_sigmoid(z) / GLA_GATE_TAU
    row = lax.broadcasted_iota(jnp.int32, (rows_blk, rows_blk), 0)
    col = lax.broadcasted_iota(jnp.int32, (rows_blk, rows_blk), 1)
    shift = int(math.log2(chunk))
    same = lax.shift_right_logical(row, shift) == lax.shift_right_logical(col, shift)
    tri = ((col <= row) & same).astype(BF16)
    b = _dot_exact_lhs(tri, a)

    key_row = lax.broadcasted_iota(jnp.int32, (chunk, 1), 0)
    key_lane = lax.broadcasted_iota(jnp.int32, (sub, chunk), 1)
    band_row = lax.broadcasted_iota(jnp.int32, (sub, chunk), 0)
    for c in range(n_chunks):
        r0 = c * chunk
        bc = b[r0:r0 + chunk]
        c_rows = pl.ds(pl.multiple_of(base + r0, chunk), chunk)
        qc = q_ref[c_rows, :].astype(F32) * q_scale
        kc = k_ref[c_rows, :].astype(F32)
        vc = v_ref[c_rows, :]
        state = state_ref[...]
        inter = _dot((qc * jnp.exp(bc)).astype(BF16), state.astype(BF16))
        bands = []
        for i in range(n_sub):
            lo, hi = i * sub, (i + 1) * sub
            qs, ks, bs = qc[lo:hi], kc[lo:hi], bc[lo:hi]
            if i == 0:
                s = jnp.zeros((sub, chunk), F32)
            else:
                ref = bc[lo - 1:lo]
                qi = (qs * jnp.exp(bs - ref)).astype(BF16)
                kdec = jnp.exp(jnp.minimum(ref - bc, 0.0))
                ki = jnp.where(key_row < lo, kc * kdec, 0.0).astype(BF16)
                s = _dot_nt(qi, ki)
            for s_ in range(sub):
                dec = jnp.exp(jnp.minimum(bs - bs[s_:s_ + 1], 0.0))
                col = jnp.sum(qs * ks[s_:s_ + 1] * dec, axis=1, keepdims=True)
                s = jnp.where((key_lane == lo + s_) & (band_row >= s_), col, s)
            bands.append(s)
        scores = jnp.concatenate(bands, axis=0)
        osc_ref[r0:r0 + chunk, :] = inter + _dot(scores.astype(BF16), vc)

        bl = bc[chunk - 1:chunk]
        kd = kc * jnp.exp(bl - bc)
        stacked = jnp.concatenate([kd, jnp.broadcast_to(jnp.exp(bl), (chunk, dk))], axis=0)
        st = stacked.T
        vpad = jnp.concatenate([vc, jnp.zeros_like(vc)], axis=0)
        state_ref[...] = state * st[:, chunk:chunk + 1] + _dot(st.astype(BF16), vpad)

    o = osc_ref[...]
    ms = jnp.mean(o * o, axis=-1, keepdims=True)
    y = o * lax.rsqrt(ms + EPS) * gn_ref[...]
    g = gg_ref[blk_rows, :].astype(F32)
    o_ref[blk_rows, :] = (y * (g * _sigmoid(g))).astype(o_ref.dtype)


def _gla(proj, gates, wup_pad, b_gate, g_norm, *, batch, seq, heads, dk, dv, rows_blk=256):
    t = proj.shape[0]
    kb0 = heads
    vb0 = 2 * heads * dk // dv
    gb0 = vb0 + heads
    kern = functools.partial(_gla_kernel, rows_blk=rows_blk, chunk=GLA_CHUNK, sub=GLA_SUB)
    return pl.pallas_call(
        kern,
        grid=(batch, heads),
        in_specs=[pl.BlockSpec((seq, dk), lambda b, h: (b, h)),
                  pl.BlockSpec((seq, dk), lambda b, h: (b, kb0 + h)),
                  pl.BlockSpec((seq, dv), lambda b, h: (b, vb0 + h)),
                  pl.BlockSpec((seq, dv), lambda b, h: (b, gb0 + h)),
                  pl.BlockSpec((seq, gates.shape[1]), lambda b, h: (b, 0)),
                  pl.BlockSpec((wup_pad.shape[0], dk), lambda b, h: (0, h)),
                  pl.BlockSpec((1, dk), lambda b, h: (0, h)),
                  pl.BlockSpec((1, dv), lambda b, h: (0, 0))],
        out_specs=pl.BlockSpec((seq, dv), lambda b, h: (b, h)),
        out_shape=jax.ShapeDtypeStruct((t, heads * dv), BF16),
        scratch_shapes=[pltpu.VMEM((dk, dv), F32),
                        pltpu.VMEM((rows_blk, dv), F32)],
        compiler_params=_params("parallel", "parallel"),
        name="gla",
    )(proj, proj, proj, proj, gates, wup_pad, b_gate, g_norm)


def _rglru_kernel(rx_ref, ry_ref, cw_ref, cb_ref, wx_ref, bx_ref, wa_ref, ba_ref, lam_ref,
                  o_ref, *, rows_t):
    seq, w = rx_ref.shape
    halo = 8
    wxb = wx_ref[...].astype(BF16)
    wab = wa_ref[...].astype(BF16)
    cw = cw_ref[...]
    neg_lam = -lam_ref[...]
    softplus = jnp.maximum(neg_lam, 0.0) + _softplus_neg_abs(neg_lam)
    coef = -RG_C * softplus
    row = lax.broadcasted_iota(jnp.int32, (rows_t, 1), 0)

    def body(t, carry):
        h_prev, tail = carry
        rows = pl.ds(pl.multiple_of(t * rows_t, rows_t), rows_t)
        x = rx_ref[rows, :].astype(F32)
        xe = jnp.concatenate([tail, x], axis=0)
        xc = cb_ref[...] + cw[RG_CONV - 1:RG_CONV] * x
        for s_ in range(1, RG_CONV):
            xc = xc + cw[RG_CONV - 1 - s_:RG_CONV - s_] * pltpu.roll(xe, s_, axis=0)[halo:]
        xcb = xc.astype(BF16)
        gate_x = _sigmoid(_dot(xcb, wxb) + bx_ref[...])
        gate_a = _sigmoid(_dot(xcb, wab) + ba_ref[...])
        log_a = coef * gate_a
        a_dec = jnp.exp(log_a)
        one_m_a2 = -jnp.tanh(log_a) * (a_dec * a_dec + 1.0)
        u = jnp.sqrt(one_m_a2) * (gate_x * xc)
        d = 1
        while d < rows_t:
            if d < halo:
                a_sh = jnp.where(row >= d, pltpu.roll(a_dec, d, axis=0), 1.0)
                u_sh = jnp.where(row >= d, pltpu.roll(u, d, axis=0), 0.0)
            else:
                a_sh = jnp.concatenate([jnp.ones((d, w), F32), a_dec[:rows_t - d]], axis=0)
                u_sh = jnp.concatenate([jnp.zeros((d, w), F32), u[:rows_t - d]], axis=0)
            u = a_dec * u_sh + u
            a_dec = a_dec * a_sh
            d *= 2
        h = u + a_dec * h_prev
        y = ry_ref[rows, :].astype(F32)
        o_ref[rows, :] = (h * _gelu_tanh(y)).astype(o_ref.dtype)
        return h[rows_t - 1:rows_t], x[rows_t - halo:]

    init = (jnp.zeros((1, w), F32), jnp.zeros((halo, w), F32))
    lax.fori_loop(0, seq // rows_t, body, init)


def _rglru(proj, conv_w, conv_b, w_x, b_x, w_a, b_a, lam, *, batch, seq, width, rows_t=256):
    t = proj.shape[0]
    wb = RG_BLOCK
    nb = width // wb
    kern = functools.partial(_rglru_kernel, rows_t=rows_t)
    vec = lambda v: v.reshape(1, width)
    vspec = pl.BlockSpec((1, wb), lambda b, g: (0, g))
    mspec = pl.BlockSpec((None, wb, wb), lambda b, g: (g, 0, 0))
    return pl.pallas_call(
        kern,
        grid=(batch, nb),
        in_specs=[pl.BlockSpec((seq, wb), lambda b, g: (b, nb + g)),
                  pl.BlockSpec((seq, wb), lambda b, g: (b, g)),
                  pl.BlockSpec((RG_CONV, wb), lambda b, g: (0, g)),
                  vspec, mspec, vspec, mspec, vspec, vspec],
        out_specs=pl.BlockSpec((seq, wb), lambda b, g: (b, g)),
        out_shape=jax.ShapeDtypeStruct((t, width), BF16),
        compiler_params=_params("parallel", "parallel"),
        name="rglru",
    )(proj, proj, conv_w, vec(conv_b), w_x, vec(b_x), w_a, vec(b_a), vec(lam))


def _ffn(h, g, w_gate, w_up, w_down, layer):
    u = _rmsnorm(h, g, BF16)
    hid = _matmul_as_swiglu(u, w_gate, w_up, layer, tm=TM_AS, tn=TN_SWIGLU)
    return _matmul_as_res(hid, w_down, layer, h, tm=TM_DOWN, tn=TN_DOWN)


def _even_mixer(h, g, w_in, i, w_gate_up, b_gate, g_norm, b_f, w_out, *, batch, seq):
    d = h.shape[1]
    dk, dv = d // 16, d // 8
    qk, vw, fw = GLA_HEADS * dk, GLA_HEADS * dv, d // 2
    fox_heads = fw // FOX_HEAD_DIM
    rank = GLA_GATE_RANK
    main0 = 2 * qk + 2 * vw
    main1 = 3 * fw
    wt_in = jnp.swapaxes(w_in, 1, 2)
    u, gates = _gate_proj_t(h, g, wt_in, i, col_a=main0, n_a=rank,
                            col_b=main0 + rank + main1, n_b=fox_heads)
    proj = _in_proj_as_t(u, wt_in, i, tm=TM_AS, tn=TN, n_plain=main0 // TN,
                         n_total=(main0 + main1) // TN, shift=rank)
    bias_row = jnp.zeros((1, LANES), F32).at[0, rank:rank + fox_heads].set(b_f)
    c, ct = _fox_cumsum(gates, bias_row, batch=batch, seq=seq, lane0=rank, heads=fox_heads)
    ck = ct.reshape(batch, fox_heads, seq // FOX_TQ, 1, FOX_TQ)
    o_fox = _fox_attention(proj, c, ck, batch=batch, seq=seq, heads=fox_heads, col0=main0,
                           lane0=rank, tq=FOX_TQ)
    wup_pad = jnp.pad(w_gate_up, ((0, LANES - rank), (0, 0)))
    o_gla = _gla(proj, gates, wup_pad, b_gate.reshape(1, qk), g_norm.reshape(1, dv),
                 batch=batch, seq=seq, heads=GLA_HEADS, dk=dk, dv=dv)
    return _matmul_res2(o_gla, o_fox, w_out, i, h, tm=TM_OUT, tn=TN)


def _odd_mixer(h, g, w_in, i, conv_w, conv_b, w_x, b_x, w_a, b_a, lam, w_out, *, batch, seq):
    d = h.shape[1]
    width = d // 2
    sb_heads = width // SB_HEAD_DIM
    u = _rmsnorm(h, g, BF16)
    proj = _matmul_as(u, w_in, i, tm=TM_AS, tn=TN, out_dtype=BF16)
    o_rg = _rglru(proj, conv_w, conv_b, w_x, b_x, w_a, b_a, lam, batch=batch, seq=seq,
                  width=width)
    o_sb = _sb_attention(proj, batch=batch, seq=seq, heads=sb_heads, col0=2 * width,
                         tq=SB_TQ, tk=SB_TK)
    return _matmul_res2(o_rg, o_sb, w_out, i, h, tm=TM_OUT, tn=TN)


def kernel(x, norm_mix, norm_ffn, ffn_w_gate, ffn_w_up, ffn_w_down, ab_w_in, gla_w_gate_up,
           gla_b_gate, gla_norm, fox_b_f, ab_w_out, cd_w_in, rg_conv_w, rg_conv_b, rg_w_x,
           rg_b_x, rg_w_a, rg_b_a, rg_lambda, cd_w_out, final_norm):
    batch, seq, d = x.shape
    h = x.reshape(batch * seq, d)
    for layer in range(norm_mix.shape[0]):
        i = layer // 2
        if layer % 2 == 0:
            h = _even_mixer(h, norm_mix[layer], ab_w_in, i, gla_w_gate_up[i], gla_b_gate[i],
                            gla_norm[i], fox_b_f[i], ab_w_out, batch=batch, seq=seq)
        else:
            h = _odd_mixer(h, norm_mix[layer], cd_w_in, i, rg_conv_w[i], rg_conv_b[i],
                           rg_w_x[i], rg_b_x[i], rg_w_a[i], rg_b_a[i], rg_lambda[i],
                           cd_w_out, batch=batch, seq=seq)
        h = _ffn(h, norm_ffn[layer], ffn_w_gate, ffn_w_up, ffn_w_down, layer)
    return _rmsnorm(h, final_norm, F32).reshape(batch, seq, d)
```

```python
import functools
import math

import jax
import jax.numpy as jnp
from jax import lax
from jax.experimental import pallas as pl
from jax.experimental.pallas import tpu as pltpu

F32 = jnp.float32
BF16 = jnp.bfloat16

EPS = 1e-6
LANES = 128
VMEM_LIMIT = 56 * 1024 * 1024

GLA_HEADS = 4
GLA_GATE_RANK = 16
GLA_GATE_TAU = 16.0
GLA_CHUNK = 64
GLA_SUB = 8
FOX_HEAD_DIM = 128
SB_HEAD_DIM = 128
RG_BLOCK = 256
RG_CONV = 4
RG_C = 8.0

TM_AS = 2048
TN = 512
TN_SWIGLU = 256
TM_DOWN, TN_DOWN = 1024, 256
TM_OUT = 1024
FOX_TQ = 1024
SB_TQ, SB_TK = 2048, 256

NEG_BIG = -1e30
_NT = (((1,), (1,)), ((), ()))


def _params(*sem):
    return pltpu.CompilerParams(dimension_semantics=sem, vmem_limit_bytes=VMEM_LIMIT)


def _dot(a, b):
    return jnp.dot(a, b, preferred_element_type=F32)


def _dot_nt(a, b):
    return lax.dot_general(a, b, _NT, preferred_element_type=F32)


def _split_bf16(x, n):
    parts = []
    r = x
    for _ in range(n):
        p = r.astype(BF16)
        parts.append(p)
        r = r - p.astype(F32)
    return parts


def _dot_f32(a, b):
    ah, al = _split_bf16(a, 2)
    bh, bl = _split_bf16(b, 2)
    return _dot(ah, bh) + _dot(ah, bl) + _dot(al, bh)


def _dot_exact_lhs(m01, x, n=3):
    acc = None
    for p in _split_bf16(x, n):
        t = _dot(m01, p)
        acc = t if acc is None else acc + t
    return acc


def _softplus_neg_abs(z):
    return jnp.log1p(jnp.exp(-jnp.abs(z)))


def _log_sigmoid(z):
    return jnp.minimum(z, 0.0) - _softplus_neg_abs(z)


def _sigmoid(z):
    return 1.0 / (1.0 + jnp.exp(-z))


def _gelu_tanh(y):
    c = math.sqrt(2.0 / math.pi)
    return y * (0.5 * (1.0 + jnp.tanh(c * (y + 0.044715 * (y * y * y)))))


def _rmsnorm_kernel(x_ref, g_ref, o_ref):
    x = x_ref[...]
    ms = jnp.mean(x * x, axis=-1, keepdims=True)
    o_ref[...] = (x * lax.rsqrt(ms + EPS) * g_ref[...]).astype(o_ref.dtype)


def _rmsnorm(x, g, out_dtype, tm=512):
    t, d = x.shape
    return pl.pallas_call(
        _rmsnorm_kernel,
        grid=(t // tm,),
        in_specs=[pl.BlockSpec((tm, d), lambda i: (i, 0)),
                  pl.BlockSpec((1, d), lambda i: (0, 0))],
        out_specs=pl.BlockSpec((tm, d), lambda i: (i, 0)),
        out_shape=jax.ShapeDtypeStruct((t, d), out_dtype),
        compiler_params=_params("parallel"),
        name="rmsnorm",
    )(x, g.reshape(1, d))


def _gate_proj_t_kernel(x_ref, g_ref, wa_ref, wb_ref, u_ref, o_ref):
    x = x_ref[...]
    ms = jnp.mean(x * x, axis=-1, keepdims=True)
    u = x * lax.rsqrt(ms + EPS) * g_ref[...]
    u_ref[...] = u.astype(u_ref.dtype)
    used = wa_ref.shape[0] + wb_ref.shape[0]
    wt = jnp.concatenate([wa_ref[...], wb_ref[...],
                          jnp.zeros((o_ref.shape[1] - used, x.shape[1]), F32)], axis=0)
    uh, ul = _split_bf16(u, 2)
    wh, wl = _split_bf16(wt, 2)
    o_ref[...] = _dot_nt(uh, wh) + _dot_nt(uh, wl) + _dot_nt(ul, wh)


def _gate_proj_t(x, g, wt, layer, *, col_a, n_a, col_b, n_b, tm=256):
    t, d = x.shape
    assert col_a % n_a == 0 and col_b % n_b == 0
    return pl.pallas_call(
        _gate_proj_t_kernel,
        grid=(t // tm,),
        in_specs=[pl.BlockSpec((tm, d), lambda i: (i, 0)),
                  pl.BlockSpec((1, d), lambda i: (0, 0)),
                  pl.BlockSpec((None, n_a, d), lambda i: (layer, col_a // n_a, 0)),
                  pl.BlockSpec((None, n_b, d), lambda i: (layer, col_b // n_b, 0))],
        out_specs=[pl.BlockSpec((tm, d), lambda i: (i, 0)),
                   pl.BlockSpec((tm, LANES), lambda i: (i, 0))],
        out_shape=[jax.ShapeDtypeStruct((t, d), BF16),
                   jax.ShapeDtypeStruct((t, LANES), F32)],
        compiler_params=_params("parallel"),
        name="gate_proj_t",
    )(x, g.reshape(1, d), wt, wt)


def _mm_as_kernel(a_ref, w_ref, o_ref):
    o_ref[...] = _dot(a_ref[...], w_ref[...].astype(BF16)).astype(o_ref.dtype)


def _mm_as_swiglu_kernel(a_ref, wg_ref, wu_ref, o_ref):
    a = a_ref[...]
    g = _dot(a, wg_ref[...].astype(BF16))
    u = _dot(a, wu_ref[...].astype(BF16))
    o_ref[...] = (g * _sigmoid(g) * u).astype(o_ref.dtype)


def _mm_as_res_kernel(a_ref, w_ref, r_ref, o_ref):
    o_ref[...] = r_ref[...] + _dot(a_ref[...], w_ref[...].astype(BF16))


def _a_spec(tm, k):
    return pl.BlockSpec((tm, k), lambda i, j: (i, 0), pipeline_mode=pl.Buffered(1))


def _wt_spec(layer, k, tn):
    return pl.BlockSpec((None, k, tn), lambda i, j: (layer, 0, j))


def _matmul_as(a, w, layer, *, tm, tn, out_dtype):
    m, k = a.shape
    n = w.shape[2]
    return pl.pallas_call(
        _mm_as_kernel,
        grid=(m // tm, n // tn),
        in_specs=[_a_spec(tm, k), _wt_spec(layer, k, tn)],
        out_specs=pl.BlockSpec((tm, tn), lambda i, j: (i, j)),
        out_shape=jax.ShapeDtypeStruct((m, n), out_dtype),
        compiler_params=_params("parallel", "arbitrary"),
        name="matmul_as",
    )(a, w)


def _matmul_as_swiglu(a, wg, wu, layer, *, tm, tn):
    m, k = a.shape
    n = wg.shape[2]
    return pl.pallas_call(
        _mm_as_swiglu_kernel,
        grid=(m // tm, n // tn),
        in_specs=[_a_spec(tm, k), _wt_spec(layer, k, tn), _wt_spec(layer, k, tn)],
        out_specs=pl.BlockSpec((tm, tn), lambda i, j: (i, j)),
        out_shape=jax.ShapeDtypeStruct((m, n), BF16),
        compiler_params=_params("parallel", "arbitrary"),
        name="matmul_as_swiglu",
    )(a, wg, wu)


def _matmul_as_res(a, w, layer, res, *, tm, tn):
    m, k = a.shape
    n = w.shape[2]
    return pl.pallas_call(
        _mm_as_res_kernel,
        grid=(m // tm, n // tn),
        in_specs=[_a_spec(tm, k), _wt_spec(layer, k, tn),
                  pl.BlockSpec((tm, tn), lambda i, j: (i, j))],
        out_specs=pl.BlockSpec((tm, tn), lambda i, j: (i, j)),
        out_shape=jax.ShapeDtypeStruct((m, n), F32),
        compiler_params=_params("parallel", "arbitrary"),
        name="matmul_as_res",
    )(a, w, res)


def _in_proj_as_t_kernel(a_ref, wt_ref, wtx_ref, o_ref, *, n_plain):
    j = pl.program_id(1)
    shift = wtx_ref.shape[0]

    @pl.when(j < n_plain)
    def _():
        o_ref[...] = _dot_nt(a_ref[...], wt_ref[...].astype(BF16)).astype(o_ref.dtype)

    @pl.when(j >= n_plain)
    def _():
        wt = jnp.concatenate([wt_ref[shift:, :], wtx_ref[...]], axis=0).astype(BF16)
        o_ref[...] = _dot_nt(a_ref[...], wt).astype(o_ref.dtype)


def _in_proj_as_t(a, wt, layer, *, tm, tn, n_plain, n_total, shift):
    m, k = a.shape
    assert tn % shift == 0
    kern = functools.partial(_in_proj_as_t_kernel, n_plain=n_plain)
    return pl.pallas_call(
        kern,
        grid=(m // tm, n_total),
        in_specs=[_a_spec(tm, k),
                  pl.BlockSpec((None, tn, k), lambda i, j: (layer, j, 0)),
                  pl.BlockSpec((None, shift, k), lambda i, j: (layer, (j + 1) * (tn // shift), 0))],
        out_specs=pl.BlockSpec((tm, tn), lambda i, j: (i, j)),
        out_shape=jax.ShapeDtypeStruct((m, n_total * tn), BF16),
        compiler_params=_params("parallel", "arbitrary"),
        name="in_proj_as_t",
    )(a, wt, wt)


_CAST_ROWS = 512


def _cast_rows(w_ref, wb_ref):
    k = w_ref.shape[0]

    def body(r, c):
        rows = pl.ds(pl.multiple_of(r * _CAST_ROWS, _CAST_ROWS), _CAST_ROWS)
        wb_ref[rows, :] = w_ref[rows, :].astype(BF16)
        return c

    lax.fori_loop(0, k // _CAST_ROWS, body, 0)


def _mm_res2_kernel(a1_ref, a2_ref, w_ref, r_ref, o_ref, wb_ref):
    @pl.when(pl.program_id(1) == 0)
    def _():
        _cast_rows(w_ref, wb_ref)

    k1 = a1_ref.shape[1]
    o_ref[...] = (r_ref[...] + _dot(a1_ref[...], wb_ref[:k1, :])
                  + _dot(a2_ref[...], wb_ref[k1:, :]))


def _matmul_res2(a1, a2, w, layer, res, *, tm, tn):
    m, k1 = a1.shape
    k2 = a2.shape[1]
    k, n = w.shape[1:]
    assert k == k1 + k2
    return pl.pallas_call(
        _mm_res2_kernel,
        grid=(n // tn, m // tm),
        in_specs=[pl.BlockSpec((tm, k1), lambda j, i: (i, 0)),
                  pl.BlockSpec((tm, k2), lambda j, i: (i, 0)),
                  pl.BlockSpec((None, k, tn), lambda j, i: (layer, 0, j)),
                  pl.BlockSpec((tm, tn), lambda j, i: (i, j))],
        out_specs=pl.BlockSpec((tm, tn), lambda j, i: (i, j)),
        out_shape=jax.ShapeDtypeStruct((m, n), F32),
        scratch_shapes=[pltpu.VMEM((k, tn), BF16)],
        compiler_params=_params("arbitrary", "arbitrary"),
        name="matmul_res2",
    )(a1, a2, w, res)


def _fox_cumsum_kernel(g_ref, b_ref, c_ref, ct_ref, c_sc, *, blk, lane0, heads):
    s = g_ref.shape[0]
    row = lax.broadcasted_iota(jnp.int32, (blk, blk), 0)
    col = lax.broadcasted_iota(jnp.int32, (blk, blk), 1)
    tri = (col <= row).astype(BF16)
    carry = jnp.zeros((1, g_ref.shape[1]), F32)
    for n in range(s // blk):
        rows = slice(n * blk, (n + 1) * blk)
        lf = _log_sigmoid(g_ref[rows, :] + b_ref[...])
        c = _dot_exact_lhs(tri, lf) + carry
        c_sc[rows, :] = c
        carry = c[blk - 1:blk, :]
    c_all = c_sc[...]
    c_ref[...] = c_all
    ct_ref[...] = c_all.T[lane0:lane0 + heads, :]


def _fox_cumsum(gates, bias_row, *, batch, seq, lane0, heads, blk=256):
    t, n = gates.shape
    kern = functools.partial(_fox_cumsum_kernel, blk=blk, lane0=lane0, heads=heads)
    return pl.pallas_call(
        kern,
        grid=(batch,),
        in_specs=[pl.BlockSpec((seq, n), lambda b: (b, 0)),
                  pl.BlockSpec((1, n), lambda b: (0, 0))],
        out_specs=[pl.BlockSpec((seq, n), lambda b: (b, 0)),
                   pl.BlockSpec((None, heads, seq), lambda b: (b, 0, 0))],
        out_shape=[jax.ShapeDtypeStruct((t, n), F32),
                   jax.ShapeDtypeStruct((batch, heads, seq), F32)],
        scratch_shapes=[pltpu.VMEM((seq, n), F32)],
        compiler_params=_params("parallel"),
        name="fox_cumsum",
    )(gates, bias_row)


def _fox_kernel(q_ref, k_ref, v_ref, c_ref, ck_ref, o_ref, *, tq, lane0):
    h = pl.program_id(1)
    for qi in range(q_ref.shape[0] // tq):
        _fox_q_block(qi, h, q_ref, k_ref, v_ref, c_ref, ck_ref, o_ref, tq=tq, lane0=lane0)


def _fox_q_block(qi, h, q_ref, k_ref, v_ref, c_ref, ck_ref, o_ref, *, tq, lane0):
    dh = q_ref.shape[1]
    scale = dh ** -0.5
    q_rows = pl.ds(qi * tq, tq)
    q = q_ref[q_rows, :]
    c_blk = c_ref[q_rows, :]
    lane = lax.broadcasted_iota(jnp.int32, c_blk.shape, 1)
    cq = jnp.sum(jnp.where(lane == lane0 + h, c_blk, 0.0), axis=1, keepdims=True)

    def logits(kb):
        rows = pl.ds(kb * tq, tq)
        s = _dot_nt(q, k_ref[rows, :]) * scale
        return s + cq - ck_ref[kb], v_ref[rows, :]

    s, v = logits(qi)
    tpos = lax.broadcasted_iota(jnp.int32, (tq, tq), 0)
    spos = lax.broadcasted_iota(jnp.int32, (tq, tq), 1)
    s = jnp.where(spos <= tpos, s, NEG_BIG)
    m0 = jnp.max(s, axis=1, keepdims=True)
    p = jnp.exp(s - m0)
    l0 = jnp.sum(p, axis=1, keepdims=True)
    acc0 = _dot(p.astype(BF16), v)

    def body(kb, carry):
        m, l, acc = carry
        s, v = logits(kb)
        m_new = jnp.maximum(m, jnp.max(s, axis=1, keepdims=True))
        alpha = jnp.exp(m - m_new)
        p = jnp.exp(s - m_new)
        l = alpha * l + jnp.sum(p, axis=1, keepdims=True)
        acc = alpha * acc + _dot(p.astype(BF16), v)
        return m_new, l, acc

    carry = (m0, l0, acc0)
    for kb in range(qi):
        carry = body(kb, carry)
    _, l, acc = carry
    o_ref[q_rows, :] = (acc / l).astype(o_ref.dtype)


def _fox_attention(proj, c, ck, *, batch, seq, heads, col0, lane0, tq):
    t = proj.shape[0]
    dh = FOX_HEAD_DIM
    nq = seq // tq
    qb = col0 // dh
    kern = functools.partial(_fox_kernel, tq=tq, lane0=lane0)
    return pl.pallas_call(
        kern,
        grid=(batch, heads),
        in_specs=[pl.BlockSpec((seq, dh), lambda b, h: (b, qb + h)),
                  pl.BlockSpec((seq, dh), lambda b, h: (b, qb + heads + h)),
                  pl.BlockSpec((seq, dh), lambda b, h: (b, qb + 2 * heads + h)),
                  pl.BlockSpec((seq, c.shape[1]), lambda b, h: (b, 0)),
                  pl.BlockSpec((None, None, nq, 1, tq), lambda b, h: (b, h, 0, 0, 0))],
        out_specs=pl.BlockSpec((seq, dh), lambda b, h: (b, h)),
        out_shape=jax.ShapeDtypeStruct((t, heads * dh), BF16),
        compiler_params=_params("parallel", "parallel"),
        name="fox_attention",
    )(proj, proj, proj, c, ck)


def _sb_kernel(q_ref, k_ref, v_ref, o_ref, *, tq, tk):
    for qi in range(q_ref.shape[0] // tq):
        _sb_q_block(qi, q_ref, k_ref, v_ref, o_ref, tq=tq, tk=tk)


def _sb_q_block(qi, q_ref, k_ref, v_ref, o_ref, *, tq, tk):
    dh = q_ref.shape[1]
    scale = dh ** -0.5
    q_rows = pl.ds(qi * tq, tq)
    q = q_ref[q_rows, :]
    nd = tq // tk
    jj = lax.broadcasted_iota(jnp.int32, (tk, tk), 0)
    ss = lax.broadcasted_iota(jnp.int32, (tk, tk), 1)
    upper = (jj > ss).astype(BF16)
    reps = tk // dh

    def block(q_rows_, kb, run, mask):
        rows = pl.ds(kb * tk, tk)
        z = _dot_nt(q_rows_, k_ref[rows, :]) * scale
        sp = jnp.log(1.0 + jnp.exp(-jnp.abs(z)))
        log_beta = jnp.minimum(z, 0.0) - sp
        log_1m = log_beta - z
        if mask is not None:
            log_1m = jnp.where(mask, log_1m, 0.0)
        hi, lo = _split_bf16(log_1m, 2)
        suffix = _dot(hi, upper) + _dot(lo, upper)
        a = jnp.exp(log_beta + suffix + jnp.concatenate([run] * reps, axis=1))
        if mask is not None:
            a = jnp.where(mask, a, 0.0)
        pv = _dot(a.astype(BF16), v_ref[rows, :])
        rs = jnp.sum(log_1m, axis=1, keepdims=True)
        return pv, jnp.broadcast_to(rs, (rs.shape[0], dh))

    run = acc = None
    for d in range(nd):
        off = (nd - 1 - d) * tk
        n_rows = tq - off
        run_rows = jnp.zeros((n_rows, dh), F32) if run is None else run[off:]
        tri = (lax.broadcasted_iota(jnp.int32, (n_rows, tk), 1)
               < lax.broadcasted_iota(jnp.int32, (n_rows, tk), 0))
        pv, rs = block(q[off:], qi * nd + nd - 1 - d, run_rows, tri)
        if off:
            pv = jnp.concatenate([jnp.zeros((off, dh), F32), pv], axis=0)
            rs = jnp.concatenate([jnp.zeros((off, dh), F32), rs], axis=0)
        acc = pv if acc is None else acc + pv
        run = rs if run is None else run + rs

    for kb in range(qi * nd - 1, -1, -1):
        pv, rs = block(q, kb, run, None)
        acc = acc + pv
        run = run + rs
    o_ref[q_rows, :] = acc.astype(o_ref.dtype)


def _sb_attention(proj, *, batch, seq, heads, col0, tq, tk):
    t = proj.shape[0]
    dh = SB_HEAD_DIM
    qb = col0 // dh
    kern = functools.partial(_sb_kernel, tq=tq, tk=tk)
    return pl.pallas_call(
        kern,
        grid=(batch, heads),
        in_specs=[pl.BlockSpec((seq, dh), lambda b, h: (b, qb + h)),
                  pl.BlockSpec((seq, dh), lambda b, h: (b, qb + heads + h)),
                  pl.BlockSpec((seq, dh), lambda b, h: (b, qb + 2 * heads + h))],
        out_specs=pl.BlockSpec((seq, dh), lambda b, h: (b, h)),
        out_shape=jax.ShapeDtypeStruct((t, heads * dh), BF16),
        compiler_params=_params("parallel", "parallel"),
        name="sb_attention",
    )(proj, proj, proj)


def _gla_kernel(q_ref, k_ref, v_ref, gg_ref, gate_ref, wup_ref, bg_ref, gn_ref, o_ref,
                state_ref, osc_ref, *, rows_blk, chunk, sub):
    state_ref[...] = jnp.zeros_like(state_ref)

    def row_block(r, carry):
        _gla_row_block(pl.multiple_of(r * rows_blk, rows_blk), q_ref, k_ref, v_ref, gg_ref,
                       gate_ref, wup_ref, bg_ref, gn_ref, o_ref, state_ref, osc_ref,
                       rows_blk=rows_blk, chunk=chunk, sub=sub)
        return carry

    lax.fori_loop(0, q_ref.shape[0] // rows_blk, row_block, 0)


def _gla_row_block(base, q_ref, k_ref, v_ref, gg_ref, gate_ref, wup_ref, bg_ref, gn_ref, o_ref,
                   state_ref, osc_ref, *, rows_blk, chunk, sub):
    dk = q_ref.shape[1]
    n_chunks = rows_blk // chunk
    n_sub = chunk // sub
    q_scale = dk ** -0.5
    blk_rows = pl.ds(base, rows_blk)

    z = _dot_f32(gate_ref[blk_rows, :], wup_ref[...]) + bg_ref[...]
    a = _log_sigmoid(z) / GLA_GATE_TAU
    row = lax.broadcasted_iota(jnp.int32, (rows_blk, rows_blk), 0)
    col = lax.broadcasted_iota(jnp.int32, (rows_blk, rows_blk), 1)
    shift = int(math.log2(chunk))
    same = lax.shift_right_logical(row, shift) == lax.shift_right_logical(col, shift)
    tri = ((col <= row) & same).astype(BF16)
    b = _dot_exact_lhs(tri, a)

    key_row = lax.broadcasted_iota(jnp.int32, (chunk, 1), 0)
    key_lane = lax.broadcasted_iota(jnp.int32, (sub, chunk), 1)
    band_row = lax.broadcasted_iota(jnp.int32, (sub, chunk), 0)
    for c in range(n_chunks):
        r0 = c * chunk
        bc = b[r0:r0 + chunk]
        c_rows = pl.ds(pl.multiple_of(base + r0, chunk), chunk)
        qc = q_ref[c_rows, :].astype(F32) * q_scale
        kc = k_ref[c_rows, :].astype(F32)
        vc = v_ref[c_rows, :]
        state = state_ref[...]
        inter = _dot((qc * jnp.exp(bc)).astype(BF16), state.astype(BF16))
        bands = []
        for i in range(n_sub):
            lo, hi = i * sub, (i + 1) * sub
            qs, ks, bs = qc[lo:hi], kc[lo:hi], bc[lo:hi]
            if i == 0:
                s = jnp.zeros((sub, chunk), F32)
            else:
                ref = bc[lo - 1:lo]
                qi = (qs * jnp.exp(bs - ref)).astype(BF16)
                kdec = jnp.exp(jnp.minimum(ref - bc, 0.0))
                ki = jnp.where(key_row < lo, kc * kdec, 0.0).astype(BF16)
                s = _dot_nt(qi, ki)
            for s_ in range(sub):
                dec = jnp.exp(jnp.minimum(bs - bs[s_:s_ + 1], 0.0))
                col = jnp.sum(qs * ks[s_:s_ + 1] * dec, axis=1, keepdims=True)
                s = jnp.where((key_lane == lo + s_) & (band_row >= s_), col, s)
            bands.append(s)
        scores = jnp.concatenate(bands, axis=0)
        osc_ref[r0:r0 + chunk, :] = inter + _dot(scores.astype(BF16), vc)

        bl = bc[chunk - 1:chunk]
        kd = kc * jnp.exp(bl - bc)
        stacked = jnp.concatenate([kd, jnp.broadcast_to(jnp.exp(bl), (chunk, dk))], axis=0)
        st = stacked.T
        vpad = jnp.concatenate([vc, jnp.zeros_like(vc)], axis=0)
        state_ref[...] = state * st[:, chunk:chunk + 1] + _dot(st.astype(BF16), vpad)

    o = osc_ref[...]
    ms = jnp.mean(o * o, axis=-1, keepdims=True)
    y = o * lax.rsqrt(ms + EPS) * gn_ref[...]
    g = gg_ref[blk_rows, :].astype(F32)
    o_ref[blk_rows, :] = (y * (g * _sigmoid(g))).astype(o_ref.dtype)


def _gla(proj, gates, wup_pad, b_gate, g_norm, *, batch, seq, heads, dk, dv, rows_blk=256):
    t = proj.shape[0]
    kb0 = heads
    vb0 = 2 * heads * dk // dv
    gb0 = vb0 + heads
    kern = functools.partial(_gla_kernel, rows_blk=rows_blk, chunk=GLA_CHUNK, sub=GLA_SUB)
    return pl.pallas_call(
        kern,
        grid=(batch, heads),
        in_specs=[pl.BlockSpec((seq, dk), lambda b, h: (b, h)),
                  pl.BlockSpec((seq, dk), lambda b, h: (b, kb0 + h)),
                  pl.BlockSpec((seq, dv), lambda b, h: (b, vb0 + h)),
                  pl.BlockSpec((seq, dv), lambda b, h: (b, gb0 + h)),
                  pl.BlockSpec((seq, gates.shape[1]), lambda b, h: (b, 0)),
                  pl.BlockSpec((wup_pad.shape[0], dk), lambda b, h: (0, h)),
                  pl.BlockSpec((1, dk), lambda b, h: (0, h)),
                  pl.BlockSpec((1, dv), lambda b, h: (0, 0))],
        out_specs=pl.BlockSpec((seq, dv), lambda b, h: (b, h)),
        out_shape=jax.ShapeDtypeStruct((t, heads * dv), BF16),
        scratch_shapes=[pltpu.VMEM((dk, dv), F32),
                        pltpu.VMEM((rows_blk, dv), F32)],
        compiler_params=_params("parallel", "parallel"),
        name="gla",
    )(proj, proj, proj, proj, gates, wup_pad, b_gate, g_norm)


def _rglru_kernel(rx_ref, ry_ref, cw_ref, cb_ref, wx_ref, bx_ref, wa_ref, ba_ref, lam_ref,
                  o_ref, *, rows_t):
    seq, w = rx_ref.shape
    halo = 8
    wxb = wx_ref[...].astype(BF16)
    wab = wa_ref[...].astype(BF16)
    cw = cw_ref[...]
    neg_lam = -lam_ref[...]
    softplus = jnp.maximum(neg_lam, 0.0) + _softplus_neg_abs(neg_lam)
    coef = -RG_C * softplus
    row = lax.broadcasted_iota(jnp.int32, (rows_t, 1), 0)

    def body(t, carry):
        h_prev, tail = carry
        rows = pl.ds(pl.multiple_of(t * rows_t, rows_t), rows_t)
        x = rx_ref[rows, :].astype(F32)
        xe = jnp.concatenate([tail, x], axis=0)
        xc = cb_ref[...] + cw[RG_CONV - 1:RG_CONV] * x
        for s_ in range(1, RG_CONV):
            xc = xc + cw[RG_CONV - 1 - s_:RG_CONV - s_] * pltpu.roll(xe, s_, axis=0)[halo:]
        xcb = xc.astype(BF16)
        gate_x = _sigmoid(_dot(xcb, wxb) + bx_ref[...])
        gate_a = _sigmoid(_dot(xcb, wab) + ba_ref[...])
        log_a = coef * gate_a
        a_dec = jnp.exp(log_a)
        one_m_a2 = -jnp.tanh(log_a) * (a_dec * a_dec + 1.0)
        u = jnp.sqrt(one_m_a2) * (gate_x * xc)
        d = 1
        while d < rows_t:
            if d < halo:
                a_sh = jnp.where(row >= d, pltpu.roll(a_dec, d, axis=0), 1.0)
                u_sh = jnp.where(row >= d, pltpu.roll(u, d, axis=0), 0.0)
            else:
                a_sh = jnp.concatenate([jnp.ones((d, w), F32), a_dec[:rows_t - d]], axis=0)
                u_sh = jnp.concatenate([jnp.zeros((d, w), F32), u[:rows_t - d]], axis=0)
            u = a_dec * u_sh + u
            a_dec = a_dec * a_sh
            d *= 2
        h = u + a_dec * h_prev
        y = ry_ref[rows, :].astype(F32)
        o_ref[rows, :] = (h * _gelu_tanh(y)).astype(o_ref.dtype)
        return h[rows_t - 1:rows_t], x[rows_t - halo:]

    init = (jnp.zeros((1, w), F32), jnp.zeros((halo, w), F32))
    lax.fori_loop(0, seq // rows_t, body, init)


def _rglru(proj, conv_w, conv_b, w_x, b_x, w_a, b_a, lam, *, batch, seq, width, rows_t=256):
    t = proj.shape[0]
    wb = RG_BLOCK
    nb = width // wb
    kern = functools.partial(_rglru_kernel, rows_t=rows_t)
    vec = lambda v: v.reshape(1, width)
    vspec = pl.BlockSpec((1, wb), lambda b, g: (0, g))
    mspec = pl.BlockSpec((None, wb, wb), lambda b, g: (g, 0, 0))
    return pl.pallas_call(
        kern,
        grid=(batch, nb),
        in_specs=[pl.BlockSpec((seq, wb), lambda b, g: (b, nb + g)),
                  pl.BlockSpec((seq, wb), lambda b, g: (b, g)),
                  pl.BlockSpec((RG_CONV, wb), lambda b, g: (0, g)),
                  vspec, mspec, vspec, mspec, vspec, vspec],
        out_specs=pl.BlockSpec((seq, wb), lambda b, g: (b, g)),
        out_shape=jax.ShapeDtypeStruct((t, width), BF16),
        compiler_params=_params("parallel", "parallel"),
        name="rglru",
    )(proj, proj, conv_w, vec(conv_b), w_x, vec(b_x), w_a, vec(b_a), vec(lam))


def _ffn(h, g, w_gate, w_up, w_down, layer):
    u = _rmsnorm(h, g, BF16)
    hid = _matmul_as_swiglu(u, w_gate, w_up, layer, tm=TM_AS, tn=TN_SWIGLU)
    return _matmul_as_res(hid, w_down, layer, h, tm=TM_DOWN, tn=TN_DOWN)


def _even_mixer(h, g, w_in, i, w_gate_up, b_gate, g_norm, b_f, w_out, *, batch, seq):
    d = h.shape[1]
    dk, dv = d // 16, d // 8
    qk, vw, fw = GLA_HEADS * dk, GLA_HEADS * dv, d // 2
    fox_heads = fw // FOX_HEAD_DIM
    rank = GLA_GATE_RANK
    main0 = 2 * qk + 2 * vw
    main1 = 3 * fw
    wt_in = jnp.swapaxes(w_in, 1, 2)
    u, gates = _gate_proj_t(h, g, wt_in, i, col_a=main0, n_a=rank,
                            col_b=main0 + rank + main1, n_b=fox_heads)
    proj = _in_proj_as_t(u, wt_in, i, tm=TM_AS, tn=TN, n_plain=main0 // TN,
                         n_total=(main0 + main1) // TN, shift=rank)
    bias_row = jnp.zeros((1, LANES), F32).at[0, rank:rank + fox_heads].set(b_f)
    c, ct = _fox_cumsum(gates, bias_row, batch=batch, seq=seq, lane0=rank, heads=fox_heads)
    ck = ct.reshape(batch, fox_heads, seq // FOX_TQ, 1, FOX_TQ)
    o_fox = _fox_attention(proj, c, ck, batch=batch, seq=seq, heads=fox_heads, col0=main0,
                           lane0=rank, tq=FOX_TQ)
    wup_pad = jnp.pad(w_gate_up, ((0, LANES - rank), (0, 0)))
    o_gla = _gla(proj, gates, wup_pad, b_gate.reshape(1, qk), g_norm.reshape(1, dv),
                 batch=batch, seq=seq, heads=GLA_HEADS, dk=dk, dv=dv)
    return _matmul_res2(o_gla, o_fox, w_out, i, h, tm=TM_OUT, tn=TN)


def _odd_mixer(h, g, w_in, i, conv_w, conv_b, w_x, b_x, w_a, b_a, lam, w_out, *, batch, seq):
    d = h.shape[1]
    width = d // 2
    sb_heads = width // SB_HEAD_DIM
    u = _rmsnorm(h, g, BF16)
    proj = _matmul_as(u, w_in, i, tm=TM_AS, tn=TN, out_dtype=BF16)
    o_rg = _rglru(proj, conv_w, conv_b, w_x, b_x, w_a, b_a, lam, batch=batch, seq=seq,
                  width=width)
    o_sb = _sb_attention(proj, batch=batch, seq=seq, heads=sb_heads, col0=2 * width,
                         tq=SB_TQ, tk=SB_TK)
    return _matmul_res2(o_rg, o_sb, w_out, i, h, tm=TM_OUT, tn=TN)


def kernel(x, norm_mix, norm_ffn, ffn_w_gate, ffn_w_up, ffn_w_down, ab_w_in, gla_w_gate_up,
           gla_b_gate, gla_norm, fox_b_f, ab_w_out, cd_w_in, rg_conv_w, rg_conv_b, rg_w_x,
           rg_b_x, rg_w_a, rg_b_a, rg_lambda, cd_w_out, final_norm):
    batch, seq, d = x.shape
    h = x.reshape(batch * seq, d)
    for layer in range(norm_mix.shape[0]):
        i = layer // 2
        if layer % 2 == 0:
            h = _even_mixer(h, norm_mix[layer], ab_w_in, i, gla_w_gate_up[i], gla_b_gate[i],
                            gla_norm[i], fox_b_f[i], ab_w_out, batch=batch, seq=seq)
        else:
            h = _odd_mixer(h, norm_mix[layer], cd_w_in, i, rg_conv_w[i], rg_conv_b[i],
                           rg_w_x[i], rg_b_x[i], rg_w_a[i], rg_b_a[i], rg_lambda[i],
                           cd_w_out, batch=batch, seq=seq)
        h = _ffn(h, norm_ffn[layer], ffn_w_gate, ffn_w_up, ffn_w_down, layer)
    return _rmsnorm(h, final_norm, F32).reshape(batch, seq, d)
```

```python
import functools
import math

import jax
import jax.numpy as jnp
from jax import lax
from jax.experimental import pallas as pl
from jax.experimental.pallas import tpu as pltpu

F32 = jnp.float32
BF16 = jnp.bfloat16

EPS = 1e-6
LANES = 128
VMEM_LIMIT = 56 * 1024 * 1024

GLA_HEADS = 4
GLA_GATE_RANK = 16
GLA_GATE_TAU = 16.0
GLA_CHUNK = 64
GLA_SUB = 8
FOX_HEAD_DIM = 128
SB_HEAD_DIM = 128
RG_BLOCK = 256
RG_CONV = 4
RG_C = 8.0

TM_AS = 2048
TN = 512
TN_SWIGLU = 256
TM_DOWN, TN_DOWN = 1024, 256
TM_OUT = 1024
FOX_TQ = 1024
SB_TQ, SB_TK = 2048, 256

NEG_BIG = -1e30
_NT = (((1,), (1,)), ((), ()))


def _params(*sem):
    return pltpu.CompilerParams(dimension_semantics=sem, vmem_limit_bytes=VMEM_LIMIT)


def _dot(a, b):
    return jnp.dot(a, b, preferred_element_type=F32)


def _dot_nt(a, b):
    return lax.dot_general(a, b, _NT, preferred_element_type=F32)


def _split_bf16(x, n):
    parts = []
    r = x
    for _ in range(n):
        p = r.astype(BF16)
        parts.append(p)
        r = r - p.astype(F32)
    return parts


def _dot_f32(a, b):
    ah, al = _split_bf16(a, 2)
    bh, bl = _split_bf16(b, 2)
    return _dot(ah, bh) + _dot(ah, bl) + _dot(al, bh)


def _dot_exact_lhs(m01, x, n=3):
    acc = None
    for p in _split_bf16(x, n):
        t = _dot(m01, p)
        acc = t if acc is None else acc + t
    return acc


def _softplus_neg_abs(z):
    return jnp.log1p(jnp.exp(-jnp.abs(z)))


def _log_sigmoid(z):
    return jnp.minimum(z, 0.0) - _softplus_neg_abs(z)


def _sigmoid(z):
    return 1.0 / (1.0 + jnp.exp(-z))


def _gelu_tanh(y):
    c = math.sqrt(2.0 / math.pi)
    return y * (0.5 * (1.0 + jnp.tanh(c * (y + 0.044715 * (y * y * y)))))


def _rmsnorm_kernel(x_ref, g_ref, o_ref):
    x = x_ref[...]
    ms = jnp.mean(x * x, axis=-1, keepdims=True)
    o_ref[...] = (x * lax.rsqrt(ms + EPS) * g_ref[...]).astype(o_ref.dtype)


def _rmsnorm(x, g, out_dtype, tm=512):
    t, d = x.shape
    return pl.pallas_call(
        _rmsnorm_kernel,
        grid=(t // tm,),
        in_specs=[pl.BlockSpec((tm, d), lambda i: (i, 0)),
                  pl.BlockSpec((1, d), lambda i: (0, 0))],
        out_specs=pl.BlockSpec((tm, d), lambda i: (i, 0)),
        out_shape=jax.ShapeDtypeStruct((t, d), out_dtype),
        compiler_params=_params("parallel"),
        name="rmsnorm",
    )(x, g.reshape(1, d))


def _gate_proj_t_kernel(x_ref, g_ref, wa_ref, wb_ref, u_ref, o_ref):
    x = x_ref[...]
    ms = jnp.mean(x * x, axis=-1, keepdims=True)
    u = x * lax.rsqrt(ms + EPS) * g_ref[...]
    u_ref[...] = u.astype(u_ref.dtype)
    used = wa_ref.shape[0] + wb_ref.shape[0]
    wt = jnp.concatenate([wa_ref[...], wb_ref[...],
                          jnp.zeros((o_ref.shape[1] - used, x.shape[1]), F32)], axis=0)
    uh, ul = _split_bf16(u, 2)
    wh, wl = _split_bf16(wt, 2)
    o_ref[...] = _dot_nt(uh, wh) + _dot_nt(uh, wl) + _dot_nt(ul, wh)


def _gate_proj_t(x, g, wt, layer, *, col_a, n_a, col_b, n_b, tm=256):
    t, d = x.shape
    assert col_a % n_a == 0 and col_b % n_b == 0
    return pl.pallas_call(
        _gate_proj_t_kernel,
        grid=(t // tm,),
        in_specs=[pl.BlockSpec((tm, d), lambda i: (i, 0)),
                  pl.BlockSpec((1, d), lambda i: (0, 0)),
                  pl.BlockSpec((None, n_a, d), lambda i: (layer, col_a // n_a, 0)),
                  pl.BlockSpec((None, n_b, d), lambda i: (layer, col_b // n_b, 0))],
        out_specs=[pl.BlockSpec((tm, d), lambda i: (i, 0)),
                   pl.BlockSpec((tm, LANES), lambda i: (i, 0))],
        out_shape=[jax.ShapeDtypeStruct((t, d), BF16),
                   jax.ShapeDtypeStruct((t, LANES), F32)],
        compiler_params=_params("parallel"),
        name="gate_proj_t",
    )(x, g.reshape(1, d), wt, wt)


def _mm_as_kernel(a_ref, w_ref, o_ref):
    o_ref[...] = _dot(a_ref[...], w_ref[...].astype(BF16)).astype(o_ref.dtype)


def _mm_as_swiglu_kernel(a_ref, wg_ref, wu_ref, o_ref):
    a = a_ref[...]
    g = _dot(a, wg_ref[...].astype(BF16))
    u = _dot(a, wu_ref[...].astype(BF16))
    o_ref[...] = (g * _sigmoid(g) * u).astype(o_ref.dtype)


def _mm_as_res_kernel(a_ref, w_ref, r_ref, o_ref):
    o_ref[...] = r_ref[...] + _dot(a_ref[...], w_ref[...].astype(BF16))


def _a_spec(tm, k):
    return pl.BlockSpec((tm, k), lambda i, j: (i, 0), pipeline_mode=pl.Buffered(1))


def _wt_spec(layer, k, tn):
    return pl.BlockSpec((None, k, tn), lambda i, j: (layer, 0, j))


def _matmul_as(a, w, layer, *, tm, tn, out_dtype):
    m, k = a.shape
    n = w.shape[2]
    return pl.pallas_call(
        _mm_as_kernel,
        grid=(m // tm, n // tn),
        in_specs=[_a_spec(tm, k), _wt_spec(layer, k, tn)],
        out_specs=pl.BlockSpec((tm, tn), lambda i, j: (i, j)),
        out_shape=jax.ShapeDtypeStruct((m, n), out_dtype),
        compiler_params=_params("parallel", "arbitrary"),
        name="matmul_as",
    )(a, w)


def _matmul_as_swiglu(a, wg, wu, layer, *, tm, tn):
    m, k = a.shape
    n = wg.shape[2]
    return pl.pallas_call(
        _mm_as_swiglu_kernel,
        grid=(m // tm, n // tn),
        in_specs=[_a_spec(tm, k), _wt_spec(layer, k, tn), _wt_spec(layer, k, tn)],
        out_specs=pl.BlockSpec((tm, tn), lambda i, j: (i, j)),
        out_shape=jax.ShapeDtypeStruct((m, n), BF16),
        compiler_params=_params("parallel", "arbitrary"),
        name="matmul_as_swiglu",
    )(a, wg, wu)


def _matmul_as_res(a, w, layer, res, *, tm, tn):
    m, k = a.shape
    n = w.shape[2]
    return pl.pallas_call(
        _mm_as_res_kernel,
        grid=(m // tm, n // tn),
        in_specs=[_a_spec(tm, k), _wt_spec(layer, k, tn),
                  pl.BlockSpec((tm, tn), lambda i, j: (i, j))],
        out_specs=pl.BlockSpec((tm, tn), lambda i, j: (i, j)),
        out_shape=jax.ShapeDtypeStruct((m, n), F32),
        compiler_params=_params("parallel", "arbitrary"),
        name="matmul_as_res",
    )(a, w, res)


def _in_proj_as_t_kernel(a_ref, wt_ref, wtx_ref, o_ref, *, n_plain):
    j = pl.program_id(1)
    shift = wtx_ref.shape[0]

    @pl.when(j < n_plain)
    def _():
        o_ref[...] = _dot_nt(a_ref[...], wt_ref[...].astype(BF16)).astype(o_ref.dtype)

    @pl.when(j >= n_plain)
    def _():
        wt = jnp.concatenate([wt_ref[shift:, :], wtx_ref[...]], axis=0).astype(BF16)
        o_ref[...] = _dot_nt(a_ref[...], wt).astype(o_ref.dtype)


def _in_proj_as_t(a, wt, layer, *, tm, tn, n_plain, n_total, shift):
    m, k = a.shape
    assert tn % shift == 0
    kern = functools.partial(_in_proj_as_t_kernel, n_plain=n_plain)
    return pl.pallas_call(
        kern,
        grid=(m // tm, n_total),
        in_specs=[_a_spec(tm, k),
                  pl.BlockSpec((None, tn, k), lambda i, j: (layer, j, 0)),
                  pl.BlockSpec((None, shift, k), lambda i, j: (layer, (j + 1) * (tn // shift), 0))],
        out_specs=pl.BlockSpec((tm, tn), lambda i, j: (i, j)),
        out_shape=jax.ShapeDtypeStruct((m, n_total * tn), BF16),
        compiler_params=_params("parallel", "arbitrary"),
        name="in_proj_as_t",
    )(a, wt, wt)


_CAST_ROWS = 512


def _cast_rows(w_ref, wb_ref):
    k = w_ref.shape[0]

    def body(r, c):
        rows = pl.ds(pl.multiple_of(r * _CAST_ROWS, _CAST_ROWS), _CAST_ROWS)
        wb_ref[rows, :] = w_ref[rows, :].astype(BF16)
        return c

    lax.fori_loop(0, k // _CAST_ROWS, body, 0)


def _mm_res2_kernel(a1_ref, a2_ref, w_ref, r_ref, o_ref, wb_ref):
    @pl.when(pl.program_id(1) == 0)
    def _():
        _cast_rows(w_ref, wb_ref)

    k1 = a1_ref.shape[1]
    o_ref[...] = (r_ref[...] + _dot(a1_ref[...], wb_ref[:k1, :])
                  + _dot(a2_ref[...], wb_ref[k1:, :]))


def _matmul_res2(a1, a2, w, layer, res, *, tm, tn):
    m, k1 = a1.shape
    k2 = a2.shape[1]
    k, n = w.shape[1:]
    assert k == k1 + k2
    return pl.pallas_call(
        _mm_res2_kernel,
        grid=(n // tn, m // tm),
        in_specs=[pl.BlockSpec((tm, k1), lambda j, i: (i, 0)),
                  pl.BlockSpec((tm, k2), lambda j, i: (i, 0)),
                  pl.BlockSpec((None, k, tn), lambda j, i: (layer, 0, j)),
                  pl.BlockSpec((tm, tn), lambda j, i: (i, j))],
        out_specs=pl.BlockSpec((tm, tn), lambda j, i: (i, j)),
        out_shape=jax.ShapeDtypeStruct((m, n), F32),
        scratch_shapes=[pltpu.VMEM((k, tn), BF16)],
        compiler_params=_params("arbitrary", "arbitrary"),
        name="matmul_res2",
    )(a1, a2, w, res)


def _fox_cumsum_kernel(g_ref, b_ref, c_ref, ct_ref, c_sc, *, blk, lane0, heads):
    s = g_ref.shape[0]
    row = lax.broadcasted_iota(jnp.int32, (blk, blk), 0)
    col = lax.broadcasted_iota(jnp.int32, (blk, blk), 1)
    tri = (col <= row).astype(BF16)
    carry = jnp.zeros((1, g_ref.shape[1]), F32)
    for n in range(s // blk):
        rows = slice(n * blk, (n + 1) * blk)
        lf = _log_sigmoid(g_ref[rows, :] + b_ref[...])
        c = _dot_exact_lhs(tri, lf) + carry
        c_sc[rows, :] = c
        carry = c[blk - 1:blk, :]
    c_all = c_sc[...]
    c_ref[...] = c_all
    ct_ref[...] = c_all.T[lane0:lane0 + heads, :]


def _fox_cumsum(gates, bias_row, *, batch, seq, lane0, heads, blk=256):
    t, n = gates.shape
    kern = functools.partial(_fox_cumsum_kernel, blk=blk, lane0=lane0, heads=heads)
    return pl.pallas_call(
        kern,
        grid=(batch,),
        in_specs=[pl.BlockSpec((seq, n), lambda b: (b, 0)),
                  pl.BlockSpec((1, n), lambda b: (0, 0))],
        out_specs=[pl.BlockSpec((seq, n), lambda b: (b, 0)),
                   pl.BlockSpec((None, heads, seq), lambda b: (b, 0, 0))],
        out_shape=[jax.ShapeDtypeStruct((t, n), F32),
                   jax.ShapeDtypeStruct((batch, heads, seq), F32)],
        scratch_shapes=[pltpu.VMEM((seq, n), F32)],
        compiler_params=_params("parallel"),
        name="fox_cumsum",
    )(gates, bias_row)


def _fox_kernel(q_ref, k_ref, v_ref, c_ref, ck_ref, o_ref, *, tq, lane0):
    h = pl.program_id(1)
    for qi in range(q_ref.shape[0] // tq):
        _fox_q_block(qi, h, q_ref, k_ref, v_ref, c_ref, ck_ref, o_ref, tq=tq, lane0=lane0)


def _fox_q_block(qi, h, q_ref, k_ref, v_ref, c_ref, ck_ref, o_ref, *, tq, lane0):
    dh = q_ref.shape[1]
    scale = dh ** -0.5
    q_rows = pl.ds(qi * tq, tq)
    q = q_ref[q_rows, :]
    c_blk = c_ref[q_rows, :]
    lane = lax.broadcasted_iota(jnp.int32, c_blk.shape, 1)
    cq = jnp.sum(jnp.where(lane == lane0 + h, c_blk, 0.0), axis=1, keepdims=True)

    def logits(kb):
        rows = pl.ds(kb * tq, tq)
        s = _dot_nt(q, k_ref[rows, :]) * scale
        return s + cq - ck_ref[kb], v_ref[rows, :]

    s, v = logits(qi)
    tpos = lax.broadcasted_iota(jnp.int32, (tq, tq), 0)
    spos = lax.broadcasted_iota(jnp.int32, (tq, tq), 1)
    s = jnp.where(spos <= tpos, s, NEG_BIG)
    m0 = jnp.max(s, axis=1, keepdims=True)
    p = jnp.exp(s - m0)
    l0 = jnp.sum(p, axis=1, keepdims=True)
    acc0 = _dot(p.astype(BF16), v)

    def body(kb, carry):
        m, l, acc = carry
        s, v = logits(kb)
        m_new = jnp.maximum(m, jnp.max(s, axis=1, keepdims=True))
        alpha = jnp.exp(m - m_new)
        p = jnp.exp(s - m_new)
        l = alpha * l + jnp.sum(p, axis=1, keepdims=True)
        acc = alpha * acc + _dot(p.astype(BF16), v)
        return m_new, l, acc

    carry = (m0, l0, acc0)
    for kb in range(qi):
        carry = body(kb, carry)
    _, l, acc = carry
    o_ref[q_rows, :] = (acc / l).astype(o_ref.dtype)


def _fox_attention(proj, c, ck, *, batch, seq, heads, col0, lane0, tq):
    t = proj.shape[0]
    dh = FOX_HEAD_DIM
    nq = seq // tq
    qb = col0 // dh
    kern = functools.partial(_fox_kernel, tq=tq, lane0=lane0)
    return pl.pallas_call(
        kern,
        grid=(batch, heads),
        in_specs=[pl.BlockSpec((seq, dh), lambda b, h: (b, qb + h)),
                  pl.BlockSpec((seq, dh), lambda b, h: (b, qb + heads + h)),
                  pl.BlockSpec((seq, dh), lambda b, h: (b, qb + 2 * heads + h)),
                  pl.BlockSpec((seq, c.shape[1]), lambda b, h: (b, 0)),
                  pl.BlockSpec((None, None, nq, 1, tq), lambda b, h: (b, h, 0, 0, 0))],
        out_specs=pl.BlockSpec((seq, dh), lambda b, h: (b, h)),
        out_shape=jax.ShapeDtypeStruct((t, heads * dh), BF16),
        compiler_params=_params("parallel", "parallel"),
        name="fox_attention",
    )(proj, proj, proj, c, ck)


def _sb_kernel(q_ref, k_ref, v_ref, o_ref, *, tq, tk):
    for qi in range(q_ref.shape[0] // tq):
        _sb_q_block(qi, q_ref, k_ref, v_ref, o_ref, tq=tq, tk=tk)


def _sb_q_block(qi, q_ref, k_ref, v_ref, o_ref, *, tq, tk):
    dh = q_ref.shape[1]
    scale = dh ** -0.5
    q_rows = pl.ds(qi * tq, tq)
    q = q_ref[q_rows, :]
    nd = tq // tk
    jj = lax.broadcasted_iota(jnp.int32, (tk, tk), 0)
    ss = lax.broadcasted_iota(jnp.int32, (tk, tk), 1)
    upper = (jj > ss).astype(BF16)
    upper2 = jnp.concatenate([upper, upper], axis=0)
    reps = tk // dh

    def block(q_rows_, kb, run, mask):
        rows = pl.ds(kb * tk, tk)
        z = _dot_nt(q_rows_, k_ref[rows, :]) * scale
        sp = jnp.log(1.0 + jnp.exp(-jnp.abs(z)))
        log_beta = jnp.minimum(z, 0.0) - sp
        log_1m = log_beta - z
        if mask is not None:
            log_1m = jnp.where(mask, log_1m, 0.0)
        hi, lo = _split_bf16(log_1m, 2)
        suffix = _dot(jnp.concatenate([hi, lo], axis=1), upper2)
        a = jnp.exp(log_beta + suffix + jnp.concatenate([run] * reps, axis=1))
        if mask is not None:
            a = jnp.where(mask, a, 0.0)
        pv = _dot(a.astype(BF16), v_ref[rows, :])
        rs = jnp.sum(log_1m, axis=1, keepdims=True)
        return pv, jnp.broadcast_to(rs, (rs.shape[0], dh))

    run = acc = None
    for d in range(nd):
        off = (nd - 1 - d) * tk
        n_rows = tq - off
        run_rows = jnp.zeros((n_rows, dh), F32) if run is None else run[off:]
        tri = (lax.broadcasted_iota(jnp.int32, (n_rows, tk), 1)
               < lax.broadcasted_iota(jnp.int32, (n_rows, tk), 0))
        pv, rs = block(q[off:], qi * nd + nd - 1 - d, run_rows, tri)
        if off:
            pv = jnp.concatenate([jnp.zeros((off, dh), F32), pv], axis=0)
            rs = jnp.concatenate([jnp.zeros((off, dh), F32), rs], axis=0)
        acc = pv if acc is None else acc + pv
        run = rs if run is None else run + rs

    for kb in range(qi * nd - 1, -1, -1):
        pv, rs = block(q, kb, run, None)
        acc = acc + pv
        run = run + rs
    o_ref[q_rows, :] = acc.astype(o_ref.dtype)


def _sb_attention(proj, *, batch, seq, heads, col0, tq, tk):
    t = proj.shape[0]
    dh = SB_HEAD_DIM
    qb = col0 // dh
    kern = functools.partial(_sb_kernel, tq=tq, tk=tk)
    return pl.pallas_call(
        kern,
        grid=(batch, heads),
        in_specs=[pl.BlockSpec((seq, dh), lambda b, h: (b, qb + h)),
                  pl.BlockSpec((seq, dh), lambda b, h: (b, qb + heads + h)),
                  pl.BlockSpec((seq, dh), lambda b, h: (b, qb + 2 * heads + h))],
        out_specs=pl.BlockSpec((seq, dh), lambda b, h: (b, h)),
        out_shape=jax.ShapeDtypeStruct((t, heads * dh), BF16),
        compiler_params=_params("parallel", "parallel"),
        name="sb_attention",
    )(proj, proj, proj)


def _gla_kernel(q_ref, k_ref, v_ref, gg_ref, gate_ref, wup_ref, bg_ref, gn_ref, o_ref,
                state_ref, osc_ref, *, rows_blk, chunk, sub):
    state_ref[...] = jnp.zeros_like(state_ref)

    def row_block(r, carry):
        _gla_row_block(pl.multiple_of(r * rows_blk, rows_blk), q_ref, k_ref, v_ref, gg_ref,
                       gate_ref, wup_ref, bg_ref, gn_ref, o_ref, state_ref, osc_ref,
                       rows_blk=rows_blk, chunk=chunk, sub=sub)
        return carry

    lax.fori_loop(0, q_ref.shape[0] // rows_blk, row_block, 0)


def _gla_row_block(base, q_ref, k_ref, v_ref, gg_ref, gate_ref, wup_ref, bg_ref, gn_ref, o_ref,
                   state_ref, osc_ref, *, rows_blk, chunk, sub):
    dk = q_ref.shape[1]
    n_chunks = rows_blk // chunk
    n_sub = chunk // sub
    q_scale = dk ** -0.5
    blk_rows = pl.ds(base, rows_blk)

    z = _dot_f32(gate_ref[blk_rows, :], wup_ref[...]) + bg_ref[...]
    a = _log_sigmoid(z) / GLA_GATE_TAU
    row = lax.broadcasted_iota(jnp.int32, (rows_blk, rows_blk), 0)
    col = lax.broadcasted_iota(jnp.int32, (rows_blk, rows_blk), 1)
    shift = int(math.log2(chunk))
    same = lax.shift_right_logical(row, shift) == lax.shift_right_logical(col, shift)
    tri = ((col <= row) & same).astype(BF16)
    b = _dot_exact_lhs(tri, a)

    key_row = lax.broadcasted_iota(jnp.int32, (chunk, 1), 0)
    key_lane = lax.broadcasted_iota(jnp.int32, (sub, chunk), 1)
    band_row = lax.broadcasted_iota(jnp.int32, (sub, chunk), 0)
    for c in range(n_chunks):
        r0 = c * chunk
        bc = b[r0:r0 + chunk]
        c_rows = pl.ds(pl.multiple_of(base + r0, chunk), chunk)
        qc = q_ref[c_rows, :].astype(F32) * q_scale
        kc = k_ref[c_rows, :].astype(F32)
        vc = v_ref[c_rows, :]
        state = state_ref[...]
        inter = _dot((qc * jnp.exp(bc)).astype(BF16), state.astype(BF16))
        bands = []
        for i in range(n_sub):
            lo, hi = i * sub, (i + 1) * sub
            qs, ks, bs = qc[lo:hi], kc[lo:hi], bc[lo:hi]
            if i == 0:
                s = jnp.zeros((sub, chunk), F32)
            else:
                ref = bc[lo - 1:lo]
                qi = (qs * jnp.exp(bs - ref)).astype(BF16)
                kdec = jnp.exp(jnp.minimum(ref - bc, 0.0))
                ki = jnp.where(key_row < lo, kc * kdec, 0.0).astype(BF16)
                s = _dot_nt(qi, ki)
            for s_ in range(sub):
                dec = jnp.exp(jnp.minimum(bs - bs[s_:s_ + 1], 0.0))
                col = jnp.sum(qs * ks[s_:s_ + 1] * dec, axis=1, keepdims=True)
                s = jnp.where((key_lane == lo + s_) & (band_row >= s_), col, s)
            bands.append(s)
        scores = jnp.concatenate(bands, axis=0)
        osc_ref[r0:r0 + chunk, :] = inter + _dot(scores.astype(BF16), vc)

        bl = bc[chunk - 1:chunk]
        kd = kc * jnp.exp(bl - bc)
        stacked = jnp.concatenate([kd, jnp.broadcast_to(jnp.exp(bl), (chunk, dk))], axis=0)
        st = stacked.T
        vpad = jnp.concatenate([vc, jnp.zeros_like(vc)], axis=0)
        state_ref[...] = state * st[:, chunk:chunk + 1] + _dot(st.astype(BF16), vpad)

    o = osc_ref[...]
    ms = jnp.mean(o * o, axis=-1, keepdims=True)
    y = o * lax.rsqrt(ms + EPS) * gn_ref[...]
    g = gg_ref[blk_rows, :].astype(F32)
    o_ref[blk_rows, :] = (y * (g * _sigmoid(g))).astype(o_ref.dtype)


def _gla(proj, gates, wup_pad, b_gate, g_norm, *, batch, seq, heads, dk, dv, rows_blk=256):
    t = proj.shape[0]
    kb0 = heads
    vb0 = 2 * heads * dk // dv
    gb0 = vb0 + heads
    kern = functools.partial(_gla_kernel, rows_blk=rows_blk, chunk=GLA_CHUNK, sub=GLA_SUB)
    return pl.pallas_call(
        kern,
        grid=(batch, heads),
        in_specs=[pl.BlockSpec((seq, dk), lambda b, h: (b, h)),
                  pl.BlockSpec((seq, dk), lambda b, h: (b, kb0 + h)),
                  pl.BlockSpec((seq, dv), lambda b, h: (b, vb0 + h)),
                  pl.BlockSpec((seq, dv), lambda b, h: (b, gb0 + h)),
                  pl.BlockSpec((seq, gates.shape[1]), lambda b, h: (b, 0)),
                  pl.BlockSpec((wup_pad.shape[0], dk), lambda b, h: (0, h)),
                  pl.BlockSpec((1, dk), lambda b, h: (0, h)),
                  pl.BlockSpec((1, dv), lambda b, h: (0, 0))],
        out_specs=pl.BlockSpec((seq, dv), lambda b, h: (b, h)),
        out_shape=jax.ShapeDtypeStruct((t, heads * dv), BF16),
        scratch_shapes=[pltpu.VMEM((dk, dv), F32),
                        pltpu.VMEM((rows_blk, dv), F32)],
        compiler_params=_params("parallel", "parallel"),
        name="gla",
    )(proj, proj, proj, proj, gates, wup_pad, b_gate, g_norm)


def _rglru_kernel(rx_ref, ry_ref, cw_ref, cb_ref, wx_ref, bx_ref, wa_ref, ba_ref, lam_ref,
                  o_ref, *, rows_t):
    seq, w = rx_ref.shape
    halo = 8
    wxb = wx_ref[...].astype(BF16)
    wab = wa_ref[...].astype(BF16)
    cw = cw_ref[...]
    neg_lam = -lam_ref[...]
    softplus = jnp.maximum(neg_lam, 0.0) + _softplus_neg_abs(neg_lam)
    coef = -RG_C * softplus
    row = lax.broadcasted_iota(jnp.int32, (rows_t, 1), 0)

    def body(t, carry):
        h_prev, tail = carry
        rows = pl.ds(pl.multiple_of(t * rows_t, rows_t), rows_t)
        x = rx_ref[rows, :].astype(F32)
        xe = jnp.concatenate([tail, x], axis=0)
        xc = cb_ref[...] + cw[RG_CONV - 1:RG_CONV] * x
        for s_ in range(1, RG_CONV):
            xc = xc + cw[RG_CONV - 1 - s_:RG_CONV - s_] * pltpu.roll(xe, s_, axis=0)[halo:]
        xcb = xc.astype(BF16)
        gate_x = _sigmoid(_dot(xcb, wxb) + bx_ref[...])
        gate_a = _sigmoid(_dot(xcb, wab) + ba_ref[...])
        log_a = coef * gate_a
        a_dec = jnp.exp(log_a)
        one_m_a2 = -jnp.tanh(log_a) * (a_dec * a_dec + 1.0)
        u = jnp.sqrt(one_m_a2) * (gate_x * xc)
        d = 1
        while d < rows_t:
            if d < halo:
                a_sh = jnp.where(row >= d, pltpu.roll(a_dec, d, axis=0), 1.0)
                u_sh = jnp.where(row >= d, pltpu.roll(u, d, axis=0), 0.0)
            else:
                a_sh = jnp.concatenate([jnp.ones((d, w), F32), a_dec[:rows_t - d]], axis=0)
                u_sh = jnp.concatenate([jnp.zeros((d, w), F32), u[:rows_t - d]], axis=0)
            u = a_dec * u_sh + u
            a_dec = a_dec * a_sh
            d *= 2
        h = u + a_dec * h_prev
        y = ry_ref[rows, :].astype(F32)
        o_ref[rows, :] = (h * _gelu_tanh(y)).astype(o_ref.dtype)
        return h[rows_t - 1:rows_t], x[rows_t - halo:]

    init = (jnp.zeros((1, w), F32), jnp.zeros((halo, w), F32))
    lax.fori_loop(0, seq // rows_t, body, init)


def _rglru(proj, conv_w, conv_b, w_x, b_x, w_a, b_a, lam, *, batch, seq, width, rows_t=256):
    t = proj.shape[0]
    wb = RG_BLOCK
    nb = width // wb
    kern = functools.partial(_rglru_kernel, rows_t=rows_t)
    vec = lambda v: v.reshape(1, width)
    vspec = pl.BlockSpec((1, wb), lambda b, g: (0, g))
    mspec = pl.BlockSpec((None, wb, wb), lambda b, g: (g, 0, 0))
    return pl.pallas_call(
        kern,
        grid=(batch, nb),
        in_specs=[pl.BlockSpec((seq, wb), lambda b, g: (b, nb + g)),
                  pl.BlockSpec((seq, wb), lambda b, g: (b, g)),
                  pl.BlockSpec((RG_CONV, wb), lambda b, g: (0, g)),
                  vspec, mspec, vspec, mspec, vspec, vspec],
        out_specs=pl.BlockSpec((seq, wb), lambda b, g: (b, g)),
        out_shape=jax.ShapeDtypeStruct((t, width), BF16),
        compiler_params=_params("parallel", "parallel"),
        name="rglru",
    )(proj, proj, conv_w, vec(conv_b), w_x, vec(b_x), w_a, vec(b_a), vec(lam))


def _ffn(h, g, w_gate, w_up, w_down, layer):
    u = _rmsnorm(h, g, BF16)
    hid = _matmul_as_swiglu(u, w_gate, w_up, layer, tm=TM_AS, tn=TN_SWIGLU)
    return _matmul_as_res(hid, w_down, layer, h, tm=TM_DOWN, tn=TN_DOWN)


def _even_mixer(h, g, w_in, i, w_gate_up, b_gate, g_norm, b_f, w_out, *, batch, seq):
    d = h.shape[1]
    dk, dv = d // 16, d // 8
    qk, vw, fw = GLA_HEADS * dk, GLA_HEADS * dv, d // 2
    fox_heads = fw // FOX_HEAD_DIM
    rank = GLA_GATE_RANK
    main0 = 2 * qk + 2 * vw
    main1 = 3 * fw
    wt_in = jnp.swapaxes(w_in, 1, 2)
    u, gates = _gate_proj_t(h, g, wt_in, i, col_a=main0, n_a=rank,
                            col_b=main0 + rank + main1, n_b=fox_heads)
    proj = _in_proj_as_t(u, wt_in, i, tm=TM_AS, tn=TN, n_plain=main0 // TN,
                         n_total=(main0 + main1) // TN, shift=rank)
    bias_row = jnp.zeros((1, LANES), F32).at[0, rank:rank + fox_heads].set(b_f)
    c, ct = _fox_cumsum(gates, bias_row, batch=batch, seq=seq, lane0=rank, heads=fox_heads)
    ck = ct.reshape(batch, fox_heads, seq // FOX_TQ, 1, FOX_TQ)
    o_fox = _fox_attention(proj, c, ck, batch=batch, seq=seq, heads=fox_heads, col0=main0,
                           lane0=rank, tq=FOX_TQ)
    wup_pad = jnp.pad(w_gate_up, ((0, LANES - rank), (0, 0)))
    o_gla = _gla(proj, gates, wup_pad, b_gate.reshape(1, qk), g_norm.reshape(1, dv),
                 batch=batch, seq=seq, heads=GLA_HEADS, dk=dk, dv=dv)
    return _matmul_res2(o_gla, o_fox, w_out, i, h, tm=TM_OUT, tn=TN)


def _odd_mixer(h, g, w_in, i, conv_w, conv_b, w_x, b_x, w_a, b_a, lam, w_out, *, batch, seq):
    d = h.shape[1]
    width = d // 2
    sb_heads = width // SB_HEAD_DIM
    u = _rmsnorm(h, g, BF16)
    proj = _matmul_as(u, w_in, i, tm=TM_AS, tn=TN, out_dtype=BF16)
    o_rg = _rglru(proj, conv_w, conv_b, w_x, b_x, w_a, b_a, lam, batch=batch, seq=seq,
                  width=width)
    o_sb = _sb_attention(proj, batch=batch, seq=seq, heads=sb_heads, col0=2 * width,
                         tq=SB_TQ, tk=SB_TK)
    return _matmul_res2(o_rg, o_sb, w_out, i, h, tm=TM_OUT, tn=TN)


def kernel(x, norm_mix, norm_ffn, ffn_w_gate, ffn_w_up, ffn_w_down, ab_w_in, gla_w_gate_up,
           gla_b_gate, gla_norm, fox_b_f, ab_w_out, cd_w_in, rg_conv_w, rg_conv_b, rg_w_x,
           rg_b_x, rg_w_a, rg_b_a, rg_lambda, cd_w_out, final_norm):
    batch, seq, d = x.shape
    h = x.reshape(batch * seq, d)
    for layer in range(norm_mix.shape[0]):
        i = layer // 2
        if layer % 2 == 0:
            h = _even_mixer(h, norm_mix[layer], ab_w_in, i, gla_w_gate_up[i], gla_b_gate[i],
                            gla_norm[i], fox_b_f[i], ab_w_out, batch=batch, seq=seq)
        else:
            h = _odd_mixer(h, norm_mix[layer], cd_w_in, i, rg_conv_w[i], rg_conv_b[i],
                           rg_w_x[i], rg_b_x[i], rg_w_a[i], rg_b_a[i], rg_lambda[i],
                           cd_w_out, batch=batch, seq=seq)
        h = _ffn(h, norm_ffn[layer], ffn_w_gate, ffn_w_up, ffn_w_down, layer)
    return _rmsnorm(h, final_norm, F32).reshape(batch, seq, d)
```
